```python
import math
import jax
import jax.numpy as jnp
from jax import lax
import numpy as np

D_MODEL = 1024
BATCH = 4
SEQ = 4096
DEPTH = 2

N_MEM = 256
BRANCH_W = 256
N_BRANCH = 4
Q_BLOCK = 128
ROPE_THETA = 500000.0
ROPE_FRAC = 4
LN_EPS = 1e-5
ALPHA = (2 * DEPTH) ** 0.25
BETA = (8 * DEPTH) ** -0.25
RET_HEADS = 4
RET_DK = 32
RET_DV = 64
RET_CHUNK = 128
RET_THETA = 10000.0
DSA_HEADS = 4
DSA_DH = 64
IDX_HEADS = 8
IDX_DH = 32
DSA_TOPK = 256
NSA_HEADS = 4
NSA_DH = 64
CMP_LEN = 32
CMP_STRIDE = 16
SEL_LEN = 64
SEL_TOPN = 16
WINDOW = 512
SSM_HEADS = 4
SSM_HEADDIM = 64
SSM_GROUPS = 2
SSM_STATE = 128
SSM_CONV = 4
SSM_CHUNK = 128
SSM_INNER = SSM_HEADS * SSM_HEADDIM
SSM_CONV_DIM = SSM_INNER + 2 * SSM_GROUPS * SSM_STATE
D_FF = 2816
X_HEADS = 4
X_DH = D_MODEL // X_HEADS

IN_SPLITS = (
    RET_HEADS * RET_DK, RET_HEADS * RET_DK, RET_HEADS * RET_DV, RET_HEADS * RET_DV,
    DSA_HEADS * DSA_DH, DSA_DH, DSA_DH, IDX_HEADS * IDX_DH, IDX_DH, IDX_HEADS,
    NSA_HEADS * NSA_DH, NSA_DH, NSA_DH, NSA_DH, NSA_DH, NSA_DH, NSA_DH, NSA_HEADS * 3,
    SSM_INNER, SSM_CONV_DIM, SSM_HEADS,
    N_BRANCH * D_MODEL,
)
D_IN = sum(IN_SPLITS)

kernel_name = 'hybrid_gated_retention_dsa_nsa_ssd_trunk'


def layer_norm(x, g, b):
    xf = x.astype(jnp.float32)
    mu = jnp.mean(xf, -1, keepdims=True)
    var = jnp.mean(jnp.square(xf - mu), -1, keepdims=True)
    return ((xf - mu) * lax.rsqrt(var + LN_EPS)).astype(x.dtype) * g + b


def rope(x, pos, rot_dim, theta):
    half = rot_dim // 2
    inv = jnp.power(jnp.float32(theta), -2.0 * jnp.arange(half, dtype=jnp.float32) / rot_dim)
    ang = pos.astype(jnp.float32)[:, None] * inv[None, :]
    cos = jnp.cos(ang)[None, :, None, :].astype(x.dtype)
    sin = jnp.sin(ang)[None, :, None, :].astype(x.dtype)
    x1, x2, rest = x[..., :half], x[..., half:rot_dim], x[..., rot_dim:]
    return jnp.concatenate([x1 * cos - x2 * sin, x2 * cos + x1 * sin, rest], axis=-1)


def masked_softmax(s, mask):
    s = jnp.where(mask, s.astype(jnp.float32), -jnp.inf)
    m = jnp.max(s, -1, keepdims=True)
    m = jnp.where(jnp.isfinite(m), m, 0.0)
    e = jnp.exp(s - m)
    return e / jnp.maximum(jnp.sum(e, -1, keepdims=True), 1e-30)


def to_blocks(a, nb):
    return jnp.moveaxis(a.reshape(a.shape[0], nb, Q_BLOCK, *a.shape[2:]), 1, 0)


def from_blocks(a):
    a = jnp.moveaxis(a, 0, 1)
    return a.reshape(a.shape[0], a.shape[1] * a.shape[2], *a.shape[3:])


def swiglu(x, w_gu, w_down):
    g, u = jnp.split(x @ w_gu, 2, axis=-1)
    return (jax.nn.silu(g) * u) @ w_down


def retention(q, k, v, g, pos):
    B, S, H, dk = q.shape
    dv = v.shape[-1]
    C = RET_CHUNK
    n = S // C
    dt = v.dtype
    q = rope(q, pos, dk, RET_THETA)
    k = rope(k, pos, dk, RET_THETA) * (dk ** -0.5)
    log_gamma = jnp.log1p(-jnp.exp2(-5.0 - jnp.arange(H, dtype=jnp.float32)))
    c = jnp.arange(C, dtype=jnp.float32)
    rel = c[:, None] - c[None, :]
    intra_decay = jnp.where(rel >= 0, jnp.exp(jnp.maximum(rel, 0.0)[None] * log_gamma[:, None, None]), 0.0).astype(dt)
    zeta = jnp.exp((C - 1 - c)[None] * log_gamma[:, None]).astype(dt)
    xi = jnp.exp((c + 1)[None] * log_gamma[:, None]).astype(dt)
    chunk_decay = jnp.exp(C * log_gamma).astype(dt)
    qc = q.reshape(B, n, C, H, dk)
    kc = k.reshape(B, n, C, H, dk)
    vc = v.reshape(B, n, C, H, dv)
    scores = jnp.einsum('bnihd,bnjhd->bnhij', qc, kc) * intra_decay
    intra = jnp.einsum('bnhij,bnjhe->bnihe', scores, vc)
    contrib = jnp.einsum('bnjhd,bnjhe,hj->nbhde', kc, vc, zeta)

    def step(state, cin):
        return state * chunk_decay[None, :, None, None] + cin, state

    _, prev = lax.scan(step, jnp.zeros((B, H, dk, dv), dt), contrib)
    cross = jnp.einsum('bnihd,nbhde,hi->bnihe', qc, prev, xi)
    o = (intra + cross).reshape(B, S, H, dv).astype(jnp.float32)
    mu = jnp.mean(o, -1, keepdims=True)
    var = jnp.mean(jnp.square(o - mu), -1, keepdims=True)
    o = ((o - mu) * lax.rsqrt(var + LN_EPS)).astype(dt)
    return jax.nn.silu(g) * o.reshape(B, S, H * dv)


def dsa_attention(q, k, v, iq, ik, iw, pos):
    B, S, H, Dh = q.shape
    n_keep = min(DSA_TOPK, S // 4)
    nb = S // Q_BLOCK
    q = rope(q, pos, Dh // ROPE_FRAC, ROPE_THETA)
    k = rope(k[:, :, None], pos, Dh // ROPE_FRAC, ROPE_THETA)[:, :, 0]
    iq = rope(iq, pos, IDX_DH // ROPE_FRAC, ROPE_THETA)
    ik = rope(ik[:, :, None], pos, IDX_DH // ROPE_FRAC, ROPE_THETA)[:, :, 0]
    iw = iw * (IDX_HEADS ** -0.5) * (IDX_DH ** -0.5)
    kpos = jnp.arange(S)
    gather = jax.vmap(lambda t, i: t[i])

    def block(args):
        qb, iqb, iwb, bi = args
        qpos = bi * Q_BLOCK + jnp.arange(Q_BLOCK)
        rel = jax.nn.relu(jnp.einsum('bqhd,bkd->bqkh', iqb, ik))
        score = jnp.einsum('bqkh,bqh->bqk', rel, iwb).astype(jnp.float32)
        causal = kpos[None, :] <= qpos[:, None]
        score = jnp.where(causal[None], score, -jnp.inf)
        _, idx = lax.top_k(score, n_keep)
        ks = gather(k, idx)
        vs = gather(v, idx)
        s = jnp.einsum('bqhd,bqnd->bhqn', qb, ks) * (Dh ** -0.5)
        valid = (idx <= qpos[None, :, None])[:, None]
        p = masked_softmax(s, valid).astype(vs.dtype)
        return jnp.einsum('bhqn,bqnd->bqhd', p, vs)

    out = lax.map(block, (to_blocks(q, nb), to_blocks(iq, nb), to_blocks(iw, nb), jnp.arange(nb)))
    return from_blocks(out).reshape(B, S, H * Dh)


def nsa_attention(q, kc, vc, ks, vs, kw, vw, gates, cmp_w1, cmp_w2, cmp_pos, pos):
    B, S, H, Dh = q.shape
    rot = Dh // ROPE_FRAC
    scale = Dh ** -0.5
    q = rope(q, pos, rot, ROPE_THETA)
    r1 = lambda t: rope(t[:, :, None], pos, rot, ROPE_THETA)[:, :, 0]
    kc, ks, kw = r1(kc), r1(ks), r1(kw)
    n_cmp = (S - CMP_LEN) // CMP_STRIDE + 1
    starts = jnp.arange(n_cmp) * CMP_STRIDE
    tok = starts[:, None] + jnp.arange(CMP_LEN)[None]

    def compress(t, i):
        blk = t[:, tok] + cmp_pos[i]
        h = jax.nn.gelu(blk.reshape(B, n_cmp, CMP_LEN * Dh) @ cmp_w1[i])
        return h @ cmp_w2[i]

    k_cmp = compress(kc, 0)
    v_cmp = compress(vc, 1)
    cmp_visible = (starts + CMP_LEN - 1)[None, :] <= pos[:, None]
    p_cmp = masked_softmax(jnp.einsum('bshd,bcd->bhsc', q, k_cmp) * scale, cmp_visible[None, None])
    o_cmp = jnp.einsum('bhsc,bcd->bshd', p_cmp.astype(v_cmp.dtype), v_cmp)
    n_blk = S // SEL_LEN
    sel_start = jnp.arange(n_blk) * SEL_LEN
    overlap = jnp.maximum(jnp.minimum(starts[:, None] + CMP_LEN, sel_start[None] + SEL_LEN)
                          - jnp.maximum(starts[:, None], sel_start[None]), 0).astype(jnp.float32) / CMP_LEN
    imp = jnp.einsum('bhsc,cj->bsj', p_cmp, overlap)
    blk = jnp.arange(n_blk)[None]
    cur = (pos // SEL_LEN)[:, None]
    forced = (blk == 0) | (blk == cur) | (blk == cur - 1)
    admissible = sel_start[None] <= pos[:, None]
    imp = jnp.where(admissible[None], jnp.where(forced[None], jnp.inf, imp), -jnp.inf)
    n_top = min(SEL_TOPN, n_blk)
    _, sel_idx = lax.top_k(imp, n_top)
    ks_blk = ks.reshape(B, n_blk, SEL_LEN, Dh)
    vs_blk = vs.reshape(B, n_blk, SEL_LEN, Dh)
    kw_pad = jnp.pad(kw, ((0, 0), (WINDOW, 0), (0, 0)))
    vw_pad = jnp.pad(vw, ((0, 0), (WINDOW, 0), (0, 0)))
    nb = S // Q_BLOCK
    gather = jax.vmap(lambda t, i: t[i])
    sel_off = jnp.arange(SEL_LEN)
    win_off = jnp.arange(WINDOW + Q_BLOCK)

    def block(args):
        qb, idxb, bi = args
        qpos = bi * Q_BLOCK + jnp.arange(Q_BLOCK)
        ksel = gather(ks_blk, idxb).reshape(B, Q_BLOCK, n_top * SEL_LEN, Dh)
        vsel = gather(vs_blk, idxb).reshape(B, Q_BLOCK, n_top * SEL_LEN, Dh)
        kpos_sel = (idxb[..., None] * SEL_LEN + sel_off).reshape(B, Q_BLOCK, n_top * SEL_LEN)
        p = masked_softmax(jnp.einsum('bqhd,bqkd->bhqk', qb, ksel) * scale,
                           (kpos_sel <= qpos[None, :, None])[:, None])
        o_sel = jnp.einsum('bhqk,bqkd->bqhd', p.astype(vsel.dtype), vsel)
        kwin = lax.dynamic_slice_in_dim(kw_pad, bi * Q_BLOCK, WINDOW + Q_BLOCK, axis=1)
        vwin = lax.dynamic_slice_in_dim(vw_pad, bi * Q_BLOCK, WINDOW + Q_BLOCK, axis=1)
        kpos_w = bi * Q_BLOCK - WINDOW + win_off
        dlt = qpos[:, None] - kpos_w[None]
        wmask = (dlt >= 0) & (dlt < WINDOW) & (kpos_w[None] >= 0)
        p = masked_softmax(jnp.einsum('bqhd,bkd->bhqk', qb, kwin) * scale, wmask[None, None])
        o_win = jnp.einsum('bhqk,bkd->bqhd', p.astype(vwin.dtype), vwin)
        return o_sel, o_win

    o_sel, o_win = lax.map(block, (to_blocks(q, nb), to_blocks(sel_idx, nb), jnp.arange(nb)))
    o_sel, o_win = from_blocks(o_sel), from_blocks(o_win)
    g = jax.nn.sigmoid(gates.reshape(B, S, H, 3))
    o = g[..., 0:1] * o_cmp + g[..., 1:2] * o_sel + g[..., 2:3] * o_win
    return o.reshape(B, S, H * Dh)


def ssd_mixer(z, xbc, dt_raw, conv_w, conv_b, dt_bias, a_log, d_skip, norm_g):
    B, S, _ = xbc.shape
    H, P, G, N = SSM_HEADS, SSM_HEADDIM, SSM_GROUPS, SSM_STATE
    Q = SSM_CHUNK
    nc = S // Q
    xbc = lax.conv_general_dilated(xbc, conv_w[:, None, :], window_strides=(1,),
                                   padding=[(SSM_CONV - 1, 0)],
                                   dimension_numbers=('NWC', 'WIO', 'NWC'),
                                   feature_group_count=SSM_CONV_DIM)
    xbc = jax.nn.silu(xbc + conv_b)
    xs, bm, cm = jnp.split(xbc, [SSM_INNER, SSM_INNER + G * N], axis=-1)
    dtx = xs.dtype
    xs = xs.reshape(B, S, H, P)
    bm = jnp.repeat(bm.reshape(B, S, G, N), H // G, axis=2)
    cm = jnp.repeat(cm.reshape(B, S, G, N), H // G, axis=2)
    dt = jax.nn.softplus((dt_raw + dt_bias).astype(jnp.float32))
    a = -jnp.exp(a_log.astype(jnp.float32))
    adt = dt * a
    X = (xs * dt[..., None].astype(dtx)).reshape(B, nc, Q, H, P)
    Bc = bm.reshape(B, nc, Q, H, N)
    Cc = cm.reshape(B, nc, Q, H, N)
    A = jnp.moveaxis(adt.reshape(B, nc, Q, H), 3, 1)
    A_cs = jnp.cumsum(A, axis=-1)
    seg = A_cs[..., :, None] - A_cs[..., None, :]
    tril = jnp.tril(jnp.ones((Q, Q), dtype=bool))
    Lm = jnp.exp(jnp.where(tril, seg, -jnp.inf)).astype(dtx)
    y_diag = jnp.einsum('bclhn,bcshn,bhcls,bcshp->bclhp', Cc, Bc, Lm, X)
    decay_states = jnp.exp(A_cs[..., -1:] - A_cs).astype(dtx)
    states = jnp.einsum('bclhn,bhcl,bclhp->cbhpn', Bc, decay_states, X)
    chunk_decay = jnp.moveaxis(jnp.exp(A_cs[..., -1]).astype(dtx), 2, 0)

    def step(h, inp):
        st, dec = inp
        return h * dec[..., None, None] + st, h

    _, prev = lax.scan(step, jnp.zeros((B, H, P, N), dtx), (states, chunk_decay))
    y_off = jnp.einsum('bclhn,cbhpn,bhcl->bclhp', Cc, prev, jnp.exp(A_cs).astype(dtx))
    y = (y_diag + y_off).reshape(B, S, H, P) + xs * d_skip[:, None]
    y = y.reshape(B, S, SSM_INNER) * jax.nn.silu(z)
    yg = y.reshape(B, S, G, SSM_INNER // G).astype(jnp.float32)
    yg = yg * lax.rsqrt(jnp.mean(yg * yg, -1, keepdims=True) + LN_EPS)
    return yg.reshape(B, S, SSM_INNER).astype(z.dtype) * norm_g


def token_mixing(x, w_in, cmp_w1, cmp_w2, cmp_pos, conv_w, conv_b, dt_bias, a_log, d_skip,
                 ssm_norm_g, w_branch, w_out):
    B, S, D = x.shape
    pos = jnp.arange(S)
    parts = jnp.split(x @ w_in, np.cumsum(IN_SPLITS)[:-1].tolist(), axis=-1)
    (r_q, r_k, r_v, r_g, d_q, d_k, d_v, i_q, i_k, i_w,
     n_q, n_kc, n_vc, n_ks, n_vs, n_kw, n_vw, n_g, s_z, s_xbc, s_dt, br_g) = parts
    y_ret = retention(r_q.reshape(B, S, RET_HEADS, RET_DK), r_k.reshape(B, S, RET_HEADS, RET_DK),
                      r_v.reshape(B, S, RET_HEADS, RET_DV), r_g, pos)
    y_dsa = dsa_attention(d_q.reshape(B, S, DSA_HEADS, DSA_DH), d_k, d_v,
                          i_q.reshape(B, S, IDX_HEADS, IDX_DH), i_k, i_w, pos)
    y_nsa = nsa_attention(n_q.reshape(B, S, NSA_HEADS, NSA_DH), n_kc, n_vc, n_ks, n_vs, n_kw, n_vw,
                          n_g, cmp_w1, cmp_w2, cmp_pos, pos)
    y_ssd = ssd_mixer(s_z, s_xbc, s_dt, conv_w, conv_b, dt_bias, a_log, d_skip, ssm_norm_g)
    ys = jnp.stack([y_ret, y_dsa, y_nsa, y_ssd], axis=2)
    gates = jax.nn.sigmoid(br_g.reshape(B, S, N_BRANCH, D))
    merged = jnp.einsum('bsnd,bsnd->bsd', gates, jnp.einsum('bsnw,nwd->bsnd', ys, w_branch))
    return merged @ w_out


def memory_cross_attention(x, mem, wq, wkv, wo):
    B, S, D = x.shape
    M = mem.shape[1]
    q = (x @ wq).reshape(B, S, X_HEADS, X_DH)
    kv = (mem @ wkv).reshape(B, M, 2, X_HEADS, X_DH)
    k, v = kv[:, :, 0], kv[:, :, 1]
    s = jnp.einsum('bshd,bmhd->bhsm', q, k) * (X_DH ** -0.5)
    p = jax.nn.softmax(s.astype(jnp.float32), axis=-1).astype(v.dtype)
    return jnp.einsum('bhsm,bmhd->bshd', p, v).reshape(B, S, D) @ wo


def setup_inputs(seed: int = 0) -> dict:
    key = jax.random.key(seed)
    ks = jax.random.split(key, 23)
    L, D = DEPTH, D_MODEL
    nrm = lambda k, shape, scale: jax.random.normal(k, shape, jnp.float32) * scale
    dt0 = jnp.exp(jax.random.uniform(ks[12], (L, SSM_HEADS), jnp.float32, math.log(1e-3), math.log(1e-1)))
    return {
        'x': nrm(ks[0], (BATCH, SEQ, D), 1.0),
        'mem': nrm(ks[1], (BATCH, N_MEM, D), 1.0),
        'ln_g': 1.0 + nrm(ks[2], (L, 4, D), 0.02),
        'ln_b': nrm(ks[3], (L, 4, D), 0.02),
        'ffn1_w_gu': nrm(ks[4], (L, D, 2 * D_FF), D ** -0.5),
        'ffn1_w_down': nrm(ks[5], (L, D_FF, D), BETA * D_FF ** -0.5),
        'w_in': nrm(ks[6], (L, D, D_IN), D ** -0.5),
        'cmp_w1': nrm(ks[7], (L, 2, CMP_LEN * NSA_DH, NSA_DH), (CMP_LEN * NSA_DH) ** -0.5),
        'cmp_w2': nrm(ks[8], (L, 2, NSA_DH, NSA_DH), NSA_DH ** -0.5),
        'cmp_pos': nrm(ks[9], (L, 2, CMP_LEN, NSA_DH), 0.1),
        'conv_w': nrm(ks[10], (L, SSM_CONV, SSM_CONV_DIM), SSM_CONV ** -0.5),
        'conv_b': nrm(ks[11], (L, SSM_CONV_DIM), 0.01),
        'dt_bias': dt0 + jnp.log(-jnp.expm1(-dt0)),
        'a_log': jnp.log(jax.random.uniform(ks[13], (L, SSM_HEADS), jnp.float32, 1.0, 16.0)),
        'd_skip': 1.0 + nrm(ks[14], (L, SSM_HEADS), 0.1),
        'ssm_norm_g': 1.0 + nrm(ks[15], (L, SSM_INNER), 0.02),
        'w_branch': nrm(ks[16], (L, N_BRANCH, BRANCH_W, D), BRANCH_W ** -0.5),
        'w_out': nrm(ks[17], (L, D, D), BETA * D ** -0.5),
        'xattn_wq': nrm(ks[18], (L, D, D), D ** -0.5),
        'xattn_wkv': nrm(ks[19], (L, D, 2 * D), D ** -0.5),
        'xattn_wo': nrm(ks[20], (L, D, D), BETA * D ** -0.5),
        'ffn2_w_gu': nrm(ks[21], (L, D, 2 * D_FF), D ** -0.5),
        'ffn2_w_down': nrm(ks[22], (L, D_FF, D), BETA * D_FF ** -0.5),
    }


def reference(x, mem, ln_g, ln_b, ffn1_w_gu, ffn1_w_down, w_in, cmp_w1, cmp_w2, cmp_pos,
              conv_w, conv_b, dt_bias, a_log, d_skip, ssm_norm_g, w_branch, w_out,
              xattn_wq, xattn_wkv, xattn_wo, ffn2_w_gu, ffn2_w_down):
    for l in range(DEPTH):
        x = layer_norm(ALPHA * x + 0.5 * swiglu(x, ffn1_w_gu[l], ffn1_w_down[l]), ln_g[l, 0], ln_b[l, 0])
        mix = token_mixing(x, w_in[l], cmp_w1[l], cmp_w2[l], cmp_pos[l], conv_w[l], conv_b[l],
                           dt_bias[l], a_log[l], d_skip[l], ssm_norm_g[l], w_branch[l], w_out[l])
        x = layer_norm(ALPHA * x + mix, ln_g[l, 1], ln_b[l, 1])
        x = layer_norm(ALPHA * x + memory_cross_attention(x, mem, xattn_wq[l], xattn_wkv[l], xattn_wo[l]),
                       ln_g[l, 2], ln_b[l, 2])
        x = layer_norm(ALPHA * x + 0.5 * swiglu(x, ffn2_w_gu[l], ffn2_w_down[l]), ln_g[l, 3], ln_b[l, 3])
    return x
```

```python
import functools
import math

import numpy as np
import jax
import jax.numpy as jnp
from jax import lax
from jax.experimental import pallas as pl
from jax.experimental.pallas import tpu as pltpu

F32 = jnp.float32
BF = jnp.bfloat16
NEG_INF = float("-inf")

Q_BLOCK = 128
ROPE_THETA = 500000.0
ROPE_FRAC = 4
LN_EPS = 1e-5
RET_HEADS, RET_DK, RET_DV, RET_CHUNK, RET_THETA = 4, 32, 64, 128, 10000.0
DSA_HEADS, DSA_DH, IDX_HEADS, IDX_DH, DSA_TOPK = 4, 64, 8, 32, 256
NSA_HEADS, NSA_DH, CMP_LEN, CMP_STRIDE, SEL_LEN, SEL_TOPN, WINDOW = 4, 64, 32, 16, 64, 16, 512
SSM_HEADS, SSM_HEADDIM, SSM_GROUPS, SSM_STATE, SSM_CONV, SSM_CHUNK = 4, 64, 2, 128, 4, 128
SSM_INNER = SSM_HEADS * SSM_HEADDIM
SSM_CONV_DIM = SSM_INNER + 2 * SSM_GROUPS * SSM_STATE
N_BRANCH = 4
BRANCH_W = 256
X_HEADS = 4

LANE = 128
CONV_PAD = 8
VMEM_LIMIT = 56 * 1024 * 1024

_SEG_NAMES = ("r_q", "r_k", "r_v", "r_g", "d_q", "d_k", "d_v", "i_q", "i_k", "i_w",
              "n_q", "n_kc", "n_vc", "n_ks", "n_vs", "n_kw", "n_vw", "n_g", "s_z", "s_xbc", "s_dt")
_SEG_WIDTHS = (RET_HEADS * RET_DK, RET_HEADS * RET_DK, RET_HEADS * RET_DV, RET_HEADS * RET_DV,
               DSA_HEADS * DSA_DH, DSA_DH, DSA_DH, IDX_HEADS * IDX_DH, IDX_DH, IDX_HEADS,
               NSA_HEADS * NSA_DH, NSA_DH, NSA_DH, NSA_DH, NSA_DH, NSA_DH, NSA_DH, NSA_HEADS * 3,
               SSM_INNER, SSM_CONV_DIM, SSM_HEADS)
_SEG = {}
_o = 0
for _n, _w in zip(_SEG_NAMES, _SEG_WIDTHS):
    _SEG[_n] = (_o, _w)
    _o += _w
GATE_START = _o

_GROUPS = (
    ("ret", ("r_q", "r_k", "r_v", "r_g"), 768),
    ("dsa", ("d_q", "i_q", "d_k", "d_v", "i_k", "i_w"), 768),
    ("nsa", ("n_q", "n_kc", "n_vc", "n_ks", "n_vs", "n_kw", "n_vw", "n_g"), 768),
    ("ssd", ("s_z", "s_xbc", "s_dt"), 1152),
)
_ROPE = {
    "r_q": (RET_DK, RET_DK, 0), "r_k": (RET_DK, RET_DK, 1),
    "d_q": (DSA_DH, DSA_DH // ROPE_FRAC, 2), "n_q": (NSA_DH, NSA_DH // ROPE_FRAC, 2),
    "d_k": (DSA_DH, DSA_DH // ROPE_FRAC, 3), "n_kc": (NSA_DH, NSA_DH // ROPE_FRAC, 3),
    "n_ks": (NSA_DH, NSA_DH // ROPE_FRAC, 3), "n_kw": (NSA_DH, NSA_DH // ROPE_FRAC, 3),
    "i_q": (IDX_DH, IDX_DH // ROPE_FRAC, 4), "i_k": (IDX_DH, IDX_DH // ROPE_FRAC, 5),
}
N_TABLES = 6


def _build_layout():
    main_cols, rot_cols, rot_sign, tile_table = [], [], [], []
    group_tiles = []
    for _, segs, width in _GROUPS:
        cols, rsrc, rsgn = [], [], []
        for s in segs:
            start, w = _SEG[s]
            for j in range(w):
                cols.append(start + j)
                if s in _ROPE:
                    hd, rot, _ = _ROPE[s]
                    jj, half = j % hd, rot // 2
                    if jj < half:
                        rsrc.append(start + j + half); rsgn.append(-1.0)
                    elif jj < rot:
                        rsrc.append(start + j - half); rsgn.append(1.0)
                    else:
                        rsrc.append(0); rsgn.append(0.0)
                else:
                    rsrc.append(0); rsgn.append(0.0)
        pad = width - len(cols)
        cols += [-1] * pad; rsrc += [0] * pad; rsgn += [0.0] * pad
        main_cols += cols; rot_cols += rsrc; rot_sign += rsgn
        group_tiles.append(width // LANE)
    main_cols = np.array(main_cols); rot_cols = np.array(rot_cols); rot_sign = np.array(rot_sign, np.float32)
    n_tiles = len(main_cols) // LANE
    roped_tiles = [t for t in range(n_tiles) if np.any(rot_sign[t * LANE:(t + 1) * LANE] != 0)]
    col_seg = {}
    for s, (start, w) in _SEG.items():
        for j in range(w):
            col_seg[start + j] = s
    for t in roped_tiles:
        tile_table.append(_ROPE[col_seg[int(main_cols[t * LANE])]][2])
    keep = np.concatenate([np.arange(t * LANE, (t + 1) * LANE) for t in roped_tiles])
    return main_cols, rot_cols[keep], rot_sign[keep], roped_tiles, tile_table, group_tiles


_MAIN_COLS, _ROT_COLS, _ROT_SIGN, _ROPED_TILES, _TILE_TABLE, _GROUP_TILES = _build_layout()
N_MAIN = len(_MAIN_COLS)
N_ROT = len(_ROT_COLS)


def _dot(a, b):
    return jnp.dot(a, b, preferred_element_type=F32)


def _dot_nt(a, b):
    return lax.dot_general(a, b, (((1,), (1,)), ((), ())), preferred_element_type=F32)


def _dot_tn(a, b):
    return lax.dot_general(a, b, (((0,), (0,)), ((), ())), preferred_element_type=F32)


def _split3(a):
    hi = a.astype(BF)
    r1 = a - hi.astype(F32)
    mid = r1.astype(BF)
    lo = (r1 - mid.astype(F32)).astype(BF)
    return hi, mid, lo


def _layer_norm(v, g, b):
    mu = jnp.mean(v, -1, keepdims=True)
    d = v - mu
    var = jnp.mean(d * d, -1, keepdims=True)
    return d * lax.rsqrt(var + LN_EPS) * g + b


def _silu(v):
    return v * jax.nn.sigmoid(v)


def _params(sem):
    return pltpu.CompilerParams(dimension_semantics=sem, vmem_limit_bytes=VMEM_LIMIT)


def _ffn_kernel(x_ref, wg_ref, wu_ref, wd_ref, g_ref, b_ref, o_ref, acc_ref, *, alpha):
    j = pl.program_id(1)

    @pl.when(j == 0)
    def _():
        acc_ref[...] = jnp.zeros_like(acc_ref)

    xb = x_ref[...].astype(BF)
    gate = _dot(xb, wg_ref[...])
    up = _dot(xb, wu_ref[...])
    h = (_silu(gate) * up).astype(BF)
    acc_ref[...] += _dot(h, wd_ref[...])

    @pl.when(j == pl.num_programs(1) - 1)
    def _():
        o_ref[...] = _layer_norm(alpha * x_ref[...] + 0.5 * acc_ref[...], g_ref[...], b_ref[...])


def _ffn(x, w_gu, w_down, g, b, alpha, tm=512):
    T, D = x.shape
    F = w_down.shape[0]
    fc = F // 2 if (F // 2) % LANE == 0 else F
    nf = F // fc
    return pl.pallas_call(
        functools.partial(_ffn_kernel, alpha=alpha),
        grid=(T // tm, nf),
        in_specs=[
            pl.BlockSpec((tm, D), lambda i, j: (i, 0)),
            pl.BlockSpec((D, fc), lambda i, j: (0, j)),
            pl.BlockSpec((D, fc), lambda i, j: (0, j + nf)),
            pl.BlockSpec((fc, D), lambda i, j: (j, 0)),
            pl.BlockSpec((1, D), lambda i, j: (0, 0)),
            pl.BlockSpec((1, D), lambda i, j: (0, 0)),
        ],
        out_specs=pl.BlockSpec((tm, D), lambda i, j: (i, 0)),
        out_shape=jax.ShapeDtypeStruct((T, D), F32),
        scratch_shapes=[pltpu.VMEM((tm, D), F32)],
        compiler_params=_params(("parallel", "arbitrary")),
        name="ffn",
    )(x, w_gu, w_gu, w_down, g, b)


AUX_TILE = 5


def _inproj_kernel(x_ref, wm_ref, wr_ref, cos_ref, sin_ref, o_ret, o_dsa, o_nsa, o_ssd, o_dsa_aux, o_nsa_aux):
    xb = x_ref[...].astype(BF)
    yr = _dot(xb, wr_ref[...])
    outs = (o_ret, o_dsa, o_nsa, o_ssd)
    aux = (None, o_dsa_aux, o_nsa_aux, None)
    t0 = 0
    for o_ref, aux_ref, nt in zip(outs, aux, _GROUP_TILES):
        y = _dot(xb, wm_ref[:, t0 * LANE:(t0 + nt) * LANE])
        for t in range(nt):
            yt = y[:, t * LANE:(t + 1) * LANE]
            gt = t0 + t
            if gt in _ROPED_TILES:
                r = _ROPED_TILES.index(gt)
                tab = _TILE_TABLE[r]
                yt = (yt * cos_ref[:, tab * LANE:(tab + 1) * LANE]
                      + yr[:, r * LANE:(r + 1) * LANE] * sin_ref[:, tab * LANE:(tab + 1) * LANE])
            o_ref[:, t * LANE:(t + 1) * LANE] = yt.astype(o_ref.dtype)
            if aux_ref is not None and t == AUX_TILE:
                aux_ref[...] = yt
        t0 += nt


def _inproj(x, wm, wr, cos_t, sin_t, S, tm=256):
    T, D = x.shape
    ns = S // tm
    widths = [nt * LANE for nt in _GROUP_TILES] + [LANE, LANE]
    dtypes = [F32, BF, BF, F32, F32, F32]
    return pl.pallas_call(
        _inproj_kernel,
        grid=(T // tm,),
        in_specs=[
            pl.BlockSpec((tm, D), lambda i: (i, 0)),
            pl.BlockSpec((D, N_MAIN), lambda i: (0, 0)),
            pl.BlockSpec((D, N_ROT), lambda i: (0, 0)),
            pl.BlockSpec((tm, N_TABLES * LANE), lambda i: (i % ns, 0)),
            pl.BlockSpec((tm, N_TABLES * LANE), lambda i: (i % ns, 0)),
        ],
        out_specs=[pl.BlockSpec((tm, w), lambda i: (i, 0)) for w in widths],
        out_shape=[jax.ShapeDtypeStruct((T, w), dt) for w, dt in zip(widths, dtypes)],
        compiler_params=_params(("parallel",)),
        name="in_proj",
    )(x, wm, wr, cos_t, sin_t)


def _ret_kernel(r_ref, o_ref, st_ref):
    C = RET_CHUNK

    @pl.when(pl.program_id(1) == 0)
    def _():
        st_ref[...] = jnp.zeros_like(st_ref)

    r = r_ref[...]
    hq = RET_HEADS * RET_DK
    q, k = r[:, :hq], r[:, hq:2 * hq]
    v = r[:, 2 * hq:2 * hq + RET_HEADS * RET_DV]
    g = r[:, 2 * hq + RET_HEADS * RET_DV:]
    rel = (lax.broadcasted_iota(jnp.int32, (C, C), 0) - lax.broadcasted_iota(jnp.int32, (C, C), 1)).astype(F32)
    row = lax.broadcasted_iota(jnp.int32, (C, 1), 0).astype(F32)
    outs = []
    for h in range(RET_HEADS):
        lg = math.log1p(-(2.0 ** (-5.0 - h)))
        decay = jnp.where(rel >= 0, jnp.exp(jnp.maximum(rel, 0.0) * lg), 0.0)
        zeta = jnp.exp((C - 1 - row) * lg)
        xi = jnp.exp((row + 1.0) * lg)
        qh = q[:, h * RET_DK:(h + 1) * RET_DK].astype(BF)
        kh = k[:, h * RET_DK:(h + 1) * RET_DK]
        vh = v[:, h * RET_DV:(h + 1) * RET_DV].astype(BF)
        sc = _dot_nt(qh, kh.astype(BF)) * decay
        intra = _dot(sc.astype(BF), vh)
        prev = st_ref[h]
        cross = _dot(qh, prev.astype(BF)) * xi
        st_ref[h] = prev * math.exp(C * lg) + _dot_tn((kh * zeta).astype(BF), vh)
        o = intra + cross
        mu = jnp.mean(o, -1, keepdims=True)
        d = o - mu
        var = jnp.mean(d * d, -1, keepdims=True)
        outs.append(d * lax.rsqrt(var + LN_EPS))
    o_ref[...] = _silu(g) * jnp.concatenate(outs, axis=-1)


def _retention(ret, B, S):
    n = S // RET_CHUNK
    W = ret.shape[1]
    return pl.pallas_call(
        _ret_kernel,
        grid=(B, n),
        in_specs=[pl.BlockSpec((RET_CHUNK, W), lambda b, c: (b * n + c, 0))],
        out_specs=pl.BlockSpec((RET_CHUNK, BRANCH_W), lambda b, c: (b * n + c, 0)),
        out_shape=jax.ShapeDtypeStruct((B * S, BRANCH_W), F32),
        scratch_shapes=[pltpu.VMEM((RET_HEADS, RET_DK, RET_DV), F32)],
        compiler_params=_params(("parallel", "arbitrary")),
        name="retention",
    )(ret)


def _softplus(v):
    return jnp.maximum(v, 0.0) + jnp.log1p(jnp.exp(-jnp.abs(v)))


def _ssd_kernel(s_ref, dtr_ref, cw_ref, cb_ref, dtb_c_ref, a_c_ref, dtb_r_ref, a_r_ref, dsk_ref, ng_ref,
                o_ref, st_ref, xpad_ref):
    Q, H, P, N = SSM_CHUNK, SSM_HEADS, SSM_HEADDIM, SSM_STATE
    c = pl.program_id(1)

    @pl.when(c == 0)
    def _():
        st_ref[...] = jnp.zeros_like(st_ref)
        xpad_ref[0:CONV_PAD, :] = jnp.zeros((CONV_PAD, SSM_CONV_DIM), F32)

    z = s_ref[:, :SSM_INNER]
    xpad_ref[CONV_PAD:, :] = s_ref[:, SSM_INNER:SSM_INNER + SSM_CONV_DIM]
    dt_raw = s_ref[:, SSM_INNER + SSM_CONV_DIM:SSM_INNER + SSM_CONV_DIM + H]
    conv = cb_ref[...]
    for kk in range(SSM_CONV):
        off = CONV_PAD - (SSM_CONV - 1) + kk
        conv = conv + cw_ref[kk:kk + 1, :] * xpad_ref[off:off + Q, :]
    xpad_ref[0:CONV_PAD, :] = xpad_ref[Q:Q + CONV_PAD, :]
    xc = _silu(conv)
    xs = xc[:, :SSM_INNER]
    gn = SSM_GROUPS * N
    bm = xc[:, SSM_INNER:SSM_INNER + gn]
    cm = xc[:, SSM_INNER + gn:]

    dt_c = _softplus(dt_raw + dtb_c_ref[...])
    adt_c = dt_c * a_c_ref[...]
    dt_r = _softplus(dtr_ref[...] + dtb_r_ref[...])
    adt_r = dt_r * a_r_ref[...]
    ri = lax.broadcasted_iota(jnp.int32, (Q, Q), 0)
    ci = lax.broadcasted_iota(jnp.int32, (Q, Q), 1)
    tril = ri >= ci
    lo_tri = jnp.where(tril, 1.0, 0.0).astype(BF)
    up_tri = jnp.where(ci >= ri, 1.0, 0.0).astype(BF)
    acs_c = sum(_dot(lo_tri, t) for t in _split3(adt_c))
    acs_r = sum(_dot(t, up_tri) for t in _split3(adt_r))

    outs = []
    for grp in range(SSM_GROUPS):
        bg = bm[:, grp * N:(grp + 1) * N]
        cg = cm[:, grp * N:(grp + 1) * N].astype(BF)
        cb = _dot_nt(cg, bg.astype(BF))
        for h in range(grp * (H // SSM_GROUPS), (grp + 1) * (H // SSM_GROUPS)):
            a_col = acs_c[:, h:h + 1]
            a_last = acs_c[Q - 1:Q, h:h + 1]
            lm = jnp.exp(jnp.where(tril, a_col - acs_r[h:h + 1, :], NEG_INF))
            xh = xs[:, h * P:(h + 1) * P]
            xdt = (xh * dt_c[:, h:h + 1]).astype(BF)
            y = _dot((cb * lm).astype(BF), xdt)
            prev = st_ref[h]
            y = y + _dot(cg, prev.astype(BF)) * jnp.exp(a_col)
            st_ref[h] = prev * jnp.exp(a_last) + _dot_tn((bg * jnp.exp(a_last - a_col)).astype(BF), xdt)
            outs.append(y)
    y = jnp.concatenate(outs, axis=-1) + xs * dsk_ref[...]
    y = y * _silu(z)
    gw = SSM_INNER // SSM_GROUPS
    normed = []
    for grp in range(SSM_GROUPS):
        yg = y[:, grp * gw:(grp + 1) * gw]
        normed.append(yg * lax.rsqrt(jnp.mean(yg * yg, -1, keepdims=True) + LN_EPS))
    o_ref[...] = jnp.concatenate(normed, axis=-1) * ng_ref[...]


def _ssd(ssd, dt_rows, conv_w, conv_b, dt_bias, a_log, d_skip, norm_g, B, S):
    Q, H = SSM_CHUNK, SSM_HEADS
    n = S // Q
    W = ssd.shape[1]
    a = -jnp.exp(a_log.astype(F32))
    pad_r = lambda v: jnp.broadcast_to(jnp.pad(v, (0, CONV_PAD - H))[:, None], (CONV_PAD, Q)).astype(F32)
    full = lambda shape: pl.BlockSpec(shape, lambda b, c: (0,) * len(shape))
    return pl.pallas_call(
        _ssd_kernel,
        grid=(B, n),
        in_specs=[
            pl.BlockSpec((Q, W), lambda b, c: (b * n + c, 0)),
            pl.BlockSpec((None, None, CONV_PAD, Q), lambda b, c: (b, c, 0, 0)),
            full((SSM_CONV, SSM_CONV_DIM)), full((1, SSM_CONV_DIM)),
            full((1, H)), full((1, H)), full((CONV_PAD, Q)), full((CONV_PAD, Q)),
            full((1, SSM_INNER)), full((1, SSM_INNER)),
        ],
        out_specs=pl.BlockSpec((Q, BRANCH_W), lambda b, c: (b * n + c, 0)),
        out_shape=jax.ShapeDtypeStruct((B * S, BRANCH_W), F32),
        scratch_shapes=[pltpu.VMEM((H, SSM_STATE, SSM_HEADDIM), F32),
                        pltpu.VMEM((Q + CONV_PAD, SSM_CONV_DIM), F32)],
        compiler_params=_params(("parallel", "arbitrary")),
        name="ssd",
    )(ssd, dt_rows, conv_w, conv_b[None, :], dt_bias[None, :], a[None, :], pad_r(dt_bias), pad_r(a),
      jnp.repeat(d_skip, SSM_HEADDIM)[None, :], norm_g[None, :])


def _masked_attention(q_bf, k_bf, v_bf, mask, heads, dh):
    outs = []
    for h in range(heads):
        s = jnp.where(mask, _dot_nt(q_bf[:, h * dh:(h + 1) * dh], k_bf), NEG_INF)
        m = jnp.max(s, -1, keepdims=True)
        m = jnp.where(m > NEG_INF, m, 0.0)
        e = jnp.exp(s - m)
        l = jnp.maximum(jnp.sum(e, -1, keepdims=True), 1e-30)
        outs.append(_dot(e.astype(BF), v_bf) / l)
    return jnp.concatenate(outs, axis=-1)


def _count(m):
    return jnp.sum(jnp.where(m, 1.0, 0.0), -1, keepdims=True)


CAUSAL_SEG = 1024
BISECT_PLAIN_STEPS = 26
BISECT_MAX_STEPS = 400


def _kth_threshold(sc, nvalid, vmin, vmax, k):
    kf = float(k)
    lo0 = jnp.where(nvalid >= kf, vmin, NEG_INF)
    done0 = jnp.where(nvalid <= kf, 1.0, 0.0)

    def split(lo, hi, done):
        mid = jnp.where(done > 0.5, lo, lo + (hi - lo) * 0.5)
        c = _count(sc >= mid)
        up = (c >= kf) & (done < 0.5)
        return jnp.where(up, mid, lo), jnp.where(up | (done > 0.5), hi, mid), up & (c == kf)

    def plain_cond(st):
        return (st[0] < BISECT_PLAIN_STEPS) & (jnp.min(st[3]) < 0.5)

    def plain_step(st):
        it, lo, hi, done = st
        lo, hi, hit = split(lo, hi, done)
        return it + 1, lo, hi, jnp.where(hit, 1.0, done)

    _, lo, hi, done = lax.while_loop(plain_cond, plain_step, (jnp.int32(0), lo0, vmax, done0))

    def exact_cond(st):
        return (st[0] < BISECT_MAX_STEPS) & (jnp.min(st[5]) < 0.5)

    def exact_step(st):
        it, lo, hi, thr, found, done = st
        cand = jnp.min(jnp.where(sc >= lo, sc, jnp.inf), -1, keepdims=True)
        fin = (_count(sc > cand) < kf) & (done < 0.5)
        thr = jnp.where(fin, cand, thr)
        found = jnp.where(fin, 1.0, found)
        done = jnp.where(fin, 1.0, done)
        lo, hi, hit = split(lo, hi, done)
        return it + 1, lo, hi, thr, found, jnp.where(hit, 1.0, done)

    _, lo, _, thr, found, _ = lax.while_loop(
        exact_cond, exact_step, (jnp.int32(0), lo, hi, lo, jnp.zeros_like(done), done))
    return jnp.where(found > 0.5, thr, lo)


def _dsa_body(q_ref, aux_ref, kv_ref, ik_ref, o_ref, *, n_keep, E):
    Qb = Q_BLOCK
    bi = pl.program_id(1)
    qpos = bi * Qb + lax.broadcasted_iota(jnp.int32, (Qb, 1), 0)
    kpos = lax.broadcasted_iota(jnp.int32, (1, E), 1)
    causal = kpos <= qpos
    hq = DSA_HEADS * DSA_DH
    iw = aux_ref[:, IDX_DH:IDX_DH + IDX_HEADS] * (IDX_HEADS ** -0.5) * (IDX_DH ** -0.5)
    ik = ik_ref[0:E, :IDX_DH]
    score = jnp.zeros((Qb, E), F32)
    for h in range(IDX_HEADS):
        rel = jnp.maximum(_dot_nt(q_ref[:, hq + h * IDX_DH:hq + (h + 1) * IDX_DH], ik), 0.0)
        score = score + rel * iw[:, h:h + 1]
    score = jnp.where(score == 0.0, 0.0, score)
    sc = jnp.where(causal, score, NEG_INF)
    vmax = jnp.max(sc, -1, keepdims=True)
    vmin = jnp.min(jnp.where(causal, score, jnp.inf), -1, keepdims=True)
    thr = _kth_threshold(sc, (qpos + 1).astype(F32), vmin, vmax, n_keep)
    gt = sc > thr
    eqf = jnp.where(sc == thr, 1.0, 0.0)
    need = n_keep - _count(gt)
    ch = 256 if E % 256 == 0 else LANE
    su = jnp.where(lax.broadcasted_iota(jnp.int32, (ch, ch), 0) < lax.broadcasted_iota(jnp.int32, (ch, ch), 1),
                   1.0, 0.0).astype(BF)
    run = jnp.zeros((Qb, 1), F32)
    take = []
    for c0 in range(0, E, ch):
        eqc = eqf[:, c0:c0 + ch]
        prefix = _dot(eqc.astype(BF), su) + run
        take.append(jnp.where(prefix < need, eqc, 0.0))
        run = run + jnp.sum(eqc, -1, keepdims=True)
    sel = causal & (gt | (jnp.concatenate(take, axis=-1) > 0.5))
    o_ref[...] = _masked_attention(q_ref[:, :hq], kv_ref[0:E, :DSA_DH], kv_ref[0:E, DSA_DH:2 * DSA_DH],
                                   sel, DSA_HEADS, DSA_DH)


def _for_causal_extent(S, seg, body):
    bi = pl.program_id(1)
    per = seg // Q_BLOCK
    for e in range(seg, S + 1, seg):
        pl.when((bi >= (e - seg) // Q_BLOCK) & (bi < e // Q_BLOCK))(functools.partial(body, e))
    assert S % seg == 0 and per * Q_BLOCK == seg


def _dsa_kernel(q_ref, aux_ref, kv_ref, ik_ref, o_ref, *, n_keep, seg):
    _for_causal_extent(kv_ref.shape[0], seg,
                       lambda e: _dsa_body(q_ref, aux_ref, kv_ref, ik_ref, o_ref, n_keep=n_keep, E=e))


def _dsa(dsa, dsa_aux, B, S):
    nb = S // Q_BLOCK
    n_keep = min(DSA_TOPK, S // 4)
    return pl.pallas_call(
        functools.partial(_dsa_kernel, n_keep=n_keep, seg=min(CAUSAL_SEG, S)),
        grid=(B, nb),
        in_specs=[
            pl.BlockSpec((Q_BLOCK, 4 * LANE), lambda b, i: (b * nb + i, 0)),
            pl.BlockSpec((Q_BLOCK, LANE), lambda b, i: (b * nb + i, 0)),
            pl.BlockSpec((S, LANE), lambda b, i: (b, 4)),
            pl.BlockSpec((S, LANE), lambda b, i: (b, 5)),
        ],
        out_specs=pl.BlockSpec((Q_BLOCK, BRANCH_W), lambda b, i: (b * nb + i, 0)),
        out_shape=jax.ShapeDtypeStruct((B * S, BRANCH_W), F32),
        compiler_params=_params(("parallel", "arbitrary")),
        name="dsa",
    )(dsa, dsa_aux, dsa, dsa)


def _cmp_kernel(g_ref, w1a_ref, w1b_ref, pos_ref, w1_ref, w2_ref, o_ref):
    g = g_ref[...].astype(BF)
    n = g.shape[0]
    a = _dot(g, w1a_ref[...])
    b = _dot(g, w1b_ref[...])
    posterm = _dot(pos_ref[...].astype(BF), w1_ref[...])[0:1, :]
    h = jax.nn.gelu(a + pltpu.roll(b, n - 1, 0) + posterm)
    o_ref[...] = _dot(h.astype(BF), w2_ref[...]).astype(o_ref.dtype)


def _compress(g2, cmp_w1, cmp_w2, cmp_pos):
    _, B, n, W = g2.shape
    Dh = NSA_DH
    half = W
    w1 = cmp_w1.astype(BF)
    pos8 = jnp.broadcast_to(cmp_pos.reshape(2, 1, CMP_LEN * Dh), (2, 8, CMP_LEN * Dh))
    return pl.pallas_call(
        _cmp_kernel,
        grid=(2, B),
        in_specs=[
            pl.BlockSpec((None, None, n, W), lambda i, b: (i, b, 0, 0)),
            pl.BlockSpec((None, half, Dh), lambda i, b: (i, 0, 0)),
            pl.BlockSpec((None, half, Dh), lambda i, b: (i, 1, 0)),
            pl.BlockSpec((None, 8, CMP_LEN * Dh), lambda i, b: (i, 0, 0)),
            pl.BlockSpec((None, CMP_LEN * Dh, Dh), lambda i, b: (i, 0, 0)),
            pl.BlockSpec((None, Dh, Dh), lambda i, b: (i, 0, 0)),
        ],
        out_specs=pl.BlockSpec((None, None, n, Dh), lambda i, b: (i, b, 0, 0)),
        out_shape=jax.ShapeDtypeStruct((2, B, n, Dh), BF),
        compiler_params=_params(("parallel", "parallel")),
        name="nsa_compress",
    )(g2, w1, w1, pos8, w1, cmp_w2.astype(BF))


def _nsa_kernel(q_ref, gt_ref, kvc_ref, sel_ref, win_ref, exp_ref, o_ref, osel_ref, *, n_top, seg):
    Qb, H, Dh = Q_BLOCK, NSA_HEADS, NSA_DH
    S = sel_ref.shape[0]
    n_cmp = kvc_ref.shape[1]
    n_blk = S // SEL_LEN
    bi = pl.program_id(1)
    qpos = bi * Qb + lax.broadcasted_iota(jnp.int32, (Qb, 1), 0)
    q = q_ref[...]

    kc = kvc_ref[0]
    vc = kvc_ref[1]
    cidx = lax.broadcasted_iota(jnp.int32, (1, n_cmp), 1)
    vis = cidx * CMP_STRIDE + (CMP_LEN - 1) <= qpos
    o_cmp = []
    psum = jnp.zeros((Qb, n_cmp), F32)
    for h in range(H):
        s = jnp.where(vis, _dot_nt(q[:, h * Dh:(h + 1) * Dh], kc), NEG_INF)
        m = jnp.max(s, -1, keepdims=True)
        m = jnp.where(m > NEG_INF, m, 0.0)
        e = jnp.exp(s - m)
        p = e / jnp.maximum(jnp.sum(e, -1, keepdims=True), 1e-30)
        psum = psum + p
        o_cmp.append(_dot(p.astype(BF), vc))

    js = lax.broadcasted_iota(jnp.int32, (n_blk, 1), 0) * SEL_LEN
    cs = lax.broadcasted_iota(jnp.int32, (1, n_cmp), 1) * CMP_STRIDE
    ov = jnp.maximum(jnp.minimum(cs + CMP_LEN, js + SEL_LEN) - jnp.maximum(cs, js), 0).astype(F32) / CMP_LEN
    ov = ov.astype(BF)
    imp = sum(_dot_nt(ov, t) for t in _split3(psum))
    blk = lax.broadcasted_iota(jnp.int32, (n_blk, 1), 0)
    sel_shift = SEL_LEN.bit_length() - 1
    cur = jnp.right_shift(bi * Qb + lax.broadcasted_iota(jnp.int32, (1, Qb), 1), sel_shift)
    forced = (blk == 0) | (blk == cur) | (blk == cur - 1)
    imp = jnp.where(blk <= cur, jnp.where(forced, jnp.inf, imp), NEG_INF)
    rank = jnp.zeros((n_blk, Qb), F32)
    for j in range(n_blk):
        row = imp[j:j + 1, :]
        rank = rank + jnp.where((row > imp) | ((row == imp) & (blk > j)), 1.0, 0.0)
    chosen = jnp.where(rank < n_top, 1.0, 0.0).T.astype(BF)

    def selected(e):
        kpos = lax.broadcasted_iota(jnp.int32, (1, e), 1)
        mask = (_dot(chosen, exp_ref[:, 0:e]) > 0.5) & (kpos <= qpos)
        osel_ref[...] = _masked_attention(q, sel_ref[0:e, :Dh], sel_ref[0:e, Dh:2 * Dh], mask, H, Dh)

    _for_causal_extent(S, seg, selected)
    o_sel = osel_ref[...]

    wlen = WINDOW + Qb
    start = pl.multiple_of(jnp.maximum(bi * Qb - WINDOW, 0), Qb)
    kwin = win_ref[pl.ds(start, wlen), :]
    dlt = qpos - (start + lax.broadcasted_iota(jnp.int32, (1, wlen), 1))
    o_win = _masked_attention(q, kwin[:, :Dh], kwin[:, Dh:2 * Dh], (dlt >= 0) & (dlt < WINDOW), H, Dh)

    g = jax.nn.sigmoid(gt_ref[:, :3 * H])
    outs = []
    for h in range(H):
        outs.append(g[:, 3 * h:3 * h + 1] * o_cmp[h]
                    + g[:, 3 * h + 1:3 * h + 2] * o_sel[:, h * Dh:(h + 1) * Dh]
                    + g[:, 3 * h + 2:3 * h + 3] * o_win[:, h * Dh:(h + 1) * Dh])
    o_ref[...] = jnp.concatenate(outs, axis=-1)


def _nsa(nsa, nsa_aux, kvc, B, S):
    nb = S // Q_BLOCK
    n_cmp = kvc.shape[2]
    n_blk = S // SEL_LEN
    n_top = min(SEL_TOPN, n_blk)
    assert S >= WINDOW + Q_BLOCK
    expand = jnp.asarray(np.arange(S)[None, :] // SEL_LEN == np.arange(n_blk)[:, None], BF)
    return pl.pallas_call(
        functools.partial(_nsa_kernel, n_top=n_top, seg=min(CAUSAL_SEG, S)),
        grid=(B, nb),
        in_specs=[
            pl.BlockSpec((Q_BLOCK, 2 * LANE), lambda b, i: (b * nb + i, 0)),
            pl.BlockSpec((Q_BLOCK, LANE), lambda b, i: (b * nb + i, 0)),
            pl.BlockSpec((2, None, n_cmp, NSA_DH), lambda b, i: (0, b, 0, 0)),
            pl.BlockSpec((S, LANE), lambda b, i: (b, 3)),
            pl.BlockSpec((S, LANE), lambda b, i: (b, 4)),
            pl.BlockSpec((n_blk, S), lambda b, i: (0, 0)),
        ],
        out_specs=pl.BlockSpec((Q_BLOCK, BRANCH_W), lambda b, i: (b * nb + i, 0)),
        out_shape=jax.ShapeDtypeStruct((B * S, BRANCH_W), F32),
        scratch_shapes=[pltpu.VMEM((Q_BLOCK, BRANCH_W), F32)],
        compiler_params=_params(("parallel", "arbitrary")),
        name="nsa",
    )(nsa, nsa_aux, kvc, nsa, nsa, expand)


def _merge_kernel(x_ref, y0, y1, y2, y3, wg_ref, wb_ref, wo_ref, g_ref, b_ref, o_ref, *, alpha):
    x = x_ref[...]
    xb = x.astype(BF)
    D = x.shape[1]
    merged = jnp.zeros_like(x)
    for n, y_ref in enumerate((y0, y1, y2, y3)):
        gate = jax.nn.sigmoid(_dot(xb, wg_ref[:, n * D:(n + 1) * D]))
        merged = merged + gate * _dot(y_ref[...].astype(BF), wb_ref[n])
    o_ref[...] = _layer_norm(alpha * x + _dot(merged.astype(BF), wo_ref[...]), g_ref[...], b_ref[...])


def _merge(x, ys, w_gate, w_branch, w_out, g, b, alpha, tm=256):
    T, D = x.shape
    full = lambda shape: pl.BlockSpec(shape, lambda i: (0,) * len(shape))
    return pl.pallas_call(
        functools.partial(_merge_kernel, alpha=alpha),
        grid=(T // tm,),
        in_specs=[pl.BlockSpec((tm, D), lambda i: (i, 0))]
        + [pl.BlockSpec((tm, BRANCH_W), lambda i: (i, 0))] * N_BRANCH
        + [full(w_gate.shape), full(w_branch.shape), full(w_out.shape), full((1, D)), full((1, D))],
        out_specs=pl.BlockSpec((tm, D), lambda i: (i, 0)),
        out_shape=jax.ShapeDtypeStruct((T, D), F32),
        compiler_params=_params(("parallel",)),
        name="merge",
    )(x, *ys, w_gate, w_branch, w_out, g, b)


def _matmul_kernel(a_ref, w_ref, o_ref):
    o_ref[...] = _dot(a_ref[...].astype(BF), w_ref[...])


def _matmul(a, w, tm):
    M, K = a.shape
    N = w.shape[1]
    return pl.pallas_call(
        _matmul_kernel,
        grid=(M // tm,),
        in_specs=[pl.BlockSpec((tm, K), lambda i: (i, 0)), pl.BlockSpec((K, N), lambda i: (0, 0))],
        out_specs=pl.BlockSpec((tm, N), lambda i: (i, 0)),
        out_shape=jax.ShapeDtypeStruct((M, N), F32),
        compiler_params=_params(("parallel",)),
        name="kv_proj",
    )(a, w)


def _xattn_kernel(x_ref, kv_ref, wq_ref, wo_ref, g_ref, b_ref, o_ref, *, alpha):
    x = x_ref[...]
    D = x.shape[1]
    dh = D // X_HEADS
    q = _dot(x.astype(BF), wq_ref[...])
    outs = []
    for h in range(X_HEADS):
        k = kv_ref[:, h * dh:(h + 1) * dh].astype(BF)
        v = kv_ref[:, D + h * dh:D + (h + 1) * dh].astype(BF)
        s = _dot_nt(q[:, h * dh:(h + 1) * dh].astype(BF), k) * (dh ** -0.5)
        e = jnp.exp(s - jnp.max(s, -1, keepdims=True))
        outs.append(_dot(e.astype(BF), v) / jnp.sum(e, -1, keepdims=True))
    att = jnp.concatenate(outs, axis=-1).astype(BF)
    o_ref[...] = _layer_norm(alpha * x + _dot(att, wo_ref[...]), g_ref[...], b_ref[...])


def _xattn(x, kv, wq, wo, g, b, alpha, S, M, tm=512):
    T, D = x.shape
    per = S // tm
    full = lambda shape: pl.BlockSpec(shape, lambda i: (0,) * len(shape))
    return pl.pallas_call(
        functools.partial(_xattn_kernel, alpha=alpha),
        grid=(T // tm,),
        in_specs=[pl.BlockSpec((tm, D), lambda i: (i, 0)),
                  pl.BlockSpec((M, 2 * D), lambda i: (i // per, 0)),
                  full(wq.shape), full(wo.shape), full((1, D)), full((1, D))],
        out_specs=pl.BlockSpec((tm, D), lambda i: (i, 0)),
        out_shape=jax.ShapeDtypeStruct((T, D), F32),
        compiler_params=_params(("parallel",)),
        name="xattn",
    )(x, kv, wq, wo, g, b)


def _rope_tables(S):
    pos = jnp.arange(S).astype(F32)

    def base(rot, theta):
        half = rot // 2
        inv = jnp.power(jnp.float32(theta), -2.0 * jnp.arange(half, dtype=F32) / rot)
        ang = pos[:, None] * inv[None, :]
        return jnp.cos(ang), jnp.sin(ang)

    def tile(hd, rot, theta, width, scale=1.0):
        c, s = base(rot, theta)
        lane = np.arange(LANE)
        jj = lane % hd
        roped = (jj < rot) & (lane < width)
        idx = jj % (rot // 2)
        ct = jnp.where(roped[None, :], c[:, idx], 1.0) * scale
        st = jnp.where(roped[None, :], s[:, idx], 0.0) * scale
        return ct, st

    tabs = [
        tile(RET_DK, RET_DK, RET_THETA, LANE),
        tile(RET_DK, RET_DK, RET_THETA, LANE, RET_DK ** -0.5),
        tile(DSA_DH, DSA_DH // ROPE_FRAC, ROPE_THETA, LANE, DSA_DH ** -0.5),
        tile(DSA_DH, DSA_DH // ROPE_FRAC, ROPE_THETA, DSA_DH),
        tile(IDX_DH, IDX_DH // ROPE_FRAC, ROPE_THETA, LANE),
        tile(IDX_DH, IDX_DH // ROPE_FRAC, ROPE_THETA, IDX_DH),
    ]
    return jnp.concatenate([t[0] for t in tabs], 1), jnp.concatenate([t[1] for t in tabs], 1)


def _pack_w_in(w):
    D = w.shape[0]
    main, rot = [], []
    for _, segs, width in _GROUPS:
        used = 0
        for s in segs:
            start, n = _SEG[s]
            piece = w[:, start:start + n]
            main.append(piece)
            if s in _ROPE:
                hd, r, _ = _ROPE[s]
                p3 = piece.reshape(D, n // hd, hd)
                rot.append(jnp.concatenate([-p3[..., r // 2:r], p3[..., :r // 2],
                                            jnp.zeros((D, n // hd, hd - r), w.dtype)], -1).reshape(D, n))
            else:
                rot.append(jnp.zeros((D, n), w.dtype))
            used += n
        pad = jnp.zeros((D, width - used), w.dtype)
        main.append(pad)
        rot.append(pad)
    wm = jnp.concatenate(main, 1)
    wrf = jnp.concatenate(rot, 1)
    wr = jnp.concatenate([wrf[:, t * LANE:(t + 1) * LANE] for t in _ROPED_TILES], 1)
    return wm.astype(BF), wr.astype(BF), w[:, GATE_START:].astype(BF)


def kernel(x, mem, ln_g, ln_b, ffn1_w_gu, ffn1_w_down, w_in, cmp_w1, cmp_w2, cmp_pos, conv_w, conv_b,
           dt_bias, a_log, d_skip, ssm_norm_g, w_branch, w_out, xattn_wq, xattn_wkv, xattn_wo,
           ffn2_w_gu, ffn2_w_down):
    B, S, D = x.shape
    M = mem.shape[1]
    depth = ln_g.shape[0]
    alpha = (2 * depth) ** 0.25
    T = B * S
    cos_t, sin_t = _rope_tables(S)
    h = x.reshape(T, D)
    mem2 = mem.reshape(B * M, D)
    for l in range(depth):
        lg = lambda i: ln_g[l, i][None, :]
        lb = lambda i: ln_b[l, i][None, :]
        h = _ffn(h, ffn1_w_gu[l].astype(BF), ffn1_w_down[l].astype(BF), lg(0), lb(0), alpha)

        wm, wr, w_gate = _pack_w_in(w_in[l])
        ret, dsa, nsa, ssd, dsa_aux, nsa_aux = _inproj(h, wm, wr, cos_t, sin_t, S)
        y_ret = _retention(ret, B, S)
        y_dsa = _dsa(dsa, dsa_aux, B, S)
        g2 = jnp.stack([nsa[:, 2 * LANE:2 * LANE + NSA_DH], nsa[:, 2 * LANE + NSA_DH:3 * LANE]])
        kvc = _compress(g2.reshape(2, B, S // CMP_STRIDE, CMP_STRIDE * NSA_DH), cmp_w1[l], cmp_w2[l], cmp_pos[l])
        y_nsa = _nsa(nsa, nsa_aux, kvc, B, S)
        dt0 = SSM_INNER + SSM_CONV_DIM
        dt_rows = jnp.pad(ssd[:, dt0:dt0 + SSM_HEADS].reshape(B, S // SSM_CHUNK, SSM_CHUNK, SSM_HEADS)
                          .transpose(0, 1, 3, 2), ((0, 0), (0, 0), (0, CONV_PAD - SSM_HEADS), (0, 0)))
        y_ssd = _ssd(ssd, dt_rows, conv_w[l], conv_b[l], dt_bias[l], a_log[l], d_skip[l], ssm_norm_g[l], B, S)
        h = _merge(h, (y_ret, y_dsa, y_nsa, y_ssd), w_gate, w_branch[l].astype(BF), w_out[l].astype(BF),
                   lg(1), lb(1), alpha)

        kv = _matmul(mem2, xattn_wkv[l].astype(BF), tm=min(256, B * M))
        h = _xattn(h, kv, xattn_wq[l].astype(BF), xattn_wo[l].astype(BF), lg(2), lb(2), alpha, S, M)
        h = _ffn(h, ffn2_w_gu[l].astype(BF), ffn2_w_down[l].astype(BF), lg(3), lb(3), alpha)
    return h.reshape(B, S, D)
```

```python
import functools
import math

import numpy as np
import jax
import jax.numpy as jnp
from jax import lax
from jax.experimental import pallas as pl
from jax.experimental.pallas import tpu as pltpu

F32 = jnp.float32
BF = jnp.bfloat16
NEG_INF = float("-inf")

Q_BLOCK = 128
ROPE_THETA = 500000.0
ROPE_FRAC = 4
LN_EPS = 1e-5
RET_HEADS, RET_DK, RET_DV, RET_CHUNK, RET_THETA = 4, 32, 64, 128, 10000.0
DSA_HEADS, DSA_DH, IDX_HEADS, IDX_DH, DSA_TOPK = 4, 64, 8, 32, 256
NSA_HEADS, NSA_DH, CMP_LEN, CMP_STRIDE, SEL_LEN, SEL_TOPN, WINDOW = 4, 64, 32, 16, 64, 16, 512
SSM_HEADS, SSM_HEADDIM, SSM_GROUPS, SSM_STATE, SSM_CONV, SSM_CHUNK = 4, 64, 2, 128, 4, 128
SSM_INNER = SSM_HEADS * SSM_HEADDIM
SSM_CONV_DIM = SSM_INNER + 2 * SSM_GROUPS * SSM_STATE
N_BRANCH = 4
BRANCH_W = 256
X_HEADS = 4

LANE = 128
CONV_PAD = 8
VMEM_LIMIT = 56 * 1024 * 1024

_SEG_NAMES = ("r_q", "r_k", "r_v", "r_g", "d_q", "d_k", "d_v", "i_q", "i_k", "i_w",
              "n_q", "n_kc", "n_vc", "n_ks", "n_vs", "n_kw", "n_vw", "n_g", "s_z", "s_xbc", "s_dt")
_SEG_WIDTHS = (RET_HEADS * RET_DK, RET_HEADS * RET_DK, RET_HEADS * RET_DV, RET_HEADS * RET_DV,
               DSA_HEADS * DSA_DH, DSA_DH, DSA_DH, IDX_HEADS * IDX_DH, IDX_DH, IDX_HEADS,
               NSA_HEADS * NSA_DH, NSA_DH, NSA_DH, NSA_DH, NSA_DH, NSA_DH, NSA_DH, NSA_HEADS * 3,
               SSM_INNER, SSM_CONV_DIM, SSM_HEADS)
_SEG = {}
_o = 0
for _n, _w in zip(_SEG_NAMES, _SEG_WIDTHS):
    _SEG[_n] = (_o, _w)
    _o += _w
GATE_START = _o

_GROUPS = (
    ("ret", ("r_q", "r_k", "r_v", "r_g"), 768),
    ("dsa", ("d_q", "i_q", "d_k", "d_v", "i_k", "i_w"), 768),
    ("nsa", ("n_q", "n_kc", "n_vc", "n_ks", "n_vs", "n_kw", "n_vw", "n_g"), 768),
    ("ssd", ("s_z", "s_xbc", "s_dt"), 1152),
)
_ROPE = {
    "r_q": (RET_DK, RET_DK, 0), "r_k": (RET_DK, RET_DK, 1),
    "d_q": (DSA_DH, DSA_DH // ROPE_FRAC, 2), "n_q": (NSA_DH, NSA_DH // ROPE_FRAC, 2),
    "d_k": (DSA_DH, DSA_DH // ROPE_FRAC, 3), "n_kc": (NSA_DH, NSA_DH // ROPE_FRAC, 3),
    "n_ks": (NSA_DH, NSA_DH // ROPE_FRAC, 3), "n_kw": (NSA_DH, NSA_DH // ROPE_FRAC, 3),
    "i_q": (IDX_DH, IDX_DH // ROPE_FRAC, 4), "i_k": (IDX_DH, IDX_DH // ROPE_FRAC, 5),
}
N_TABLES = 6
_TABLE_HALF = (RET_DK // 2, RET_DK // 2, DSA_DH // ROPE_FRAC // 2, DSA_DH // ROPE_FRAC // 2,
               IDX_DH // ROPE_FRAC // 2, IDX_DH // ROPE_FRAC // 2)
LOG2E = math.log2(math.e)


def _build_layout():
    tile_seg = []
    group_tiles = []
    for _, segs, width in _GROUPS:
        cols = []
        for s in segs:
            cols += [s] * _SEG[s][1]
        cols += [None] * (width - len(cols))
        tile_seg += [cols[t * LANE:(t + 1) * LANE] for t in range(width // LANE)]
        group_tiles.append(width // LANE)
    roped_tiles = [t for t, segs in enumerate(tile_seg) if any(s in _ROPE for s in segs)]
    tile_table = [_ROPE[tile_seg[t][0]][2] for t in roped_tiles]
    return roped_tiles, tile_table, group_tiles, len(tile_seg) * LANE


_ROPED_TILES, _TILE_TABLE, _GROUP_TILES, N_MAIN = _build_layout()


def _dot(a, b):
    return jnp.dot(a, b, preferred_element_type=F32)


def _dot_nt(a, b):
    return lax.dot_general(a, b, (((1,), (1,)), ((), ())), preferred_element_type=F32)


def _dot_tn(a, b):
    return lax.dot_general(a, b, (((0,), (0,)), ((), ())), preferred_element_type=F32)


def _split3(a):
    hi = a.astype(BF)
    r1 = a - hi.astype(F32)
    mid = r1.astype(BF)
    lo = (r1 - mid.astype(F32)).astype(BF)
    return hi, mid, lo


def _layer_norm(v, g, b):
    mu = jnp.mean(v, -1, keepdims=True)
    d = v - mu
    var = jnp.mean(d * d, -1, keepdims=True)
    return d * lax.rsqrt(var + LN_EPS) * g + b


def _silu(v):
    return v * jax.nn.sigmoid(v)


def _params(sem):
    return pltpu.CompilerParams(dimension_semantics=sem, vmem_limit_bytes=VMEM_LIMIT)


def _ffn_kernel(x_ref, wg_ref, wu_ref, wd_ref, g_ref, b_ref, o_ref, acc_ref, *, alpha):
    j = pl.program_id(1)

    @pl.when(j == 0)
    def _():
        acc_ref[...] = jnp.zeros_like(acc_ref)

    xb = x_ref[...].astype(BF)
    gate = _dot(xb, wg_ref[...])
    up = _dot(xb, wu_ref[...])
    h = (_silu(gate) * up).astype(BF)
    acc_ref[...] += _dot(h, wd_ref[...])

    @pl.when(j == pl.num_programs(1) - 1)
    def _():
        o_ref[...] = _layer_norm(alpha * x_ref[...] + 0.5 * acc_ref[...], g_ref[...], b_ref[...])


def _ffn(x, w_gu, w_down, g, b, alpha, tm=512):
    T, D = x.shape
    F = w_down.shape[0]
    fc = F // 2 if (F // 2) % LANE == 0 else F
    nf = F // fc
    return pl.pallas_call(
        functools.partial(_ffn_kernel, alpha=alpha),
        grid=(T // tm, nf),
        in_specs=[
            pl.BlockSpec((tm, D), lambda i, j: (i, 0)),
            pl.BlockSpec((D, fc), lambda i, j: (0, j)),
            pl.BlockSpec((D, fc), lambda i, j: (0, j + nf)),
            pl.BlockSpec((fc, D), lambda i, j: (j, 0)),
            pl.BlockSpec((1, D), lambda i, j: (0, 0)),
            pl.BlockSpec((1, D), lambda i, j: (0, 0)),
        ],
        out_specs=pl.BlockSpec((tm, D), lambda i, j: (i, 0)),
        out_shape=jax.ShapeDtypeStruct((T, D), F32),
        scratch_shapes=[pltpu.VMEM((tm, D), F32)],
        compiler_params=_params(("parallel", "arbitrary")),
        name="ffn",
    )(x, w_gu, w_gu, w_down, g, b)


AUX_TILE = 5


def _inproj_kernel(x_ref, wm_ref, cos_ref, shi_ref, slo_ref, o_ret, o_dsa, o_nsa, o_ssd, o_dsa_aux, o_nsa_aux):
    xb = x_ref[...].astype(BF)
    outs = (o_ret, o_dsa, o_nsa, o_ssd)
    aux = (None, o_dsa_aux, o_nsa_aux, None)
    t0 = 0
    for o_ref, aux_ref, nt in zip(outs, aux, _GROUP_TILES):
        y = _dot(xb, wm_ref[:, t0 * LANE:(t0 + nt) * LANE])
        for t in range(nt):
            yt = y[:, t * LANE:(t + 1) * LANE]
            gt = t0 + t
            if gt in _ROPED_TILES:
                tab = _TILE_TABLE[_ROPED_TILES.index(gt)]
                half = _TABLE_HALF[tab]
                lanes = slice(tab * LANE, (tab + 1) * LANE)
                yt = (yt * cos_ref[:, lanes] + pltpu.roll(yt, LANE - half, 1) * shi_ref[:, lanes]
                      + pltpu.roll(yt, half, 1) * slo_ref[:, lanes])
            o_ref[:, t * LANE:(t + 1) * LANE] = yt.astype(o_ref.dtype)
            if aux_ref is not None and t == AUX_TILE:
                aux_ref[...] = yt
        t0 += nt


def _inproj(x, wm, tables, S, tm=256):
    T, D = x.shape
    ns = S // tm
    widths = [nt * LANE for nt in _GROUP_TILES] + [LANE, LANE]
    dtypes = [F32, BF, BF, F32, F32, F32]
    table_spec = pl.BlockSpec((tm, N_TABLES * LANE), lambda i: (i % ns, 0))
    return pl.pallas_call(
        _inproj_kernel,
        grid=(T // tm,),
        in_specs=[
            pl.BlockSpec((tm, D), lambda i: (i, 0)),
            pl.BlockSpec((D, N_MAIN), lambda i: (0, 0)),
            table_spec, table_spec, table_spec,
        ],
        out_specs=[pl.BlockSpec((tm, w), lambda i: (i, 0)) for w in widths],
        out_shape=[jax.ShapeDtypeStruct((T, w), dt) for w, dt in zip(widths, dtypes)],
        compiler_params=_params(("parallel",)),
        name="in_proj",
    )(x, wm, *tables)


def _ret_kernel(r_ref, o_ref, st_ref):
    C = RET_CHUNK

    @pl.when(pl.program_id(0) == 0)
    def _():
        st_ref[...] = jnp.zeros_like(st_ref)

    hq = RET_HEADS * RET_DK
    rel = (lax.broadcasted_iota(jnp.int32, (C, C), 0) - lax.broadcasted_iota(jnp.int32, (C, C), 1)).astype(F32)
    row = lax.broadcasted_iota(jnp.int32, (C, 1), 0).astype(F32)
    head_consts = []
    for h in range(RET_HEADS):
        lg = math.log1p(-(2.0 ** (-5.0 - h)))
        head_consts.append((jnp.where(rel >= 0, jnp.exp(jnp.maximum(rel, 0.0) * lg), 0.0),
                            jnp.exp((C - 1 - row) * lg), jnp.exp((row + 1.0) * lg), math.exp(C * lg)))
    for b in range(r_ref.shape[0]):
        r = r_ref[b]
        q, k = r[:, :hq], r[:, hq:2 * hq]
        v = r[:, 2 * hq:2 * hq + RET_HEADS * RET_DV]
        g = r[:, 2 * hq + RET_HEADS * RET_DV:]
        outs = []
        for h, (decay, zeta, xi, chunk_decay) in enumerate(head_consts):
            qh = q[:, h * RET_DK:(h + 1) * RET_DK].astype(BF)
            kh = k[:, h * RET_DK:(h + 1) * RET_DK]
            vh = v[:, h * RET_DV:(h + 1) * RET_DV].astype(BF)
            sc = _dot_nt(qh, kh.astype(BF)) * decay
            intra = _dot(sc.astype(BF), vh)
            prev = st_ref[b * RET_HEADS + h]
            cross = _dot(qh, prev.astype(BF)) * xi
            st_ref[b * RET_HEADS + h] = prev * chunk_decay + _dot_tn((kh * zeta).astype(BF), vh)
            o = intra + cross
            mu = jnp.mean(o, -1, keepdims=True)
            d = o - mu
            var = jnp.mean(d * d, -1, keepdims=True)
            outs.append(d * lax.rsqrt(var + LN_EPS))
        o_ref[b] = _silu(g) * jnp.concatenate(outs, axis=-1)


def _retention(ret, B, S):
    n = S // RET_CHUNK
    W = ret.shape[1]
    return pl.pallas_call(
        _ret_kernel,
        grid=(n,),
        in_specs=[pl.BlockSpec((B, RET_CHUNK, W), lambda c: (0, c, 0))],
        out_specs=pl.BlockSpec((B, RET_CHUNK, BRANCH_W), lambda c: (0, c, 0)),
        out_shape=jax.ShapeDtypeStruct((B, S, BRANCH_W), F32),
        scratch_shapes=[pltpu.VMEM((B * RET_HEADS, RET_DK, RET_DV), F32)],
        compiler_params=_params(("arbitrary",)),
        name="retention",
    )(ret.reshape(B, S, W)).reshape(B * S, BRANCH_W)


def _softplus(v):
    return jnp.maximum(v, 0.0) + jnp.log1p(jnp.exp(-jnp.abs(v)))


def _ssd_kernel(s_ref, dtr_ref, cw_ref, cb_ref, dtb_c_ref, a_c_ref, dtb_r_ref, a_r_ref, dsk_ref, ng_ref,
                o_ref, st_ref, xpad_ref):
    Q, H, P, N = SSM_CHUNK, SSM_HEADS, SSM_HEADDIM, SSM_STATE

    @pl.when(pl.program_id(0) == 0)
    def _():
        st_ref[...] = jnp.zeros_like(st_ref)
        xpad_ref[:, 0:CONV_PAD, :] = jnp.zeros((xpad_ref.shape[0], CONV_PAD, SSM_CONV_DIM), F32)

    ri = lax.broadcasted_iota(jnp.int32, (Q, Q), 0)
    ci = lax.broadcasted_iota(jnp.int32, (Q, Q), 1)
    tril = ri >= ci
    lo_tri = jnp.where(tril, 1.0, 0.0).astype(BF)
    up_tri = jnp.where(ci >= ri, 1.0, 0.0).astype(BF)
    gn = SSM_GROUPS * N
    gw = SSM_INNER // SSM_GROUPS
    for b in range(s_ref.shape[0]):
        z = s_ref[b, :, :SSM_INNER]
        xpad_ref[b, CONV_PAD:, :] = s_ref[b, :, SSM_INNER:SSM_INNER + SSM_CONV_DIM]
        dt_raw = s_ref[b, :, SSM_INNER + SSM_CONV_DIM:SSM_INNER + SSM_CONV_DIM + H]
        conv = cb_ref[...]
        for kk in range(SSM_CONV):
            off = CONV_PAD - (SSM_CONV - 1) + kk
            conv = conv + cw_ref[kk:kk + 1, :] * xpad_ref[b, off:off + Q, :]
        xpad_ref[b, 0:CONV_PAD, :] = xpad_ref[b, Q:Q + CONV_PAD, :]
        xc = _silu(conv)
        xs = xc[:, :SSM_INNER]
        bm = xc[:, SSM_INNER:SSM_INNER + gn]
        cm = xc[:, SSM_INNER + gn:]

        dt_c = _softplus(dt_raw + dtb_c_ref[...])
        adt_c = dt_c * a_c_ref[...]
        dt_r = _softplus(dtr_ref[b] + dtb_r_ref[...])
        adt_r = dt_r * a_r_ref[...]
        acs_c = sum(_dot(lo_tri, t) for t in _split3(adt_c))
        acs_r = sum(_dot(t, up_tri) for t in _split3(adt_r))

        outs = []
        for grp in range(SSM_GROUPS):
            bg = bm[:, grp * N:(grp + 1) * N]
            cg = cm[:, grp * N:(grp + 1) * N].astype(BF)
            cb = _dot_nt(cg, bg.astype(BF))
            for h in range(grp * (H // SSM_GROUPS), (grp + 1) * (H // SSM_GROUPS)):
                a_col = acs_c[:, h:h + 1]
                a_last = acs_c[Q - 1:Q, h:h + 1]
                lm = jnp.exp(jnp.where(tril, a_col - acs_r[h:h + 1, :], NEG_INF))
                xh = xs[:, h * P:(h + 1) * P]
                xdt = (xh * dt_c[:, h:h + 1]).astype(BF)
                y = _dot((cb * lm).astype(BF), xdt)
                prev = st_ref[b * H + h]
                y = y + _dot(cg, prev.astype(BF)) * jnp.exp(a_col)
                st_ref[b * H + h] = (prev * jnp.exp(a_last)
                                     + _dot_tn((bg * jnp.exp(a_last - a_col)).astype(BF), xdt))
                outs.append(y)
        y = jnp.concatenate(outs, axis=-1) + xs * dsk_ref[...]
        y = y * _silu(z)
        normed = []
        for grp in range(SSM_GROUPS):
            yg = y[:, grp * gw:(grp + 1) * gw]
            normed.append(yg * lax.rsqrt(jnp.mean(yg * yg, -1, keepdims=True) + LN_EPS))
        o_ref[b] = jnp.concatenate(normed, axis=-1) * ng_ref[...]


def _ssd(ssd, dt_rows, conv_w, conv_b, dt_bias, a_log, d_skip, norm_g, B, S):
    Q, H = SSM_CHUNK, SSM_HEADS
    n = S // Q
    W = ssd.shape[1]
    a = -jnp.exp(a_log.astype(F32))
    pad_r = lambda v: jnp.broadcast_to(jnp.pad(v, (0, CONV_PAD - H))[:, None], (CONV_PAD, Q)).astype(F32)
    full = lambda shape: pl.BlockSpec(shape, lambda c: (0,) * len(shape))
    return pl.pallas_call(
        _ssd_kernel,
        grid=(n,),
        in_specs=[
            pl.BlockSpec((B, Q, W), lambda c: (0, c, 0)),
            pl.BlockSpec((B, None, CONV_PAD, Q), lambda c: (0, c, 0, 0)),
            full((SSM_CONV, SSM_CONV_DIM)), full((1, SSM_CONV_DIM)),
            full((1, H)), full((1, H)), full((CONV_PAD, Q)), full((CONV_PAD, Q)),
            full((1, SSM_INNER)), full((1, SSM_INNER)),
        ],
        out_specs=pl.BlockSpec((B, Q, BRANCH_W), lambda c: (0, c, 0)),
        out_shape=jax.ShapeDtypeStruct((B, S, BRANCH_W), F32),
        scratch_shapes=[pltpu.VMEM((B * H, SSM_STATE, SSM_HEADDIM), F32),
                        pltpu.VMEM((B, Q + CONV_PAD, SSM_CONV_DIM), F32)],
        compiler_params=_params(("arbitrary",)),
        name="ssd",
    )(ssd.reshape(B, S, W), dt_rows, conv_w, conv_b[None, :], dt_bias[None, :], a[None, :], pad_r(dt_bias),
      pad_r(a), jnp.repeat(d_skip, SSM_HEADDIM)[None, :], norm_g[None, :]).reshape(B * S, BRANCH_W)


def _masked_attention(q_bf, k_bf, v_bf, mask, heads, dh):
    outs = []
    for h in range(heads):
        s = jnp.where(mask, _dot_nt(q_bf[:, h * dh:(h + 1) * dh], k_bf), NEG_INF)
        m = jnp.max(s, -1, keepdims=True)
        m = jnp.where(m > NEG_INF, m, 0.0)
        e = jnp.exp2(s - m)
        l = jnp.maximum(jnp.sum(e, -1, keepdims=True), 1e-30)
        outs.append(_dot(e.astype(BF), v_bf) / l)
    return jnp.concatenate(outs, axis=-1)


def _count(m):
    return jnp.sum(jnp.where(m, 1.0, 0.0), -1, keepdims=True)


SEL_BIG = float(2 ** 20)
CAUSAL_SEG = 1024
BISECT_PLAIN_STEPS = 26
BISECT_MAX_STEPS = 400


def _kth_threshold(sc, nvalid, vmin, vmax, k):
    kf = float(k)
    short = nvalid <= kf
    c_ge0 = _count(sc >= 0.0)
    c_gt0 = _count(sc > 0.0)
    up0 = c_ge0 >= kf
    tie0 = up0 & (c_gt0 < kf) & (c_ge0 > kf) & (nvalid > kf)
    lo0 = jnp.where(short, jnp.finfo(F32).min, jnp.where(up0, 0.0, vmin))
    hi0 = jnp.where(up0, vmax, 0.0)
    done0 = jnp.where(short | tie0 | (c_ge0 == kf), 1.0, 0.0)

    def split(lo, hi, done):
        mid = jnp.where(done > 0.5, lo, lo + (hi - lo) * 0.5)
        c = _count(sc >= mid)
        up = (c >= kf) & (done < 0.5)
        return jnp.where(up, mid, lo), jnp.where(up | (done > 0.5), hi, mid), jnp.where(up & (c == kf), 1.0, done)

    def plain_cond(st):
        return (st[0] < BISECT_PLAIN_STEPS) & (jnp.min(st[3]) < 0.5)

    def plain_step(st):
        it, lo, hi, done = st
        lo, hi, done = split(lo, hi, done)
        lo, hi, done = split(lo, hi, done)
        return it + 2, lo, hi, done

    _, lo, hi, done = lax.while_loop(plain_cond, plain_step, (jnp.int32(0), lo0, hi0, done0))

    def exact_cond(st):
        return (st[0] < BISECT_MAX_STEPS) & (jnp.min(st[4]) < 0.5)

    def exact_step(st):
        it, lo, hi, tie, done = st
        cand = jnp.min(jnp.where(sc >= lo, sc, jnp.inf), -1, keepdims=True)
        fin = (_count(sc > cand) < kf) & (done < 0.5)
        lo = jnp.where(fin, cand, lo)
        tie = jnp.where(fin, 1.0, tie)
        done = jnp.where(fin, 1.0, done)
        lo, hi, done = split(lo, hi, done)
        return it + 1, lo, hi, tie, done

    _, thr, _, tie, _ = lax.while_loop(
        exact_cond, exact_step, (jnp.int32(0), lo, hi, jnp.where(tie0, 1.0, 0.0), done))
    return thr, tie


def _dsa_body(q_ref, aux_ref, kv_ref, ik_ref, o_ref, *, n_keep, E):
    Qb = Q_BLOCK
    bi = pl.program_id(1)
    qpos = bi * Qb + lax.broadcasted_iota(jnp.int32, (Qb, 1), 0)
    kpos = lax.broadcasted_iota(jnp.int32, (1, E), 1)
    causal = kpos <= qpos
    hq = DSA_HEADS * DSA_DH
    iw = aux_ref[:, IDX_DH:IDX_DH + IDX_HEADS] * (IDX_HEADS ** -0.5) * (IDX_DH ** -0.5)
    ik = ik_ref[0:E, :IDX_DH]
    score = jnp.zeros((Qb, E), F32)
    for h in range(IDX_HEADS):
        rel = jnp.maximum(_dot_nt(q_ref[:, hq + h * IDX_DH:hq + (h + 1) * IDX_DH], ik), 0.0)
        score = score + rel * iw[:, h:h + 1]
    sc = jnp.where(causal, score, NEG_INF)
    vmax = jnp.max(sc, -1, keepdims=True)
    vmin = jnp.min(jnp.where(causal, score, jnp.inf), -1, keepdims=True)
    thr, tie = _kth_threshold(sc, (qpos + 1).astype(F32), vmin, vmax, n_keep)

    def resolve_ties():
        gt = sc > thr
        eqf = jnp.where(sc == thr, 1.0, 0.0)
        need = n_keep - _count(gt)
        ch = 256 if E % 256 == 0 else LANE
        before = jnp.where(lax.broadcasted_iota(jnp.int32, (ch, ch), 0)
                           < lax.broadcasted_iota(jnp.int32, (ch, ch), 1), 1.0, 0.0).astype(BF)
        run = jnp.zeros((Qb, 1), F32)
        take = []
        for c0 in range(0, E, ch):
            eqc = eqf[:, c0:c0 + ch]
            prefix = _dot(eqc.astype(BF), before) + run
            take.append(jnp.where(prefix < need, eqc, 0.0))
            run = run + jnp.sum(eqc, -1, keepdims=True)
        return jnp.where(gt | (jnp.concatenate(take, axis=-1) > 0.5), jnp.inf, NEG_INF)

    picked = lax.cond(jnp.max(tie) > 0.5, resolve_ties, lambda: sc)
    o_ref[...] = _masked_attention(q_ref[:, :hq], kv_ref[0:E, :DSA_DH], kv_ref[0:E, DSA_DH:2 * DSA_DH],
                                   picked >= thr, DSA_HEADS, DSA_DH)


def _for_causal_extent(S, seg, body):
    bi = pl.program_id(1)
    per = seg // Q_BLOCK
    for e in range(seg, S + 1, seg):
        pl.when((bi >= (e - seg) // Q_BLOCK) & (bi < e // Q_BLOCK))(functools.partial(body, e))
    assert S % seg == 0 and per * Q_BLOCK == seg


def _dsa_kernel(q_ref, aux_ref, kv_ref, ik_ref, o_ref, *, n_keep, seg):
    _for_causal_extent(kv_ref.shape[0], seg,
                       lambda e: _dsa_body(q_ref, aux_ref, kv_ref, ik_ref, o_ref, n_keep=n_keep, E=e))


def _dsa(dsa, dsa_aux, B, S):
    nb = S // Q_BLOCK
    n_keep = min(DSA_TOPK, S // 4)
    return pl.pallas_call(
        functools.partial(_dsa_kernel, n_keep=n_keep, seg=min(CAUSAL_SEG, S)),
        grid=(B, nb),
        in_specs=[
            pl.BlockSpec((Q_BLOCK, 4 * LANE), lambda b, i: (b * nb + i, 0)),
            pl.BlockSpec((Q_BLOCK, LANE), lambda b, i: (b * nb + i, 0)),
            pl.BlockSpec((S, LANE), lambda b, i: (b, 4)),
            pl.BlockSpec((S, LANE), lambda b, i: (b, 5)),
        ],
        out_specs=pl.BlockSpec((Q_BLOCK, BRANCH_W), lambda b, i: (b * nb + i, 0)),
        out_shape=jax.ShapeDtypeStruct((B * S, BRANCH_W), F32),
        compiler_params=_params(("parallel", "arbitrary")),
        name="dsa",
    )(dsa, dsa_aux, dsa, dsa)


def _cmp_kernel(g_ref, w1a_ref, w1b_ref, pos_ref, w1_ref, w2_ref, o_ref):
    g = g_ref[...].astype(BF)
    n = g.shape[0]
    a = _dot(g, w1a_ref[...])
    b = _dot(g, w1b_ref[...])
    posterm = _dot(pos_ref[...].astype(BF), w1_ref[...])[0:1, :]
    h = jax.nn.gelu(a + pltpu.roll(b, n - 1, 0) + posterm)
    o_ref[...] = _dot(h.astype(BF), w2_ref[...]).astype(o_ref.dtype)


def _compress(g2, cmp_w1, cmp_w2, cmp_pos):
    _, B, n, W = g2.shape
    Dh = NSA_DH
    half = W
    w1 = cmp_w1.astype(BF)
    pos8 = jnp.broadcast_to(cmp_pos.reshape(2, 1, CMP_LEN * Dh), (2, 8, CMP_LEN * Dh))
    return pl.pallas_call(
        _cmp_kernel,
        grid=(2, B),
        in_specs=[
            pl.BlockSpec((None, None, n, W), lambda i, b: (i, b, 0, 0)),
            pl.BlockSpec((None, half, Dh), lambda i, b: (i, 0, 0)),
            pl.BlockSpec((None, half, Dh), lambda i, b: (i, 1, 0)),
            pl.BlockSpec((None, 8, CMP_LEN * Dh), lambda i, b: (i, 0, 0)),
            pl.BlockSpec((None, CMP_LEN * Dh, Dh), lambda i, b: (i, 0, 0)),
            pl.BlockSpec((None, Dh, Dh), lambda i, b: (i, 0, 0)),
        ],
        out_specs=pl.BlockSpec((None, None, n, Dh), lambda i, b: (i, b, 0, 0)),
        out_shape=jax.ShapeDtypeStruct((2, B, n, Dh), BF),
        compiler_params=_params(("parallel", "parallel")),
        name="nsa_compress",
    )(g2, w1, w1, pos8, w1, cmp_w2.astype(BF))


def _nsa_kernel(q_ref, gt_ref, kvc_ref, sel_ref, win_ref, exp_ref, o_ref, osel_ref, *, n_top, seg):
    Qb, H, Dh = Q_BLOCK, NSA_HEADS, NSA_DH
    S = sel_ref.shape[0]
    n_cmp = kvc_ref.shape[1]
    n_blk = S // SEL_LEN
    bi = pl.program_id(1)
    qpos = bi * Qb + lax.broadcasted_iota(jnp.int32, (Qb, 1), 0)
    q = q_ref[...]

    kc = kvc_ref[0]
    vc = kvc_ref[1]
    cidx = lax.broadcasted_iota(jnp.int32, (1, n_cmp), 1)
    vis = cidx * CMP_STRIDE + (CMP_LEN - 1) <= qpos
    o_cmp = []
    psum = jnp.zeros((Qb, n_cmp), F32)
    for h in range(H):
        s = jnp.where(vis, _dot_nt(q[:, h * Dh:(h + 1) * Dh], kc), NEG_INF)
        m = jnp.max(s, -1, keepdims=True)
        m = jnp.where(m > NEG_INF, m, 0.0)
        e = jnp.exp2(s - m)
        p = e / jnp.maximum(jnp.sum(e, -1, keepdims=True), 1e-30)
        psum = psum + p
        o_cmp.append(_dot(p.astype(BF), vc))

    js = lax.broadcasted_iota(jnp.int32, (n_blk, 1), 0) * SEL_LEN
    cs = lax.broadcasted_iota(jnp.int32, (1, n_cmp), 1) * CMP_STRIDE
    ov = jnp.maximum(jnp.minimum(cs + CMP_LEN, js + SEL_LEN) - jnp.maximum(cs, js), 0).astype(F32) / CMP_LEN
    ov = ov.astype(BF)
    imp = sum(_dot_nt(ov, t) for t in _split3(psum))
    blk = lax.broadcasted_iota(jnp.int32, (n_blk, 1), 0)
    sel_shift = SEL_LEN.bit_length() - 1
    cur = jnp.right_shift(bi * Qb + lax.broadcasted_iota(jnp.int32, (1, Qb), 1), sel_shift)
    forced = (blk == 0) | (blk == cur) | (blk == cur - 1)
    imp = jnp.where(blk <= cur, jnp.where(forced, jnp.inf, imp), NEG_INF)
    rank = jnp.zeros((n_blk, Qb), F32)
    for j in range(n_blk):
        row = imp[j:j + 1, :]
        rank = rank + jnp.where((row > imp) | ((row == imp) & (blk > j)), 1.0, 0.0)
    sub = lax.broadcasted_iota(jnp.int32, (LANE - n_blk, Qb), 0)
    chosen = jnp.concatenate([jnp.where(rank < n_top, SEL_BIG, 0.0), jnp.where(sub < 2, 1.0, 0.0)], axis=0)
    chosen = chosen.T.astype(BF)
    bound = SEL_BIG - 0.5 - qpos.astype(F32)

    def selected(e):
        mask = _dot(chosen, exp_ref[:, 0:e]) > bound
        osel_ref[...] = _masked_attention(q, sel_ref[0:e, :Dh], sel_ref[0:e, Dh:2 * Dh], mask, H, Dh)

    _for_causal_extent(S, seg, selected)
    o_sel = osel_ref[...]

    wlen = WINDOW + Qb
    start = pl.multiple_of(jnp.maximum(bi * Qb - WINDOW, 0), Qb)
    kwin = win_ref[pl.ds(start, wlen), :]
    dlt = qpos - (start + lax.broadcasted_iota(jnp.int32, (1, wlen), 1))
    o_win = _masked_attention(q, kwin[:, :Dh], kwin[:, Dh:2 * Dh], (dlt >= 0) & (dlt < WINDOW), H, Dh)

    g = jax.nn.sigmoid(gt_ref[:, :3 * H])
    outs = []
    for h in range(H):
        outs.append(g[:, 3 * h:3 * h + 1] * o_cmp[h]
                    + g[:, 3 * h + 1:3 * h + 2] * o_sel[:, h * Dh:(h + 1) * Dh]
                    + g[:, 3 * h + 2:3 * h + 3] * o_win[:, h * Dh:(h + 1) * Dh])
    o_ref[...] = jnp.concatenate(outs, axis=-1)


def _nsa(nsa, nsa_aux, kvc, B, S):
    nb = S // Q_BLOCK
    n_cmp = kvc.shape[2]
    n_blk = S // SEL_LEN
    n_top = min(SEL_TOPN, n_blk)
    assert S >= WINDOW + Q_BLOCK and n_blk + 2 <= LANE and S < SEL_BIG and n_blk <= 256
    kpos = np.arange(S)
    expand = np.zeros((LANE, S), np.float32)
    expand[:n_blk] = kpos[None, :] // SEL_LEN == np.arange(n_blk)[:, None]
    expand[n_blk] = -(kpos // SEL_LEN * SEL_LEN)
    expand[n_blk + 1] = -(kpos % SEL_LEN)
    expand = jnp.asarray(expand, BF)
    return pl.pallas_call(
        functools.partial(_nsa_kernel, n_top=n_top, seg=min(CAUSAL_SEG, S)),
        grid=(B, nb),
        in_specs=[
            pl.BlockSpec((Q_BLOCK, 2 * LANE), lambda b, i: (b * nb + i, 0)),
            pl.BlockSpec((Q_BLOCK, LANE), lambda b, i: (b * nb + i, 0)),
            pl.BlockSpec((2, None, n_cmp, NSA_DH), lambda b, i: (0, b, 0, 0)),
            pl.BlockSpec((S, LANE), lambda b, i: (b, 3)),
            pl.BlockSpec((S, LANE), lambda b, i: (b, 4)),
            pl.BlockSpec((LANE, S), lambda b, i: (0, 0)),
        ],
        out_specs=pl.BlockSpec((Q_BLOCK, BRANCH_W), lambda b, i: (b * nb + i, 0)),
        out_shape=jax.ShapeDtypeStruct((B * S, BRANCH_W), F32),
        scratch_shapes=[pltpu.VMEM((Q_BLOCK, BRANCH_W), F32)],
        compiler_params=_params(("parallel", "arbitrary")),
        name="nsa",
    )(nsa, nsa_aux, kvc, nsa, nsa, expand)


def _merge_kernel(x_ref, y0, y1, y2, y3, wg_ref, wb_ref, wo_ref, g_ref, b_ref, o_ref, *, alpha):
    x = x_ref[...]
    xb = x.astype(BF)
    D = x.shape[1]
    merged = jnp.zeros_like(x)
    for n, y_ref in enumerate((y0, y1, y2, y3)):
        gate = jax.nn.sigmoid(_dot(xb, wg_ref[:, n * D:(n + 1) * D]))
        merged = merged + gate * _dot(y_ref[...].astype(BF), wb_ref[n])
    o_ref[...] = _layer_norm(alpha * x + _dot(merged.astype(BF), wo_ref[...]), g_ref[...], b_ref[...])


def _merge(x, ys, w_gate, w_branch, w_out, g, b, alpha, tm=256):
    T, D = x.shape
    full = lambda shape: pl.BlockSpec(shape, lambda i: (0,) * len(shape))
    return pl.pallas_call(
        functools.partial(_merge_kernel, alpha=alpha),
        grid=(T // tm,),
        in_specs=[pl.BlockSpec((tm, D), lambda i: (i, 0))]
        + [pl.BlockSpec((tm, BRANCH_W), lambda i: (i, 0))] * N_BRANCH
        + [full(w_gate.shape), full(w_branch.shape), full(w_out.shape), full((1, D)), full((1, D))],
        out_specs=pl.BlockSpec((tm, D), lambda i: (i, 0)),
        out_shape=jax.ShapeDtypeStruct((T, D), F32),
        compiler_params=_params(("parallel",)),
        name="merge",
    )(x, *ys, w_gate, w_branch, w_out, g, b)


def _matmul_kernel(a_ref, w_ref, o_ref):
    o_ref[...] = _dot(a_ref[...].astype(BF), w_ref[...])


def _matmul(a, w, tm):
    M, K = a.shape
    N = w.shape[1]
    return pl.pallas_call(
        _matmul_kernel,
        grid=(M // tm,),
        in_specs=[pl.BlockSpec((tm, K), lambda i: (i, 0)), pl.BlockSpec((K, N), lambda i: (0, 0))],
        out_specs=pl.BlockSpec((tm, N), lambda i: (i, 0)),
        out_shape=jax.ShapeDtypeStruct((M, N), F32),
        compiler_params=_params(("parallel",)),
        name="kv_proj",
    )(a, w)


def _xattn_kernel(x_ref, kv_ref, wq_ref, wo_ref, g_ref, b_ref, o_ref, *, alpha):
    x = x_ref[...]
    D = x.shape[1]
    dh = D // X_HEADS
    q = _dot(x.astype(BF), wq_ref[...])
    outs = []
    for h in range(X_HEADS):
        k = kv_ref[:, h * dh:(h + 1) * dh].astype(BF)
        v = kv_ref[:, D + h * dh:D + (h + 1) * dh].astype(BF)
        s = _dot_nt(q[:, h * dh:(h + 1) * dh].astype(BF), k) * (dh ** -0.5)
        e = jnp.exp(s - jnp.max(s, -1, keepdims=True))
        outs.append(_dot(e.astype(BF), v) / jnp.sum(e, -1, keepdims=True))
    att = jnp.concatenate(outs, axis=-1).astype(BF)
    o_ref[...] = _layer_norm(alpha * x + _dot(att, wo_ref[...]), g_ref[...], b_ref[...])


def _xattn(x, kv, wq, wo, g, b, alpha, S, M, tm=512):
    T, D = x.shape
    per = S // tm
    full = lambda shape: pl.BlockSpec(shape, lambda i: (0,) * len(shape))
    return pl.pallas_call(
        functools.partial(_xattn_kernel, alpha=alpha),
        grid=(T // tm,),
        in_specs=[pl.BlockSpec((tm, D), lambda i: (i, 0)),
                  pl.BlockSpec((M, 2 * D), lambda i: (i // per, 0)),
                  full(wq.shape), full(wo.shape), full((1, D)), full((1, D))],
        out_specs=pl.BlockSpec((tm, D), lambda i: (i, 0)),
        out_shape=jax.ShapeDtypeStruct((T, D), F32),
        compiler_params=_params(("parallel",)),
        name="xattn",
    )(x, kv, wq, wo, g, b)


def _rope_tables(S):
    pos = jnp.arange(S).astype(F32)

    def base(rot, theta):
        half = rot // 2
        inv = jnp.power(jnp.float32(theta), -2.0 * jnp.arange(half, dtype=F32) / rot)
        ang = pos[:, None] * inv[None, :]
        return jnp.cos(ang), jnp.sin(ang)

    def tile(hd, rot, theta, width, scale=1.0):
        c, s = base(rot, theta)
        lane = np.arange(LANE)
        jj = lane % hd
        half = rot // 2
        first = (jj < half) & (lane < width)
        second = (jj >= half) & (jj < rot) & (lane < width)
        idx = jj % half
        ct = jnp.where((first | second)[None, :], c[:, idx], 1.0) * scale
        s_hi = jnp.where(first[None, :], -s[:, idx], 0.0) * scale
        s_lo = jnp.where(second[None, :], s[:, idx], 0.0) * scale
        return ct, s_hi, s_lo

    q_scale = DSA_DH ** -0.5 * LOG2E
    tabs = [
        tile(RET_DK, RET_DK, RET_THETA, LANE),
        tile(RET_DK, RET_DK, RET_THETA, LANE, RET_DK ** -0.5),
        tile(DSA_DH, DSA_DH // ROPE_FRAC, ROPE_THETA, LANE, q_scale),
        tile(DSA_DH, DSA_DH // ROPE_FRAC, ROPE_THETA, DSA_DH),
        tile(IDX_DH, IDX_DH // ROPE_FRAC, ROPE_THETA, LANE),
        tile(IDX_DH, IDX_DH // ROPE_FRAC, ROPE_THETA, IDX_DH),
    ]
    return tuple(jnp.concatenate([t[i] for t in tabs], 1) for i in range(3))


def _pack_w_in(w):
    D = w.shape[0]
    wb = w.astype(BF)
    main = []
    for _, segs, width in _GROUPS:
        used = 0
        for s in segs:
            start, n = _SEG[s]
            main.append(wb[:, start:start + n])
            used += n
        main.append(jnp.zeros((D, width - used), BF))
    return jnp.concatenate(main, 1), wb[:, GATE_START:]


def kernel(x, mem, ln_g, ln_b, ffn1_w_gu, ffn1_w_down, w_in, cmp_w1, cmp_w2, cmp_pos, conv_w, conv_b,
           dt_bias, a_log, d_skip, ssm_norm_g, w_branch, w_out, xattn_wq, xattn_wkv, xattn_wo,
           ffn2_w_gu, ffn2_w_down):
    B, S, D = x.shape
    M = mem.shape[1]
    depth = ln_g.shape[0]
    alpha = (2 * depth) ** 0.25
    T = B * S
    tables = _rope_tables(S)
    h = x.reshape(T, D)
    mem2 = mem.reshape(B * M, D)
    for l in range(depth):
        lg = lambda i: ln_g[l, i][None, :]
        lb = lambda i: ln_b[l, i][None, :]
        h = _ffn(h, ffn1_w_gu[l].astype(BF), ffn1_w_down[l].astype(BF), lg(0), lb(0), alpha)

        wm, w_gate = _pack_w_in(w_in[l])
        ret, dsa, nsa, ssd, dsa_aux, nsa_aux = _inproj(h, wm, tables, S)
        y_ret = _retention(ret, B, S)
        y_dsa = _dsa(dsa, dsa_aux, B, S)
        g2 = jnp.stack([nsa[:, 2 * LANE:2 * LANE + NSA_DH], nsa[:, 2 * LANE + NSA_DH:3 * LANE]])
        kvc = _compress(g2.reshape(2, B, S // CMP_STRIDE, CMP_STRIDE * NSA_DH), cmp_w1[l], cmp_w2[l], cmp_pos[l])
        y_nsa = _nsa(nsa, nsa_aux, kvc, B, S)
        dt0 = SSM_INNER + SSM_CONV_DIM
        dt_rows = jnp.pad(ssd[:, dt0:dt0 + SSM_HEADS].reshape(B, S // SSM_CHUNK, SSM_CHUNK, SSM_HEADS)
                          .transpose(0, 1, 3, 2), ((0, 0), (0, 0), (0, CONV_PAD - SSM_HEADS), (0, 0)))
        y_ssd = _ssd(ssd, dt_rows, conv_w[l], conv_b[l], dt_bias[l], a_log[l], d_skip[l], ssm_norm_g[l], B, S)
        h = _merge(h, (y_ret, y_dsa, y_nsa, y_ssd), w_gate, w_branch[l].astype(BF), w_out[l].astype(BF),
                   lg(1), lb(1), alpha)

        kv = _matmul(mem2, xattn_wkv[l].astype(BF), tm=min(256, B * M))
        h = _xattn(h, kv, xattn_wq[l].astype(BF), xattn_wo[l].astype(BF), lg(2), lb(2), alpha, S, M)
        h = _ffn(h, ffn2_w_gu[l].astype(BF), ffn2_w_down[l].astype(BF), lg(3), lb(3), alpha)
    return h.reshape(B, S, D)
```

```python
import functools
import math

import numpy as np
import jax
import jax.numpy as jnp
from jax import lax
from jax.experimental import pallas as pl
from jax.experimental.pallas import tpu as pltpu

F32 = jnp.float32
BF = jnp.bfloat16
NEG_INF = float("-inf")

Q_BLOCK = 256
ROPE_THETA = 500000.0
ROPE_FRAC = 4
LN_EPS = 1e-5
RET_HEADS, RET_DK, RET_DV, RET_CHUNK, RET_THETA = 4, 32, 64, 128, 10000.0
DSA_HEADS, DSA_DH, IDX_HEADS, IDX_DH, DSA_TOPK = 4, 64, 8, 32, 256
NSA_HEADS, NSA_DH, CMP_LEN, CMP_STRIDE, SEL_LEN, SEL_TOPN, WINDOW = 4, 64, 32, 16, 64, 16, 512
SSM_HEADS, SSM_HEADDIM, SSM_GROUPS, SSM_STATE, SSM_CONV, SSM_CHUNK = 4, 64, 2, 128, 4, 128
SSM_INNER = SSM_HEADS * SSM_HEADDIM
SSM_CONV_DIM = SSM_INNER + 2 * SSM_GROUPS * SSM_STATE
N_BRANCH = 4
BRANCH_W = 256
X_HEADS = 4

LANE = 128
CONV_PAD = 8
VMEM_LIMIT = 56 * 1024 * 1024

_SEG_NAMES = ("r_q", "r_k", "r_v", "r_g", "d_q", "d_k", "d_v", "i_q", "i_k", "i_w",
              "n_q", "n_kc", "n_vc", "n_ks", "n_vs", "n_kw", "n_vw", "n_g", "s_z", "s_xbc", "s_dt")
_SEG_WIDTHS = (RET_HEADS * RET_DK, RET_HEADS * RET_DK, RET_HEADS * RET_DV, RET_HEADS * RET_DV,
               DSA_HEADS * DSA_DH, DSA_DH, DSA_DH, IDX_HEADS * IDX_DH, IDX_DH, IDX_HEADS,
               NSA_HEADS * NSA_DH, NSA_DH, NSA_DH, NSA_DH, NSA_DH, NSA_DH, NSA_DH, NSA_HEADS * 3,
               SSM_INNER, SSM_CONV_DIM, SSM_HEADS)
_SEG = {}
_o = 0
for _n, _w in zip(_SEG_NAMES, _SEG_WIDTHS):
    _SEG[_n] = (_o, _w)
    _o += _w
GATE_START = _o

_GROUPS = (
    ("ret", ("r_q", "r_k", "r_v", "r_g"), 768),
    ("dsa", ("d_q", "i_q", "d_k", "d_v", "i_k", "i_w"), 768),
    ("nsa", ("n_q", "n_kc", "n_vc", "n_ks", "n_vs", "n_kw", "n_vw", "n_g"), 768),
    ("ssd", ("s_z", "s_xbc", "s_dt"), 1152),
)
_ROPE = {
    "r_q": (RET_DK, RET_DK, 0), "r_k": (RET_DK, RET_DK, 1),
    "d_q": (DSA_DH, DSA_DH // ROPE_FRAC, 2), "n_q": (NSA_DH, NSA_DH // ROPE_FRAC, 2),
    "d_k": (DSA_DH, DSA_DH // ROPE_FRAC, 3), "n_kc": (NSA_DH, NSA_DH // ROPE_FRAC, 3),
    "n_ks": (NSA_DH, NSA_DH // ROPE_FRAC, 3), "n_kw": (NSA_DH, NSA_DH // ROPE_FRAC, 3),
    "i_q": (IDX_DH, IDX_DH // ROPE_FRAC, 4), "i_k": (IDX_DH, IDX_DH // ROPE_FRAC, 5),
}
N_TABLES = 6
_TABLE_HALF = (RET_DK // 2, RET_DK // 2, DSA_DH // ROPE_FRAC // 2, DSA_DH // ROPE_FRAC // 2,
               IDX_DH // ROPE_FRAC // 2, IDX_DH // ROPE_FRAC // 2)
LOG2E = math.log2(math.e)


def _build_layout():
    tile_seg = []
    group_tiles = []
    for _, segs, width in _GROUPS:
        cols = []
        for s in segs:
            cols += [s] * _SEG[s][1]
        cols += [None] * (width - len(cols))
        tile_seg += [cols[t * LANE:(t + 1) * LANE] for t in range(width // LANE)]
        group_tiles.append(width // LANE)
    roped_tiles = [t for t, segs in enumerate(tile_seg) if any(s in _ROPE for s in segs)]
    tile_table = [_ROPE[tile_seg[t][0]][2] for t in roped_tiles]
    return roped_tiles, tile_table, group_tiles, len(tile_seg) * LANE


_ROPED_TILES, _TILE_TABLE, _GROUP_TILES, N_MAIN = _build_layout()


def _dot(a, b):
    return jnp.dot(a, b, preferred_element_type=F32)


def _dot_nt(a, b):
    return lax.dot_general(a, b, (((1,), (1,)), ((), ())), preferred_element_type=F32)


def _dot_tn(a, b):
    return lax.dot_general(a, b, (((0,), (0,)), ((), ())), preferred_element_type=F32)


def _split3(a):
    hi = a.astype(BF)
    r1 = a - hi.astype(F32)
    mid = r1.astype(BF)
    lo = (r1 - mid.astype(F32)).astype(BF)
    return hi, mid, lo


def _layer_norm(v, g, b):
    mu = jnp.mean(v, -1, keepdims=True)
    d = v - mu
    var = jnp.mean(d * d, -1, keepdims=True)
    return d * lax.rsqrt(var + LN_EPS) * g + b


def _silu(v):
    return v * jax.nn.sigmoid(v)


def _params(sem):
    return pltpu.CompilerParams(dimension_semantics=sem, vmem_limit_bytes=VMEM_LIMIT)


def _ffn_kernel(x_ref, wg_ref, wu_ref, wd_ref, g_ref, b_ref, o_ref, acc_ref, *, alpha):
    j = pl.program_id(1)

    @pl.when(j == 0)
    def _():
        acc_ref[...] = jnp.zeros_like(acc_ref)

    xb = x_ref[...].astype(BF)
    gate = _dot(xb, wg_ref[...])
    up = _dot(xb, wu_ref[...])
    h = (_silu(gate) * up).astype(BF)
    acc_ref[...] += _dot(h, wd_ref[...])

    @pl.when(j == pl.num_programs(1) - 1)
    def _():
        o_ref[...] = _layer_norm(alpha * x_ref[...] + 0.5 * acc_ref[...], g_ref[...], b_ref[...])


def _ffn(x, w_gu, w_down, g, b, alpha, tm=512):
    T, D = x.shape
    F = w_down.shape[0]
    fc = F // 2 if (F // 2) % LANE == 0 else F
    nf = F // fc
    return pl.pallas_call(
        functools.partial(_ffn_kernel, alpha=alpha),
        grid=(T // tm, nf),
        in_specs=[
            pl.BlockSpec((tm, D), lambda i, j: (i, 0)),
            pl.BlockSpec((D, fc), lambda i, j: (0, j)),
            pl.BlockSpec((D, fc), lambda i, j: (0, j + nf)),
            pl.BlockSpec((fc, D), lambda i, j: (j, 0)),
            pl.BlockSpec((1, D), lambda i, j: (0, 0)),
            pl.BlockSpec((1, D), lambda i, j: (0, 0)),
        ],
        out_specs=pl.BlockSpec((tm, D), lambda i, j: (i, 0)),
        out_shape=jax.ShapeDtypeStruct((T, D), F32),
        scratch_shapes=[pltpu.VMEM((tm, D), F32)],
        compiler_params=_params(("parallel", "arbitrary")),
        name="ffn",
    )(x, w_gu, w_gu, w_down, g, b)


AUX_TILE = 5


def _inproj_kernel(x_ref, wm_ref, cos_ref, shi_ref, slo_ref, o_ret, o_dsa, o_nsa, o_ssd, o_dsa_aux, o_nsa_aux):
    xb = x_ref[...].astype(BF)
    outs = (o_ret, o_dsa, o_nsa, o_ssd)
    aux = (None, o_dsa_aux, o_nsa_aux, None)
    t0 = 0
    for o_ref, aux_ref, nt in zip(outs, aux, _GROUP_TILES):
        y = _dot(xb, wm_ref[:, t0 * LANE:(t0 + nt) * LANE])
        for t in range(nt):
            yt = y[:, t * LANE:(t + 1) * LANE]
            gt = t0 + t
            if gt in _ROPED_TILES:
                tab = _TILE_TABLE[_ROPED_TILES.index(gt)]
                half = _TABLE_HALF[tab]
                lanes = slice(tab * LANE, (tab + 1) * LANE)
                yt = (yt * cos_ref[:, lanes] + pltpu.roll(yt, LANE - half, 1) * shi_ref[:, lanes]
                      + pltpu.roll(yt, half, 1) * slo_ref[:, lanes])
            o_ref[:, t * LANE:(t + 1) * LANE] = yt.astype(o_ref.dtype)
            if aux_ref is not None and t == AUX_TILE:
                aux_ref[...] = yt
        t0 += nt


def _inproj(x, wm, tables, S, tm=256):
    T, D = x.shape
    ns = S // tm
    widths = [nt * LANE for nt in _GROUP_TILES] + [LANE, LANE]
    dtypes = [F32, BF, BF, F32, F32, F32]
    table_spec = pl.BlockSpec((tm, N_TABLES * LANE), lambda i: (i % ns, 0))
    return pl.pallas_call(
        _inproj_kernel,
        grid=(T // tm,),
        in_specs=[
            pl.BlockSpec((tm, D), lambda i: (i, 0)),
            pl.BlockSpec((D, N_MAIN), lambda i: (0, 0)),
            table_spec, table_spec, table_spec,
        ],
        out_specs=[pl.BlockSpec((tm, w), lambda i: (i, 0)) for w in widths],
        out_shape=[jax.ShapeDtypeStruct((T, w), dt) for w, dt in zip(widths, dtypes)],
        compiler_params=_params(("parallel",)),
        name="in_proj",
    )(x, wm, *tables)


def _ret_kernel(r_ref, o_ref, st_ref):
    C = RET_CHUNK

    @pl.when(pl.program_id(0) == 0)
    def _():
        st_ref[...] = jnp.zeros_like(st_ref)

    hq = RET_HEADS * RET_DK
    rel = (lax.broadcasted_iota(jnp.int32, (C, C), 0) - lax.broadcasted_iota(jnp.int32, (C, C), 1)).astype(F32)
    row = lax.broadcasted_iota(jnp.int32, (C, 1), 0).astype(F32)
    head_consts = []
    for h in range(RET_HEADS):
        lg = math.log1p(-(2.0 ** (-5.0 - h)))
        head_consts.append((jnp.where(rel >= 0, jnp.exp(jnp.maximum(rel, 0.0) * lg), 0.0),
                            jnp.exp((C - 1 - row) * lg), jnp.exp((row + 1.0) * lg), math.exp(C * lg)))
    for b in range(r_ref.shape[0]):
        r = r_ref[b]
        q, k = r[:, :hq], r[:, hq:2 * hq]
        v = r[:, 2 * hq:2 * hq + RET_HEADS * RET_DV]
        g = r[:, 2 * hq + RET_HEADS * RET_DV:]
        outs = []
        for h, (decay, zeta, xi, chunk_decay) in enumerate(head_consts):
            qh = q[:, h * RET_DK:(h + 1) * RET_DK].astype(BF)
            kh = k[:, h * RET_DK:(h + 1) * RET_DK]
            vh = v[:, h * RET_DV:(h + 1) * RET_DV].astype(BF)
            sc = _dot_nt(qh, kh.astype(BF)) * decay
            intra = _dot(sc.astype(BF), vh)
            prev = st_ref[b * RET_HEADS + h]
            cross = _dot(qh, prev.astype(BF)) * xi
            st_ref[b * RET_HEADS + h] = prev * chunk_decay + _dot_tn((kh * zeta).astype(BF), vh)
            o = intra + cross
            mu = jnp.mean(o, -1, keepdims=True)
            d = o - mu
            var = jnp.mean(d * d, -1, keepdims=True)
            outs.append(d * lax.rsqrt(var + LN_EPS))
        o_ref[b] = _silu(g) * jnp.concatenate(outs, axis=-1)


def _retention(ret, B, S):
    n = S // RET_CHUNK
    W = ret.shape[1]
    return pl.pallas_call(
        _ret_kernel,
        grid=(n,),
        in_specs=[pl.BlockSpec((B, RET_CHUNK, W), lambda c: (0, c, 0))],
        out_specs=pl.BlockSpec((B, RET_CHUNK, BRANCH_W), lambda c: (0, c, 0)),
        out_shape=jax.ShapeDtypeStruct((B, S, BRANCH_W), F32),
        scratch_shapes=[pltpu.VMEM((B * RET_HEADS, RET_DK, RET_DV), F32)],
        compiler_params=_params(("arbitrary",)),
        name="retention",
    )(ret.reshape(B, S, W)).reshape(B * S, BRANCH_W)


def _softplus(v):
    return jnp.maximum(v, 0.0) + jnp.log1p(jnp.exp(-jnp.abs(v)))


def _ssd_kernel(s_ref, dtr_ref, cw_ref, cb_ref, dtb_c_ref, a_c_ref, dtb_r_ref, a_r_ref, dsk_ref, ng_ref,
                o_ref, st_ref, xpad_ref):
    Q, H, P, N = SSM_CHUNK, SSM_HEADS, SSM_HEADDIM, SSM_STATE

    @pl.when(pl.program_id(0) == 0)
    def _():
        st_ref[...] = jnp.zeros_like(st_ref)
        xpad_ref[:, 0:CONV_PAD, :] = jnp.zeros((xpad_ref.shape[0], CONV_PAD, SSM_CONV_DIM), F32)

    ri = lax.broadcasted_iota(jnp.int32, (Q, Q), 0)
    ci = lax.broadcasted_iota(jnp.int32, (Q, Q), 1)
    tril = ri >= ci
    lo_tri = jnp.where(tril, 1.0, 0.0).astype(BF)
    up_tri = jnp.where(ci >= ri, 1.0, 0.0).astype(BF)
    gn = SSM_GROUPS * N
    gw = SSM_INNER // SSM_GROUPS
    for b in range(s_ref.shape[0]):
        z = s_ref[b, :, :SSM_INNER]
        xpad_ref[b, CONV_PAD:, :] = s_ref[b, :, SSM_INNER:SSM_INNER + SSM_CONV_DIM]
        dt_raw = s_ref[b, :, SSM_INNER + SSM_CONV_DIM:SSM_INNER + SSM_CONV_DIM + H]
        conv = cb_ref[...]
        for kk in range(SSM_CONV):
            off = CONV_PAD - (SSM_CONV - 1) + kk
            conv = conv + cw_ref[kk:kk + 1, :] * xpad_ref[b, off:off + Q, :]
        xpad_ref[b, 0:CONV_PAD, :] = xpad_ref[b, Q:Q + CONV_PAD, :]
        xc = _silu(conv)
        xs = xc[:, :SSM_INNER]
        bm = xc[:, SSM_INNER:SSM_INNER + gn]
        cm = xc[:, SSM_INNER + gn:]

        dt_c = _softplus(dt_raw + dtb_c_ref[...])
        adt_c = dt_c * a_c_ref[...]
        dt_r = _softplus(dtr_ref[b] + dtb_r_ref[...])
        adt_r = dt_r * a_r_ref[...]
        acs_c = sum(_dot(lo_tri, t) for t in _split3(adt_c))
        acs_r = sum(_dot(t, up_tri) for t in _split3(adt_r))

        outs = []
        for grp in range(SSM_GROUPS):
            bg = bm[:, grp * N:(grp + 1) * N]
            cg = cm[:, grp * N:(grp + 1) * N].astype(BF)
            cb = _dot_nt(cg, bg.astype(BF))
            for h in range(grp * (H // SSM_GROUPS), (grp + 1) * (H // SSM_GROUPS)):
                a_col = acs_c[:, h:h + 1]
                a_last = acs_c[Q - 1:Q, h:h + 1]
                lm = jnp.exp(jnp.where(tril, a_col - acs_r[h:h + 1, :], NEG_INF))
                xh = xs[:, h * P:(h + 1) * P]
                xdt = (xh * dt_c[:, h:h + 1]).astype(BF)
                y = _dot((cb * lm).astype(BF), xdt)
                prev = st_ref[b * H + h]
                y = y + _dot(cg, prev.astype(BF)) * jnp.exp(a_col)
                st_ref[b * H + h] = (prev * jnp.exp(a_last)
                                     + _dot_tn((bg * jnp.exp(a_last - a_col)).astype(BF), xdt))
                outs.append(y)
        y = jnp.concatenate(outs, axis=-1) + xs * dsk_ref[...]
        y = y * _silu(z)
        normed = []
        for grp in range(SSM_GROUPS):
            yg = y[:, grp * gw:(grp + 1) * gw]
            normed.append(yg * lax.rsqrt(jnp.mean(yg * yg, -1, keepdims=True) + LN_EPS))
        o_ref[b] = jnp.concatenate(normed, axis=-1) * ng_ref[...]


def _ssd(ssd, dt_rows, conv_w, conv_b, dt_bias, a_log, d_skip, norm_g, B, S):
    Q, H = SSM_CHUNK, SSM_HEADS
    n = S // Q
    W = ssd.shape[1]
    a = -jnp.exp(a_log.astype(F32))
    pad_r = lambda v: jnp.broadcast_to(jnp.pad(v, (0, CONV_PAD - H))[:, None], (CONV_PAD, Q)).astype(F32)
    full = lambda shape: pl.BlockSpec(shape, lambda c: (0,) * len(shape))
    return pl.pallas_call(
        _ssd_kernel,
        grid=(n,),
        in_specs=[
            pl.BlockSpec((B, Q, W), lambda c: (0, c, 0)),
            pl.BlockSpec((B, None, CONV_PAD, Q), lambda c: (0, c, 0, 0)),
            full((SSM_CONV, SSM_CONV_DIM)), full((1, SSM_CONV_DIM)),
            full((1, H)), full((1, H)), full((CONV_PAD, Q)), full((CONV_PAD, Q)),
            full((1, SSM_INNER)), full((1, SSM_INNER)),
        ],
        out_specs=pl.BlockSpec((B, Q, BRANCH_W), lambda c: (0, c, 0)),
        out_shape=jax.ShapeDtypeStruct((B, S, BRANCH_W), F32),
        scratch_shapes=[pltpu.VMEM((B * H, SSM_STATE, SSM_HEADDIM), F32),
                        pltpu.VMEM((B, Q + CONV_PAD, SSM_CONV_DIM), F32)],
        compiler_params=_params(("arbitrary",)),
        name="ssd",
    )(ssd.reshape(B, S, W), dt_rows, conv_w, conv_b[None, :], dt_bias[None, :], a[None, :], pad_r(dt_bias),
      pad_r(a), jnp.repeat(d_skip, SSM_HEADDIM)[None, :], norm_g[None, :]).reshape(B * S, BRANCH_W)


def _masked_attention(q_bf, k_bf, v_bf, mask, heads, dh):
    outs = []
    for h in range(heads):
        s = jnp.where(mask, _dot_nt(q_bf[:, h * dh:(h + 1) * dh], k_bf), NEG_INF)
        m = jnp.max(s, -1, keepdims=True)
        m = jnp.where(m > NEG_INF, m, 0.0)
        e = jnp.exp2(s - m)
        l = jnp.maximum(jnp.sum(e, -1, keepdims=True), 1e-30)
        outs.append(_dot(e.astype(BF), v_bf) / l)
    return jnp.concatenate(outs, axis=-1)


def _count(m):
    return jnp.sum(jnp.where(m, 1.0, 0.0), -1, keepdims=True)


SEL_BIG = float(2 ** 20)
CAUSAL_SEG = 1024
BISECT_PLAIN_STEPS = 26
BISECT_MAX_STEPS = 400


def _kth_threshold(sc, nvalid, vmin, vmax, k):
    kf = float(k)
    short = nvalid <= kf
    c_ge0 = _count(sc >= 0.0)
    c_gt0 = _count(sc > 0.0)
    up0 = c_ge0 >= kf
    tie0 = up0 & (c_gt0 < kf) & (c_ge0 > kf) & (nvalid > kf)
    lo0 = jnp.where(short, jnp.finfo(F32).min, jnp.where(up0, 0.0, vmin))
    hi0 = jnp.where(up0, vmax, 0.0)
    done0 = jnp.where(short | tie0 | (c_ge0 == kf), 1.0, 0.0)

    def split(lo, hi, done):
        mid = jnp.where(done > 0.5, lo, lo + (hi - lo) * 0.5)
        c = _count(sc >= mid)
        up = (c >= kf) & (done < 0.5)
        return jnp.where(up, mid, lo), jnp.where(up | (done > 0.5), hi, mid), jnp.where(up & (c == kf), 1.0, done)

    def plain_cond(st):
        return (st[0] < BISECT_PLAIN_STEPS) & (jnp.min(st[3]) < 0.5)

    def plain_step(st):
        it, lo, hi, done = st
        lo, hi, done = split(lo, hi, done)
        lo, hi, done = split(lo, hi, done)
        return it + 2, lo, hi, done

    _, lo, hi, done = lax.while_loop(plain_cond, plain_step, (jnp.int32(0), lo0, hi0, done0))

    def exact_cond(st):
        return (st[0] < BISECT_MAX_STEPS) & (jnp.min(st[4]) < 0.5)

    def exact_step(st):
        it, lo, hi, tie, done = st
        cand = jnp.min(jnp.where(sc >= lo, sc, jnp.inf), -1, keepdims=True)
        fin = (_count(sc > cand) < kf) & (done < 0.5)
        lo = jnp.where(fin, cand, lo)
        tie = jnp.where(fin, 1.0, tie)
        done = jnp.where(fin, 1.0, done)
        lo, hi, done = split(lo, hi, done)
        return it + 1, lo, hi, tie, done

    _, thr, _, tie, _ = lax.while_loop(
        exact_cond, exact_step, (jnp.int32(0), lo, hi, jnp.where(tie0, 1.0, 0.0), done))
    return thr, tie


def _dsa_body(q_ref, aux_ref, kv_ref, ik_ref, o_ref, *, n_keep, E):
    Qb = Q_BLOCK
    bi = pl.program_id(1)
    qpos = bi * Qb + lax.broadcasted_iota(jnp.int32, (Qb, 1), 0)
    kpos = lax.broadcasted_iota(jnp.int32, (1, E), 1)
    causal = kpos <= qpos
    hq = DSA_HEADS * DSA_DH
    iw = aux_ref[:, IDX_DH:IDX_DH + IDX_HEADS] * (IDX_HEADS ** -0.5) * (IDX_DH ** -0.5)
    ik = ik_ref[0:E, :IDX_DH]
    score = jnp.zeros((Qb, E), F32)
    for h in range(IDX_HEADS):
        rel = jnp.maximum(_dot_nt(q_ref[:, hq + h * IDX_DH:hq + (h + 1) * IDX_DH], ik), 0.0)
        score = score + rel * iw[:, h:h + 1]
    sc = jnp.where(causal, score, NEG_INF)
    vmax = jnp.max(sc, -1, keepdims=True)
    vmin = jnp.min(jnp.where(causal, score, jnp.inf), -1, keepdims=True)
    thr, tie = _kth_threshold(sc, (qpos + 1).astype(F32), vmin, vmax, n_keep)

    def resolve_ties():
        gt = sc > thr
        eqf = jnp.where(sc == thr, 1.0, 0.0)
        need = n_keep - _count(gt)
        ch = 256 if E % 256 == 0 else LANE
        before = jnp.where(lax.broadcasted_iota(jnp.int32, (ch, ch), 0)
                           < lax.broadcasted_iota(jnp.int32, (ch, ch), 1), 1.0, 0.0).astype(BF)
        run = jnp.zeros((Qb, 1), F32)
        take = []
        for c0 in range(0, E, ch):
            eqc = eqf[:, c0:c0 + ch]
            prefix = _dot(eqc.astype(BF), before) + run
            take.append(jnp.where(prefix < need, eqc, 0.0))
            run = run + jnp.sum(eqc, -1, keepdims=True)
        return jnp.where(gt | (jnp.concatenate(take, axis=-1) > 0.5), jnp.inf, NEG_INF)

    picked = lax.cond(jnp.max(tie) > 0.5, resolve_ties, lambda: sc)
    o_ref[...] = _masked_attention(q_ref[:, :hq], kv_ref[0:E, :DSA_DH], kv_ref[0:E, DSA_DH:2 * DSA_DH],
                                   picked >= thr, DSA_HEADS, DSA_DH)


def _for_causal_extent(S, seg, body):
    bi = pl.program_id(1)
    per = seg // Q_BLOCK
    for e in range(seg, S + 1, seg):
        pl.when((bi >= (e - seg) // Q_BLOCK) & (bi < e // Q_BLOCK))(functools.partial(body, e))
    assert S % seg == 0 and per * Q_BLOCK == seg


def _dsa_kernel(q_ref, aux_ref, kv_ref, ik_ref, o_ref, *, n_keep, seg):
    _for_causal_extent(kv_ref.shape[0], seg,
                       lambda e: _dsa_body(q_ref, aux_ref, kv_ref, ik_ref, o_ref, n_keep=n_keep, E=e))


def _dsa(dsa, dsa_aux, B, S):
    nb = S // Q_BLOCK
    n_keep = min(DSA_TOPK, S // 4)
    return pl.pallas_call(
        functools.partial(_dsa_kernel, n_keep=n_keep, seg=min(CAUSAL_SEG, S)),
        grid=(B, nb),
        in_specs=[
            pl.BlockSpec((Q_BLOCK, 4 * LANE), lambda b, i: (b * nb + i, 0)),
            pl.BlockSpec((Q_BLOCK, LANE), lambda b, i: (b * nb + i, 0)),
            pl.BlockSpec((S, LANE), lambda b, i: (b, 4)),
            pl.BlockSpec((S, LANE), lambda b, i: (b, 5)),
        ],
        out_specs=pl.BlockSpec((Q_BLOCK, BRANCH_W), lambda b, i: (b * nb + i, 0)),
        out_shape=jax.ShapeDtypeStruct((B * S, BRANCH_W), F32),
        compiler_params=_params(("parallel", "arbitrary")),
        name="dsa",
    )(dsa, dsa_aux, dsa, dsa)


def _cmp_kernel(g_ref, w1a_ref, w1b_ref, pos_ref, w1_ref, w2_ref, o_ref):
    g = g_ref[...].astype(BF)
    n = g.shape[0]
    a = _dot(g, w1a_ref[...])
    b = _dot(g, w1b_ref[...])
    posterm = _dot(pos_ref[...].astype(BF), w1_ref[...])[0:1, :]
    h = jax.nn.gelu(a + pltpu.roll(b, n - 1, 0) + posterm)
    o_ref[...] = _dot(h.astype(BF), w2_ref[...]).astype(o_ref.dtype)


def _compress(g2, cmp_w1, cmp_w2, cmp_pos):
    _, B, n, W = g2.shape
    Dh = NSA_DH
    half = W
    w1 = cmp_w1.astype(BF)
    pos8 = jnp.broadcast_to(cmp_pos.reshape(2, 1, CMP_LEN * Dh), (2, 8, CMP_LEN * Dh))
    return pl.pallas_call(
        _cmp_kernel,
        grid=(2, B),
        in_specs=[
            pl.BlockSpec((None, None, n, W), lambda i, b: (i, b, 0, 0)),
            pl.BlockSpec((None, half, Dh), lambda i, b: (i, 0, 0)),
            pl.BlockSpec((None, half, Dh), lambda i, b: (i, 1, 0)),
            pl.BlockSpec((None, 8, CMP_LEN * Dh), lambda i, b: (i, 0, 0)),
            pl.BlockSpec((None, CMP_LEN * Dh, Dh), lambda i, b: (i, 0, 0)),
            pl.BlockSpec((None, Dh, Dh), lambda i, b: (i, 0, 0)),
        ],
        out_specs=pl.BlockSpec((None, None, n, Dh), lambda i, b: (i, b, 0, 0)),
        out_shape=jax.ShapeDtypeStruct((2, B, n, Dh), BF),
        compiler_params=_params(("parallel", "parallel")),
        name="nsa_compress",
    )(g2, w1, w1, pos8, w1, cmp_w2.astype(BF))


def _nsa_kernel(q_ref, gt_ref, kvc_ref, sel_ref, win_ref, exp_ref, o_ref, osel_ref, *, n_top, seg):
    Qb, H, Dh = Q_BLOCK, NSA_HEADS, NSA_DH
    S = sel_ref.shape[0]
    n_cmp = kvc_ref.shape[1]
    n_blk = S // SEL_LEN
    bi = pl.program_id(1)
    qpos = bi * Qb + lax.broadcasted_iota(jnp.int32, (Qb, 1), 0)
    q = q_ref[...]

    kc = kvc_ref[0]
    vc = kvc_ref[1]
    cidx = lax.broadcasted_iota(jnp.int32, (1, n_cmp), 1)
    vis = cidx * CMP_STRIDE + (CMP_LEN - 1) <= qpos
    o_cmp = []
    psum = jnp.zeros((Qb, n_cmp), F32)
    for h in range(H):
        s = jnp.where(vis, _dot_nt(q[:, h * Dh:(h + 1) * Dh], kc), NEG_INF)
        m = jnp.max(s, -1, keepdims=True)
        m = jnp.where(m > NEG_INF, m, 0.0)
        e = jnp.exp2(s - m)
        p = e / jnp.maximum(jnp.sum(e, -1, keepdims=True), 1e-30)
        psum = psum + p
        o_cmp.append(_dot(p.astype(BF), vc))

    js = lax.broadcasted_iota(jnp.int32, (n_blk, 1), 0) * SEL_LEN
    cs = lax.broadcasted_iota(jnp.int32, (1, n_cmp), 1) * CMP_STRIDE
    ov = jnp.maximum(jnp.minimum(cs + CMP_LEN, js + SEL_LEN) - jnp.maximum(cs, js), 0).astype(F32) / CMP_LEN
    ov = ov.astype(BF)
    imp = sum(_dot_nt(ov, t) for t in _split3(psum))
    blk = lax.broadcasted_iota(jnp.int32, (n_blk, 1), 0)
    sel_shift = SEL_LEN.bit_length() - 1
    cur = jnp.right_shift(bi * Qb + lax.broadcasted_iota(jnp.int32, (1, Qb), 1), sel_shift)
    forced = (blk == 0) | (blk == cur) | (blk == cur - 1)
    imp = jnp.where(blk <= cur, jnp.where(forced, jnp.inf, imp), NEG_INF)
    rank = jnp.zeros((n_blk, Qb), F32)
    for j in range(n_blk):
        row = imp[j:j + 1, :]
        rank = rank + jnp.where((row > imp) | ((row == imp) & (blk > j)), 1.0, 0.0)
    sub = lax.broadcasted_iota(jnp.int32, (LANE - n_blk, Qb), 0)
    chosen = jnp.concatenate([jnp.where(rank < n_top, SEL_BIG, 0.0), jnp.where(sub < 2, 1.0, 0.0)], axis=0)
    chosen = chosen.T.astype(BF)
    bound = SEL_BIG - 0.5 - qpos.astype(F32)

    def selected(e):
        mask = _dot(chosen, exp_ref[:, 0:e]) > bound
        osel_ref[...] = _masked_attention(q, sel_ref[0:e, :Dh], sel_ref[0:e, Dh:2 * Dh], mask, H, Dh)

    _for_causal_extent(S, seg, selected)
    o_sel = osel_ref[...]

    wlen = WINDOW + Qb
    start = pl.multiple_of(jnp.maximum(bi * Qb - WINDOW, 0), Qb)
    kwin = win_ref[pl.ds(start, wlen), :]
    dlt = qpos - (start + lax.broadcasted_iota(jnp.int32, (1, wlen), 1))
    o_win = _masked_attention(q, kwin[:, :Dh], kwin[:, Dh:2 * Dh], (dlt >= 0) & (dlt < WINDOW), H, Dh)

    g = jax.nn.sigmoid(gt_ref[:, :3 * H])
    outs = []
    for h in range(H):
        outs.append(g[:, 3 * h:3 * h + 1] * o_cmp[h]
                    + g[:, 3 * h + 1:3 * h + 2] * o_sel[:, h * Dh:(h + 1) * Dh]
                    + g[:, 3 * h + 2:3 * h + 3] * o_win[:, h * Dh:(h + 1) * Dh])
    o_ref[...] = jnp.concatenate(outs, axis=-1)


def _nsa(nsa, nsa_aux, kvc, B, S):
    nb = S // Q_BLOCK
    n_cmp = kvc.shape[2]
    n_blk = S // SEL_LEN
    n_top = min(SEL_TOPN, n_blk)
    assert S >= WINDOW + Q_BLOCK and n_blk + 2 <= LANE and S < SEL_BIG and n_blk <= 256
    kpos = np.arange(S)
    expand = np.zeros((LANE, S), np.float32)
    expand[:n_blk] = kpos[None, :] // SEL_LEN == np.arange(n_blk)[:, None]
    expand[n_blk] = -(kpos // SEL_LEN * SEL_LEN)
    expand[n_blk + 1] = -(kpos % SEL_LEN)
    expand = jnp.asarray(expand, BF)
    return pl.pallas_call(
        functools.partial(_nsa_kernel, n_top=n_top, seg=min(CAUSAL_SEG, S)),
        grid=(B, nb),
        in_specs=[
            pl.BlockSpec((Q_BLOCK, 2 * LANE), lambda b, i: (b * nb + i, 0)),
            pl.BlockSpec((Q_BLOCK, LANE), lambda b, i: (b * nb + i, 0)),
            pl.BlockSpec((2, None, n_cmp, NSA_DH), lambda b, i: (0, b, 0, 0)),
            pl.BlockSpec((S, LANE), lambda b, i: (b, 3)),
            pl.BlockSpec((S, LANE), lambda b, i: (b, 4)),
            pl.BlockSpec((LANE, S), lambda b, i: (0, 0)),
        ],
        out_specs=pl.BlockSpec((Q_BLOCK, BRANCH_W), lambda b, i: (b * nb + i, 0)),
        out_shape=jax.ShapeDtypeStruct((B * S, BRANCH_W), F32),
        scratch_shapes=[pltpu.VMEM((Q_BLOCK, BRANCH_W), F32)],
        compiler_params=_params(("parallel", "arbitrary")),
        name="nsa",
    )(nsa, nsa_aux, kvc, nsa, nsa, expand)


def _merge_kernel(x_ref, y0, y1, y2, y3, wg_ref, wb_ref, wo_ref, g_ref, b_ref, o_ref, *, alpha):
    x = x_ref[...]
    xb = x.astype(BF)
    D = x.shape[1]
    merged = jnp.zeros_like(x)
    for n, y_ref in enumerate((y0, y1, y2, y3)):
        gate = jax.nn.sigmoid(_dot(xb, wg_ref[:, n * D:(n + 1) * D]))
        merged = merged + gate * _dot(y_ref[...].astype(BF), wb_ref[n])
    o_ref[...] = _layer_norm(alpha * x + _dot(merged.astype(BF), wo_ref[...]), g_ref[...], b_ref[...])


def _merge(x, ys, w_gate, w_branch, w_out, g, b, alpha, tm=256):
    T, D = x.shape
    full = lambda shape: pl.BlockSpec(shape, lambda i: (0,) * len(shape))
    return pl.pallas_call(
        functools.partial(_merge_kernel, alpha=alpha),
        grid=(T // tm,),
        in_specs=[pl.BlockSpec((tm, D), lambda i: (i, 0))]
        + [pl.BlockSpec((tm, BRANCH_W), lambda i: (i, 0))] * N_BRANCH
        + [full(w_gate.shape), full(w_branch.shape), full(w_out.shape), full((1, D)), full((1, D))],
        out_specs=pl.BlockSpec((tm, D), lambda i: (i, 0)),
        out_shape=jax.ShapeDtypeStruct((T, D), F32),
        compiler_params=_params(("parallel",)),
        name="merge",
    )(x, *ys, w_gate, w_branch, w_out, g, b)


def _matmul_kernel(a_ref, w_ref, o_ref):
    o_ref[...] = _dot(a_ref[...].astype(BF), w_ref[...])


def _matmul(a, w, tm):
    M, K = a.shape
    N = w.shape[1]
    return pl.pallas_call(
        _matmul_kernel,
        grid=(M // tm,),
        in_specs=[pl.BlockSpec((tm, K), lambda i: (i, 0)), pl.BlockSpec((K, N), lambda i: (0, 0))],
        out_specs=pl.BlockSpec((tm, N), lambda i: (i, 0)),
        out_shape=jax.ShapeDtypeStruct((M, N), F32),
        compiler_params=_params(("parallel",)),
        name="kv_proj",
    )(a, w)


def _xattn_kernel(x_ref, kv_ref, wq_ref, wo_ref, g_ref, b_ref, o_ref, *, alpha):
    x = x_ref[...]
    D = x.shape[1]
    dh = D // X_HEADS
    q = _dot(x.astype(BF), wq_ref[...])
    outs = []
    for h in range(X_HEADS):
        k = kv_ref[:, h * dh:(h + 1) * dh].astype(BF)
        v = kv_ref[:, D + h * dh:D + (h + 1) * dh].astype(BF)
        s = _dot_nt(q[:, h * dh:(h + 1) * dh].astype(BF), k) * (dh ** -0.5)
        e = jnp.exp(s - jnp.max(s, -1, keepdims=True))
        outs.append(_dot(e.astype(BF), v) / jnp.sum(e, -1, keepdims=True))
    att = jnp.concatenate(outs, axis=-1).astype(BF)
    o_ref[...] = _layer_norm(alpha * x + _dot(att, wo_ref[...]), g_ref[...], b_ref[...])


def _xattn(x, kv, wq, wo, g, b, alpha, S, M, tm=512):
    T, D = x.shape
    per = S // tm
    full = lambda shape: pl.BlockSpec(shape, lambda i: (0,) * len(shape))
    return pl.pallas_call(
        functools.partial(_xattn_kernel, alpha=alpha),
        grid=(T // tm,),
        in_specs=[pl.BlockSpec((tm, D), lambda i: (i, 0)),
                  pl.BlockSpec((M, 2 * D), lambda i: (i // per, 0)),
                  full(wq.shape), full(wo.shape), full((1, D)), full((1, D))],
        out_specs=pl.BlockSpec((tm, D), lambda i: (i, 0)),
        out_shape=jax.ShapeDtypeStruct((T, D), F32),
        compiler_params=_params(("parallel",)),
        name="xattn",
    )(x, kv, wq, wo, g, b)


def _rope_tables(S):
    pos = jnp.arange(S).astype(F32)

    def base(rot, theta):
        half = rot // 2
        inv = jnp.power(jnp.float32(theta), -2.0 * jnp.arange(half, dtype=F32) / rot)
        ang = pos[:, None] * inv[None, :]
        return jnp.cos(ang), jnp.sin(ang)

    def tile(hd, rot, theta, width, scale=1.0):
        c, s = base(rot, theta)
        lane = np.arange(LANE)
        jj = lane % hd
        half = rot // 2
        first = (jj < half) & (lane < width)
        second = (jj >= half) & (jj < rot) & (lane < width)
        idx = jj % half
        ct = jnp.where((first | second)[None, :], c[:, idx], 1.0) * scale
        s_hi = jnp.where(first[None, :], -s[:, idx], 0.0) * scale
        s_lo = jnp.where(second[None, :], s[:, idx], 0.0) * scale
        return ct, s_hi, s_lo

    q_scale = DSA_DH ** -0.5 * LOG2E
    tabs = [
        tile(RET_DK, RET_DK, RET_THETA, LANE),
        tile(RET_DK, RET_DK, RET_THETA, LANE, RET_DK ** -0.5),
        tile(DSA_DH, DSA_DH // ROPE_FRAC, ROPE_THETA, LANE, q_scale),
        tile(DSA_DH, DSA_DH // ROPE_FRAC, ROPE_THETA, DSA_DH),
        tile(IDX_DH, IDX_DH // ROPE_FRAC, ROPE_THETA, LANE),
        tile(IDX_DH, IDX_DH // ROPE_FRAC, ROPE_THETA, IDX_DH),
    ]
    return tuple(jnp.concatenate([t[i] for t in tabs], 1) for i in range(3))


def _pack_w_in(w):
    D = w.shape[0]
    wb = w.astype(BF)
    main = []
    for _, segs, width in _GROUPS:
        used = 0
        for s in segs:
            start, n = _SEG[s]
            main.append(wb[:, start:start + n])
            used += n
        main.append(jnp.zeros((D, width - used), BF))
    return jnp.concatenate(main, 1), wb[:, GATE_START:]


def kernel(x, mem, ln_g, ln_b, ffn1_w_gu, ffn1_w_down, w_in, cmp_w1, cmp_w2, cmp_pos, conv_w, conv_b,
           dt_bias, a_log, d_skip, ssm_norm_g, w_branch, w_out, xattn_wq, xattn_wkv, xattn_wo,
           ffn2_w_gu, ffn2_w_down):
    B, S, D = x.shape
    M = mem.shape[1]
    depth = ln_g.shape[0]
    alpha = (2 * depth) ** 0.25
    T = B * S
    tables = _rope_tables(S)
    h = x.reshape(T, D)
    mem2 = mem.reshape(B * M, D)
    for l in range(depth):
        lg = lambda i: ln_g[l, i][None, :]
        lb = lambda i: ln_b[l, i][None, :]
        h = _ffn(h, ffn1_w_gu[l].astype(BF), ffn1_w_down[l].astype(BF), lg(0), lb(0), alpha)

        wm, w_gate = _pack_w_in(w_in[l])
        ret, dsa, nsa, ssd, dsa_aux, nsa_aux = _inproj(h, wm, tables, S)
        y_ret = _retention(ret, B, S)
        y_dsa = _dsa(dsa, dsa_aux, B, S)
        g2 = jnp.stack([nsa[:, 2 * LANE:2 * LANE + NSA_DH], nsa[:, 2 * LANE + NSA_DH:3 * LANE]])
        kvc = _compress(g2.reshape(2, B, S // CMP_STRIDE, CMP_STRIDE * NSA_DH), cmp_w1[l], cmp_w2[l], cmp_pos[l])
        y_nsa = _nsa(nsa, nsa_aux, kvc, B, S)
        dt0 = SSM_INNER + SSM_CONV_DIM
        dt_rows = jnp.pad(ssd[:, dt0:dt0 + SSM_HEADS].reshape(B, S // SSM_CHUNK, SSM_CHUNK, SSM_HEADS)
                          .transpose(0, 1, 3, 2), ((0, 0), (0, 0), (0, CONV_PAD - SSM_HEADS), (0, 0)))
        y_ssd = _ssd(ssd, dt_rows, conv_w[l], conv_b[l], dt_bias[l], a_log[l], d_skip[l], ssm_norm_g[l], B, S)
        h = _merge(h, (y_ret, y_dsa, y_nsa, y_ssd), w_gate, w_branch[l].astype(BF), w_out[l].astype(BF),
                   lg(1), lb(1), alpha)

        kv = _matmul(mem2, xattn_wkv[l].astype(BF), tm=min(256, B * M))
        h = _xattn(h, kv, xattn_wq[l].astype(BF), xattn_wo[l].astype(BF), lg(2), lb(2), alpha, S, M)
        h = _ffn(h, ffn2_w_gu[l].astype(BF), ffn2_w_down[l].astype(BF), lg(3), lb(3), alpha)
    return h.reshape(B, S, D)
```

```python
import functools
import math

import numpy as np
import jax
import jax.numpy as jnp
from jax import lax
from jax.experimental import pallas as pl
from jax.experimental.pallas import tpu as pltpu

F32 = jnp.float32
BF = jnp.bfloat16
NEG_INF = float("-inf")

DSA_QB = 128
NSA_QB = 256
ROPE_THETA = 500000.0
ROPE_FRAC = 4
LN_EPS = 1e-5
RET_HEADS, RET_DK, RET_DV, RET_CHUNK, RET_THETA = 4, 32, 64, 128, 10000.0
DSA_HEADS, DSA_DH, IDX_HEADS, IDX_DH, DSA_TOPK = 4, 64, 8, 32, 256
NSA_HEADS, NSA_DH, CMP_LEN, CMP_STRIDE, SEL_LEN, SEL_TOPN, WINDOW = 4, 64, 32, 16, 64, 16, 512
SSM_HEADS, SSM_HEADDIM, SSM_GROUPS, SSM_STATE, SSM_CONV, SSM_CHUNK = 4, 64, 2, 128, 4, 128
SSM_INNER = SSM_HEADS * SSM_HEADDIM
SSM_CONV_DIM = SSM_INNER + 2 * SSM_GROUPS * SSM_STATE
N_BRANCH = 4
BRANCH_W = 256
X_HEADS = 4

LANE = 128
CONV_PAD = 8
VMEM_LIMIT = 56 * 1024 * 1024

_SEG_NAMES = ("r_q", "r_k", "r_v", "r_g", "d_q", "d_k", "d_v", "i_q", "i_k", "i_w",
              "n_q", "n_kc", "n_vc", "n_ks", "n_vs", "n_kw", "n_vw", "n_g", "s_z", "s_xbc", "s_dt")
_SEG_WIDTHS = (RET_HEADS * RET_DK, RET_HEADS * RET_DK, RET_HEADS * RET_DV, RET_HEADS * RET_DV,
               DSA_HEADS * DSA_DH, DSA_DH, DSA_DH, IDX_HEADS * IDX_DH, IDX_DH, IDX_HEADS,
               NSA_HEADS * NSA_DH, NSA_DH, NSA_DH, NSA_DH, NSA_DH, NSA_DH, NSA_DH, NSA_HEADS * 3,
               SSM_INNER, SSM_CONV_DIM, SSM_HEADS)
_SEG = {}
_o = 0
for _n, _w in zip(_SEG_NAMES, _SEG_WIDTHS):
    _SEG[_n] = (_o, _w)
    _o += _w
GATE_START = _o

_GROUPS = (
    ("ret", ("r_q", "r_k", "r_v", "r_g"), 768),
    ("dsa", ("d_q", "i_q", "d_k", "d_v", "i_k", "i_w"), 768),
    ("nsa", ("n_q", "n_kc", "n_vc", "n_ks", "n_vs", "n_kw", "n_vw", "n_g"), 768),
    ("ssd", ("s_z", "s_xbc", "s_dt"), 1152),
)
_ROPE = {
    "r_q": (RET_DK, RET_DK, 0), "r_k": (RET_DK, RET_DK, 1),
    "d_q": (DSA_DH, DSA_DH // ROPE_FRAC, 2), "n_q": (NSA_DH, NSA_DH // ROPE_FRAC, 2),
    "d_k": (DSA_DH, DSA_DH // ROPE_FRAC, 3), "n_kc": (NSA_DH, NSA_DH // ROPE_FRAC, 3),
    "n_ks": (NSA_DH, NSA_DH // ROPE_FRAC, 3), "n_kw": (NSA_DH, NSA_DH // ROPE_FRAC, 3),
    "i_q": (IDX_DH, IDX_DH // ROPE_FRAC, 4), "i_k": (IDX_DH, IDX_DH // ROPE_FRAC, 5),
}
N_TABLES = 6
_TABLE_HALF = (RET_DK // 2, RET_DK // 2, DSA_DH // ROPE_FRAC // 2, DSA_DH // ROPE_FRAC // 2,
               IDX_DH // ROPE_FRAC // 2, IDX_DH // ROPE_FRAC // 2)
LOG2E = math.log2(math.e)


def _build_layout():
    tile_seg = []
    group_tiles = []
    for _, segs, width in _GROUPS:
        cols = []
        for s in segs:
            cols += [s] * _SEG[s][1]
        cols += [None] * (width - len(cols))
        tile_seg += [cols[t * LANE:(t + 1) * LANE] for t in range(width // LANE)]
        group_tiles.append(width // LANE)
    roped_tiles = [t for t, segs in enumerate(tile_seg) if any(s in _ROPE for s in segs)]
    tile_table = [_ROPE[tile_seg[t][0]][2] for t in roped_tiles]
    return roped_tiles, tile_table, group_tiles, len(tile_seg) * LANE


_ROPED_TILES, _TILE_TABLE, _GROUP_TILES, N_MAIN = _build_layout()


def _dot(a, b):
    return jnp.dot(a, b, preferred_element_type=F32)


def _dot_nt(a, b):
    return lax.dot_general(a, b, (((1,), (1,)), ((), ())), preferred_element_type=F32)


def _dot_tn(a, b):
    return lax.dot_general(a, b, (((0,), (0,)), ((), ())), preferred_element_type=F32)


def _split3(a):
    hi = a.astype(BF)
    r1 = a - hi.astype(F32)
    mid = r1.astype(BF)
    lo = (r1 - mid.astype(F32)).astype(BF)
    return hi, mid, lo


def _layer_norm(v, g, b):
    mu = jnp.mean(v, -1, keepdims=True)
    d = v - mu
    var = jnp.mean(d * d, -1, keepdims=True)
    return d * lax.rsqrt(var + LN_EPS) * g + b


def _silu(v):
    return v * jax.nn.sigmoid(v)


def _params(sem):
    return pltpu.CompilerParams(dimension_semantics=sem, vmem_limit_bytes=VMEM_LIMIT)


def _ffn_kernel(x_ref, wg_ref, wu_ref, wd_ref, g_ref, b_ref, o_ref, acc_ref, *, alpha):
    j = pl.program_id(1)

    @pl.when(j == 0)
    def _():
        acc_ref[...] = jnp.zeros_like(acc_ref)

    xb = x_ref[...].astype(BF)
    gate = _dot(xb, wg_ref[...])
    up = _dot(xb, wu_ref[...])
    h = (_silu(gate) * up).astype(BF)
    acc_ref[...] += _dot(h, wd_ref[...])

    @pl.when(j == pl.num_programs(1) - 1)
    def _():
        o_ref[...] = _layer_norm(alpha * x_ref[...] + 0.5 * acc_ref[...], g_ref[...], b_ref[...])


def _ffn(x, w_gu, w_down, g, b, alpha, tm=512):
    T, D = x.shape
    F = w_down.shape[0]
    fc = F // 2 if (F // 2) % LANE == 0 else F
    nf = F // fc
    return pl.pallas_call(
        functools.partial(_ffn_kernel, alpha=alpha),
        grid=(T // tm, nf),
        in_specs=[
            pl.BlockSpec((tm, D), lambda i, j: (i, 0)),
            pl.BlockSpec((D, fc), lambda i, j: (0, j)),
            pl.BlockSpec((D, fc), lambda i, j: (0, j + nf)),
            pl.BlockSpec((fc, D), lambda i, j: (j, 0)),
            pl.BlockSpec((1, D), lambda i, j: (0, 0)),
            pl.BlockSpec((1, D), lambda i, j: (0, 0)),
        ],
        out_specs=pl.BlockSpec((tm, D), lambda i, j: (i, 0)),
        out_shape=jax.ShapeDtypeStruct((T, D), F32),
        scratch_shapes=[pltpu.VMEM((tm, D), F32)],
        compiler_params=_params(("parallel", "arbitrary")),
        name="ffn",
    )(x, w_gu, w_gu, w_down, g, b)


AUX_TILE = 5


def _inproj_kernel(x_ref, wm_ref, cos_ref, shi_ref, slo_ref, o_ret, o_dsa, o_nsa, o_ssd, o_dsa_aux, o_nsa_aux):
    xb = x_ref[...].astype(BF)
    outs = (o_ret, o_dsa, o_nsa, o_ssd)
    aux = (None, o_dsa_aux, o_nsa_aux, None)
    t0 = 0
    for o_ref, aux_ref, nt in zip(outs, aux, _GROUP_TILES):
        y = _dot(xb, wm_ref[:, t0 * LANE:(t0 + nt) * LANE])
        for t in range(nt):
            yt = y[:, t * LANE:(t + 1) * LANE]
            gt = t0 + t
            if gt in _ROPED_TILES:
                tab = _TILE_TABLE[_ROPED_TILES.index(gt)]
                half = _TABLE_HALF[tab]
                lanes = slice(tab * LANE, (tab + 1) * LANE)
                yt = (yt * cos_ref[:, lanes] + pltpu.roll(yt, LANE - half, 1) * shi_ref[:, lanes]
                      + pltpu.roll(yt, half, 1) * slo_ref[:, lanes])
            o_ref[:, t * LANE:(t + 1) * LANE] = yt.astype(o_ref.dtype)
            if aux_ref is not None and t == AUX_TILE:
                aux_ref[...] = yt
        t0 += nt


def _inproj(x, wm, tables, S, tm=256):
    T, D = x.shape
    ns = S // tm
    widths = [nt * LANE for nt in _GROUP_TILES] + [LANE, LANE]
    dtypes = [F32, BF, BF, F32, F32, F32]
    table_spec = pl.BlockSpec((tm, N_TABLES * LANE), lambda i: (i % ns, 0))
    return pl.pallas_call(
        _inproj_kernel,
        grid=(T // tm,),
        in_specs=[
            pl.BlockSpec((tm, D), lambda i: (i, 0)),
            pl.BlockSpec((D, N_MAIN), lambda i: (0, 0)),
            table_spec, table_spec, table_spec,
        ],
        out_specs=[pl.BlockSpec((tm, w), lambda i: (i, 0)) for w in widths],
        out_shape=[jax.ShapeDtypeStruct((T, w), dt) for w, dt in zip(widths, dtypes)],
        compiler_params=_params(("parallel",)),
        name="in_proj",
    )(x, wm, *tables)


def _ret_kernel(r_ref, o_ref, st_ref):
    C = RET_CHUNK

    @pl.when(pl.program_id(0) == 0)
    def _():
        st_ref[...] = jnp.zeros_like(st_ref)

    hq = RET_HEADS * RET_DK
    rel = (lax.broadcasted_iota(jnp.int32, (C, C), 0) - lax.broadcasted_iota(jnp.int32, (C, C), 1)).astype(F32)
    row = lax.broadcasted_iota(jnp.int32, (C, 1), 0).astype(F32)
    head_consts = []
    for h in range(RET_HEADS):
        lg = math.log1p(-(2.0 ** (-5.0 - h)))
        head_consts.append((jnp.where(rel >= 0, jnp.exp(jnp.maximum(rel, 0.0) * lg), 0.0),
                            jnp.exp((C - 1 - row) * lg), jnp.exp((row + 1.0) * lg), math.exp(C * lg)))
    for b in range(r_ref.shape[0]):
        r = r_ref[b]
        q, k = r[:, :hq], r[:, hq:2 * hq]
        v = r[:, 2 * hq:2 * hq + RET_HEADS * RET_DV]
        g = r[:, 2 * hq + RET_HEADS * RET_DV:]
        outs = []
        for h, (decay, zeta, xi, chunk_decay) in enumerate(head_consts):
            qh = q[:, h * RET_DK:(h + 1) * RET_DK].astype(BF)
            kh = k[:, h * RET_DK:(h + 1) * RET_DK]
            vh = v[:, h * RET_DV:(h + 1) * RET_DV].astype(BF)
            sc = _dot_nt(qh, kh.astype(BF)) * decay
            intra = _dot(sc.astype(BF), vh)
            prev = st_ref[b * RET_HEADS + h]
            cross = _dot(qh, prev.astype(BF)) * xi
            st_ref[b * RET_HEADS + h] = prev * chunk_decay + _dot_tn((kh * zeta).astype(BF), vh)
            o = intra + cross
            mu = jnp.mean(o, -1, keepdims=True)
            d = o - mu
            var = jnp.mean(d * d, -1, keepdims=True)
            outs.append(d * lax.rsqrt(var + LN_EPS))
        o_ref[b] = _silu(g) * jnp.concatenate(outs, axis=-1)


def _retention(ret, B, S):
    n = S // RET_CHUNK
    W = ret.shape[1]
    return pl.pallas_call(
        _ret_kernel,
        grid=(n,),
        in_specs=[pl.BlockSpec((B, RET_CHUNK, W), lambda c: (0, c, 0))],
        out_specs=pl.BlockSpec((B, RET_CHUNK, BRANCH_W), lambda c: (0, c, 0)),
        out_shape=jax.ShapeDtypeStruct((B, S, BRANCH_W), F32),
        scratch_shapes=[pltpu.VMEM((B * RET_HEADS, RET_DK, RET_DV), F32)],
        compiler_params=_params(("arbitrary",)),
        name="retention",
    )(ret.reshape(B, S, W)).reshape(B * S, BRANCH_W)


def _softplus(v):
    return jnp.maximum(v, 0.0) + jnp.log1p(jnp.exp(-jnp.abs(v)))


def _ssd_kernel(s_ref, dtr_ref, cw_ref, cb_ref, dtb_c_ref, a_c_ref, dtb_r_ref, a_r_ref, dsk_ref, ng_ref,
                o_ref, st_ref, xpad_ref):
    Q, H, P, N = SSM_CHUNK, SSM_HEADS, SSM_HEADDIM, SSM_STATE

    @pl.when(pl.program_id(0) == 0)
    def _():
        st_ref[...] = jnp.zeros_like(st_ref)
        xpad_ref[:, 0:CONV_PAD, :] = jnp.zeros((xpad_ref.shape[0], CONV_PAD, SSM_CONV_DIM), F32)

    ri = lax.broadcasted_iota(jnp.int32, (Q, Q), 0)
    ci = lax.broadcasted_iota(jnp.int32, (Q, Q), 1)
    tril = ri >= ci
    lo_tri = jnp.where(tril, 1.0, 0.0).astype(BF)
    up_tri = jnp.where(ci >= ri, 1.0, 0.0).astype(BF)
    gn = SSM_GROUPS * N
    gw = SSM_INNER // SSM_GROUPS
    for b in range(s_ref.shape[0]):
        z = s_ref[b, :, :SSM_INNER]
        xpad_ref[b, CONV_PAD:, :] = s_ref[b, :, SSM_INNER:SSM_INNER + SSM_CONV_DIM]
        dt_raw = s_ref[b, :, SSM_INNER + SSM_CONV_DIM:SSM_INNER + SSM_CONV_DIM + H]
        conv = cb_ref[...]
        for kk in range(SSM_CONV):
            off = CONV_PAD - (SSM_CONV - 1) + kk
            conv = conv + cw_ref[kk:kk + 1, :] * xpad_ref[b, off:off + Q, :]
        xpad_ref[b, 0:CONV_PAD, :] = xpad_ref[b, Q:Q + CONV_PAD, :]
        xc = _silu(conv)
        xs = xc[:, :SSM_INNER]
        bm = xc[:, SSM_INNER:SSM_INNER + gn]
        cm = xc[:, SSM_INNER + gn:]

        dt_c = _softplus(dt_raw + dtb_c_ref[...])
        adt_c = dt_c * a_c_ref[...]
        dt_r = _softplus(dtr_ref[b] + dtb_r_ref[...])
        adt_r = dt_r * a_r_ref[...]
        acs_c = sum(_dot(lo_tri, t) for t in _split3(adt_c))
        acs_r = sum(_dot(t, up_tri) for t in _split3(adt_r))

        outs = []
        for grp in range(SSM_GROUPS):
            bg = bm[:, grp * N:(grp + 1) * N]
            cg = cm[:, grp * N:(grp + 1) * N].astype(BF)
            cb = _dot_nt(cg, bg.astype(BF))
            for h in range(grp * (H // SSM_GROUPS), (grp + 1) * (H // SSM_GROUPS)):
                a_col = acs_c[:, h:h + 1]
                a_last = acs_c[Q - 1:Q, h:h + 1]
                lm = jnp.exp(jnp.where(tril, a_col - acs_r[h:h + 1, :], NEG_INF))
                xh = xs[:, h * P:(h + 1) * P]
                xdt = (xh * dt_c[:, h:h + 1]).astype(BF)
                y = _dot((cb * lm).astype(BF), xdt)
                prev = st_ref[b * H + h]
                y = y + _dot(cg, prev.astype(BF)) * jnp.exp(a_col)
                st_ref[b * H + h] = (prev * jnp.exp(a_last)
                                     + _dot_tn((bg * jnp.exp(a_last - a_col)).astype(BF), xdt))
                outs.append(y)
        y = jnp.concatenate(outs, axis=-1) + xs * dsk_ref[...]
        y = y * _silu(z)
        normed = []
        for grp in range(SSM_GROUPS):
            yg = y[:, grp * gw:(grp + 1) * gw]
            normed.append(yg * lax.rsqrt(jnp.mean(yg * yg, -1, keepdims=True) + LN_EPS))
        o_ref[b] = jnp.concatenate(normed, axis=-1) * ng_ref[...]


def _ssd(ssd, dt_rows, conv_w, conv_b, dt_bias, a_log, d_skip, norm_g, B, S):
    Q, H = SSM_CHUNK, SSM_HEADS
    n = S // Q
    W = ssd.shape[1]
    a = -jnp.exp(a_log.astype(F32))
    pad_r = lambda v: jnp.broadcast_to(jnp.pad(v, (0, CONV_PAD - H))[:, None], (CONV_PAD, Q)).astype(F32)
    full = lambda shape: pl.BlockSpec(shape, lambda c: (0,) * len(shape))
    return pl.pallas_call(
        _ssd_kernel,
        grid=(n,),
        in_specs=[
            pl.BlockSpec((B, Q, W), lambda c: (0, c, 0)),
            pl.BlockSpec((B, None, CONV_PAD, Q), lambda c: (0, c, 0, 0)),
            full((SSM_CONV, SSM_CONV_DIM)), full((1, SSM_CONV_DIM)),
            full((1, H)), full((1, H)), full((CONV_PAD, Q)), full((CONV_PAD, Q)),
            full((1, SSM_INNER)), full((1, SSM_INNER)),
        ],
        out_specs=pl.BlockSpec((B, Q, BRANCH_W), lambda c: (0, c, 0)),
        out_shape=jax.ShapeDtypeStruct((B, S, BRANCH_W), F32),
        scratch_shapes=[pltpu.VMEM((B * H, SSM_STATE, SSM_HEADDIM), F32),
                        pltpu.VMEM((B, Q + CONV_PAD, SSM_CONV_DIM), F32)],
        compiler_params=_params(("arbitrary",)),
        name="ssd",
    )(ssd.reshape(B, S, W), dt_rows, conv_w, conv_b[None, :], dt_bias[None, :], a[None, :], pad_r(dt_bias),
      pad_r(a), jnp.repeat(d_skip, SSM_HEADDIM)[None, :], norm_g[None, :]).reshape(B * S, BRANCH_W)


def _masked_attention(q_bf, k_bf, v_bf, mask, heads, dh):
    outs = []
    for h in range(heads):
        s = jnp.where(mask, _dot_nt(q_bf[:, h * dh:(h + 1) * dh], k_bf), NEG_INF)
        m = jnp.max(s, -1, keepdims=True)
        m = jnp.where(m > NEG_INF, m, 0.0)
        e = jnp.exp2(s - m)
        l = jnp.maximum(jnp.sum(e, -1, keepdims=True), 1e-30)
        outs.append(_dot(e.astype(BF), v_bf) / l)
    return jnp.concatenate(outs, axis=-1)


def _count(m):
    return jnp.sum(jnp.where(m, 1.0, 0.0), -1, keepdims=True)


IDX_KEY_CHUNK = 256
SEL_BIG = float(2 ** 20)
CAUSAL_SEG = 1024
BISECT_PLAIN_STEPS = 26
BISECT_MAX_STEPS = 400


def _kth_threshold(load, nvalid, vmin, vmax, c_ge0, c_gt0, k):
    kf = float(k)
    short = nvalid <= kf
    up0 = c_ge0 >= kf
    tie0 = up0 & (c_gt0 < kf) & (c_ge0 > kf) & (nvalid > kf)
    lo0 = jnp.where(short, jnp.finfo(F32).min, jnp.where(up0, 0.0, vmin))
    hi0 = jnp.where(up0, vmax, 0.0)
    done0 = jnp.where(short | tie0 | (c_ge0 == kf), 1.0, 0.0)

    def split(lo, hi, done):
        mid = jnp.where(done > 0.5, lo, lo + (hi - lo) * 0.5)
        c = _count(load() >= mid)
        up = (c >= kf) & (done < 0.5)
        return jnp.where(up, mid, lo), jnp.where(up | (done > 0.5), hi, mid), jnp.where(up & (c == kf), 1.0, done)

    def plain_cond(st):
        return (st[0] < BISECT_PLAIN_STEPS) & (jnp.min(st[3]) < 0.5)

    def plain_step(st):
        it, lo, hi, done = st
        lo, hi, done = split(lo, hi, done)
        lo, hi, done = split(lo, hi, done)
        return it + 2, lo, hi, done

    _, lo, hi, done = lax.while_loop(plain_cond, plain_step, (jnp.int32(0), lo0, hi0, done0))

    def exact_cond(st):
        return (st[0] < BISECT_MAX_STEPS) & (jnp.min(st[4]) < 0.5)

    def exact_step(st):
        it, lo, hi, tie, done = st
        sc = load()
        cand = jnp.min(jnp.where(sc >= lo, sc, jnp.inf), -1, keepdims=True)
        fin = (_count(sc > cand) < kf) & (done < 0.5)
        lo = jnp.where(fin, cand, lo)
        tie = jnp.where(fin, 1.0, tie)
        done = jnp.where(fin, 1.0, done)
        lo, hi, done = split(lo, hi, done)
        return it + 1, lo, hi, tie, done

    _, thr, _, tie, _ = lax.while_loop(
        exact_cond, exact_step, (jnp.int32(0), lo, hi, jnp.where(tie0, 1.0, 0.0), done))
    return thr, tie


def _dsa_body(q_ref, aux_ref, kv_ref, ik_ref, o_ref, sc_ref, *, n_keep, E):
    Qb, CH = DSA_QB, IDX_KEY_CHUNK
    bi = pl.program_id(1)
    qpos = bi * Qb + lax.broadcasted_iota(jnp.int32, (Qb, 1), 0)
    hq = DSA_HEADS * DSA_DH
    iw = aux_ref[:, IDX_DH:IDX_DH + IDX_HEADS] * (IDX_HEADS ** -0.5) * (IDX_DH ** -0.5)
    iq = [q_ref[:, hq + h * IDX_DH:hq + (h + 1) * IDX_DH] for h in range(IDX_HEADS)]
    hi_acc = jnp.full((Qb, CH), NEG_INF, F32)
    lo_acc = jnp.full((Qb, CH), jnp.inf, F32)
    ge0 = jnp.zeros((Qb, CH), F32)
    gt0 = jnp.zeros((Qb, CH), F32)
    for c0 in range(0, E, CH):
        ik = ik_ref[c0:c0 + CH, :IDX_DH]
        acc = jnp.maximum(_dot_nt(iq[0], ik), 0.0) * iw[:, 0:1]
        for h in range(1, IDX_HEADS):
            acc = acc + jnp.maximum(_dot_nt(iq[h], ik), 0.0) * iw[:, h:h + 1]
        causal = c0 + lax.broadcasted_iota(jnp.int32, (1, CH), 1) <= qpos
        masked = jnp.where(causal, acc, NEG_INF)
        sc_ref[:, c0:c0 + CH] = masked
        hi_acc = jnp.maximum(hi_acc, masked)
        lo_acc = jnp.minimum(lo_acc, jnp.where(causal, acc, jnp.inf))
        ge0 = ge0 + jnp.where(masked >= 0.0, 1.0, 0.0)
        gt0 = gt0 + jnp.where(masked > 0.0, 1.0, 0.0)
    load = lambda: sc_ref[:, 0:E]
    thr, tie = _kth_threshold(load, (qpos + 1).astype(F32), jnp.min(lo_acc, -1, keepdims=True),
                              jnp.max(hi_acc, -1, keepdims=True), jnp.sum(ge0, -1, keepdims=True),
                              jnp.sum(gt0, -1, keepdims=True), n_keep)

    @pl.when(jnp.max(tie) > 0.5)
    def resolve_ties():
        sc = load()
        gt = sc > thr
        eqf = jnp.where(sc == thr, 1.0, 0.0)
        need = n_keep - _count(gt)
        ch = 256 if E % 256 == 0 else LANE
        before = jnp.where(lax.broadcasted_iota(jnp.int32, (ch, ch), 0)
                           < lax.broadcasted_iota(jnp.int32, (ch, ch), 1), 1.0, 0.0).astype(BF)
        run = jnp.zeros((Qb, 1), F32)
        take = []
        for c0 in range(0, E, ch):
            eqc = eqf[:, c0:c0 + ch]
            prefix = _dot(eqc.astype(BF), before) + run
            take.append(jnp.where(prefix < need, eqc, 0.0))
            run = run + jnp.sum(eqc, -1, keepdims=True)
        sc_ref[:, 0:E] = jnp.where(gt | (jnp.concatenate(take, axis=-1) > 0.5), jnp.inf, NEG_INF)

    o_ref[...] = _masked_attention(q_ref[:, :hq], kv_ref[0:E, :DSA_DH], kv_ref[0:E, DSA_DH:2 * DSA_DH],
                                   load() >= thr, DSA_HEADS, DSA_DH)


def _for_causal_extent(S, seg, qb, body):
    bi = pl.program_id(1)
    assert S % seg == 0 and seg % qb == 0
    for e in range(seg, S + 1, seg):
        pl.when((bi >= (e - seg) // qb) & (bi < e // qb))(functools.partial(body, e))


def _dsa_kernel(q_ref, aux_ref, kv_ref, ik_ref, o_ref, sc_ref, *, n_keep, seg):
    _for_causal_extent(kv_ref.shape[0], seg, DSA_QB,
                       lambda e: _dsa_body(q_ref, aux_ref, kv_ref, ik_ref, o_ref, sc_ref, n_keep=n_keep, E=e))


def _dsa(dsa, dsa_aux, B, S):
    nb = S // DSA_QB
    n_keep = min(DSA_TOPK, S // 4)
    return pl.pallas_call(
        functools.partial(_dsa_kernel, n_keep=n_keep, seg=min(CAUSAL_SEG, S)),
        grid=(B, nb),
        in_specs=[
            pl.BlockSpec((DSA_QB, 4 * LANE), lambda b, i: (b * nb + i, 0)),
            pl.BlockSpec((DSA_QB, LANE), lambda b, i: (b * nb + i, 0)),
            pl.BlockSpec((S, LANE), lambda b, i: (b, 4)),
            pl.BlockSpec((S, LANE), lambda b, i: (b, 5)),
        ],
        out_specs=pl.BlockSpec((DSA_QB, BRANCH_W), lambda b, i: (b * nb + i, 0)),
        out_shape=jax.ShapeDtypeStruct((B * S, BRANCH_W), F32),
        scratch_shapes=[pltpu.VMEM((DSA_QB, S), F32)],
        compiler_params=_params(("parallel", "arbitrary")),
        name="dsa",
    )(dsa, dsa_aux, dsa, dsa)


def _cmp_kernel(g_ref, w1a_ref, w1b_ref, pos_ref, w1_ref, w2_ref, o_ref):
    g = g_ref[...].astype(BF)
    n = g.shape[0]
    a = _dot(g, w1a_ref[...])
    b = _dot(g, w1b_ref[...])
    posterm = _dot(pos_ref[...].astype(BF), w1_ref[...])[0:1, :]
    h = jax.nn.gelu(a + pltpu.roll(b, n - 1, 0) + posterm)
    o_ref[...] = _dot(h.astype(BF), w2_ref[...]).astype(o_ref.dtype)


def _compress(g2, cmp_w1, cmp_w2, cmp_pos):
    _, B, n, W = g2.shape
    Dh = NSA_DH
    half = W
    w1 = cmp_w1.astype(BF)
    pos8 = jnp.broadcast_to(cmp_pos.reshape(2, 1, CMP_LEN * Dh), (2, 8, CMP_LEN * Dh))
    return pl.pallas_call(
        _cmp_kernel,
        grid=(2, B),
        in_specs=[
            pl.BlockSpec((None, None, n, W), lambda i, b: (i, b, 0, 0)),
            pl.BlockSpec((None, half, Dh), lambda i, b: (i, 0, 0)),
            pl.BlockSpec((None, half, Dh), lambda i, b: (i, 1, 0)),
            pl.BlockSpec((None, 8, CMP_LEN * Dh), lambda i, b: (i, 0, 0)),
            pl.BlockSpec((None, CMP_LEN * Dh, Dh), lambda i, b: (i, 0, 0)),
            pl.BlockSpec((None, Dh, Dh), lambda i, b: (i, 0, 0)),
        ],
        out_specs=pl.BlockSpec((None, None, n, Dh), lambda i, b: (i, b, 0, 0)),
        out_shape=jax.ShapeDtypeStruct((2, B, n, Dh), BF),
        compiler_params=_params(("parallel", "parallel")),
        name="nsa_compress",
    )(g2, w1, w1, pos8, w1, cmp_w2.astype(BF))


def _nsa_kernel(q_ref, gt_ref, kvc_ref, sel_ref, win_ref, exp_ref, o_ref, osel_ref, *, n_top, seg):
    Qb, H, Dh = NSA_QB, NSA_HEADS, NSA_DH
    S = sel_ref.shape[0]
    n_cmp = kvc_ref.shape[1]
    n_blk = S // SEL_LEN
    bi = pl.program_id(1)
    qpos = bi * Qb + lax.broadcasted_iota(jnp.int32, (Qb, 1), 0)
    q = q_ref[...]

    kc = kvc_ref[0]
    vc = kvc_ref[1]
    cidx = lax.broadcasted_iota(jnp.int32, (1, n_cmp), 1)
    vis = cidx * CMP_STRIDE + (CMP_LEN - 1) <= qpos
    o_cmp = []
    psum = jnp.zeros((Qb, n_cmp), F32)
    for h in range(H):
        s = jnp.where(vis, _dot_nt(q[:, h * Dh:(h + 1) * Dh], kc), NEG_INF)
        m = jnp.max(s, -1, keepdims=True)
        m = jnp.where(m > NEG_INF, m, 0.0)
        e = jnp.exp2(s - m)
        p = e / jnp.maximum(jnp.sum(e, -1, keepdims=True), 1e-30)
        psum = psum + p
        o_cmp.append(_dot(p.astype(BF), vc))

    js = lax.broadcasted_iota(jnp.int32, (n_blk, 1), 0) * SEL_LEN
    cs = lax.broadcasted_iota(jnp.int32, (1, n_cmp), 1) * CMP_STRIDE
    ov = jnp.maximum(jnp.minimum(cs + CMP_LEN, js + SEL_LEN) - jnp.maximum(cs, js), 0).astype(F32) / CMP_LEN
    ov = ov.astype(BF)
    imp = sum(_dot_nt(ov, t) for t in _split3(psum))
    blk = lax.broadcasted_iota(jnp.int32, (n_blk, 1), 0)
    sel_shift = SEL_LEN.bit_length() - 1
    cur = jnp.right_shift(bi * Qb + lax.broadcasted_iota(jnp.int32, (1, Qb), 1), sel_shift)
    forced = (blk == 0) | (blk == cur) | (blk == cur - 1)
    imp = jnp.where(blk <= cur, jnp.where(forced, jnp.inf, imp), NEG_INF)
    rank = jnp.zeros((n_blk, Qb), F32)
    for j in range(n_blk):
        row = imp[j:j + 1, :]
        rank = rank + jnp.where((row > imp) | ((row == imp) & (blk > j)), 1.0, 0.0)
    sub = lax.broadcasted_iota(jnp.int32, (LANE - n_blk, Qb), 0)
    chosen = jnp.concatenate([jnp.where(rank < n_top, SEL_BIG, 0.0), jnp.where(sub < 2, 1.0, 0.0)], axis=0)
    chosen = chosen.T.astype(BF)
    bound = SEL_BIG - 0.5 - qpos.astype(F32)

    def selected(e):
        mask = _dot(chosen, exp_ref[:, 0:e]) > bound
        osel_ref[...] = _masked_attention(q, sel_ref[0:e, :Dh], sel_ref[0:e, Dh:2 * Dh], mask, H, Dh)

    _for_causal_extent(S, seg, Qb, selected)
    o_sel = osel_ref[...]

    wlen = WINDOW + Qb
    start = pl.multiple_of(jnp.maximum(bi * Qb - WINDOW, 0), Qb)
    kwin = win_ref[pl.ds(start, wlen), :]
    dlt = qpos - (start + lax.broadcasted_iota(jnp.int32, (1, wlen), 1))
    o_win = _masked_attention(q, kwin[:, :Dh], kwin[:, Dh:2 * Dh], (dlt >= 0) & (dlt < WINDOW), H, Dh)

    g = jax.nn.sigmoid(gt_ref[:, :3 * H])
    outs = []
    for h in range(H):
        outs.append(g[:, 3 * h:3 * h + 1] * o_cmp[h]
                    + g[:, 3 * h + 1:3 * h + 2] * o_sel[:, h * Dh:(h + 1) * Dh]
                    + g[:, 3 * h + 2:3 * h + 3] * o_win[:, h * Dh:(h + 1) * Dh])
    o_ref[...] = jnp.concatenate(outs, axis=-1)


def _nsa(nsa, nsa_aux, kvc, B, S):
    nb = S // NSA_QB
    n_cmp = kvc.shape[2]
    n_blk = S // SEL_LEN
    n_top = min(SEL_TOPN, n_blk)
    assert S >= WINDOW + NSA_QB and n_blk + 2 <= LANE and S < SEL_BIG and n_blk <= 256
    kpos = np.arange(S)
    expand = np.zeros((LANE, S), np.float32)
    expand[:n_blk] = kpos[None, :] // SEL_LEN == np.arange(n_blk)[:, None]
    expand[n_blk] = -(kpos // SEL_LEN * SEL_LEN)
    expand[n_blk + 1] = -(kpos % SEL_LEN)
    expand = jnp.asarray(expand, BF)
    return pl.pallas_call(
        functools.partial(_nsa_kernel, n_top=n_top, seg=min(CAUSAL_SEG, S)),
        grid=(B, nb),
        in_specs=[
            pl.BlockSpec((NSA_QB, 2 * LANE), lambda b, i: (b * nb + i, 0)),
            pl.BlockSpec((NSA_QB, LANE), lambda b, i: (b * nb + i, 0)),
            pl.BlockSpec((2, None, n_cmp, NSA_DH), lambda b, i: (0, b, 0, 0)),
            pl.BlockSpec((S, LANE), lambda b, i: (b, 3)),
            pl.BlockSpec((S, LANE), lambda b, i: (b, 4)),
            pl.BlockSpec((LANE, S), lambda b, i: (0, 0)),
        ],
        out_specs=pl.BlockSpec((NSA_QB, BRANCH_W), lambda b, i: (b * nb + i, 0)),
        out_shape=jax.ShapeDtypeStruct((B * S, BRANCH_W), F32),
        scratch_shapes=[pltpu.VMEM((NSA_QB, BRANCH_W), F32)],
        compiler_params=_params(("parallel", "arbitrary")),
        name="nsa",
    )(nsa, nsa_aux, kvc, nsa, nsa, expand)


def _merge_kernel(x_ref, y0, y1, y2, y3, wg_ref, wb_ref, wo_ref, g_ref, b_ref, o_ref, *, alpha):
    x = x_ref[...]
    xb = x.astype(BF)
    D = x.shape[1]
    merged = jnp.zeros_like(x)
    for n, y_ref in enumerate((y0, y1, y2, y3)):
        gate = jax.nn.sigmoid(_dot(xb, wg_ref[:, n * D:(n + 1) * D]))
        merged = merged + gate * _dot(y_ref[...].astype(BF), wb_ref[n])
    o_ref[...] = _layer_norm(alpha * x + _dot(merged.astype(BF), wo_ref[...]), g_ref[...], b_ref[...])


def _merge(x, ys, w_gate, w_branch, w_out, g, b, alpha, tm=256):
    T, D = x.shape
    full = lambda shape: pl.BlockSpec(shape, lambda i: (0,) * len(shape))
    return pl.pallas_call(
        functools.partial(_merge_kernel, alpha=alpha),
        grid=(T // tm,),
        in_specs=[pl.BlockSpec((tm, D), lambda i: (i, 0))]
        + [pl.BlockSpec((tm, BRANCH_W), lambda i: (i, 0))] * N_BRANCH
        + [full(w_gate.shape), full(w_branch.shape), full(w_out.shape), full((1, D)), full((1, D))],
        out_specs=pl.BlockSpec((tm, D), lambda i: (i, 0)),
        out_shape=jax.ShapeDtypeStruct((T, D), F32),
        compiler_params=_params(("parallel",)),
        name="merge",
    )(x, *ys, w_gate, w_branch, w_out, g, b)


def _matmul_kernel(a_ref, w_ref, o_ref):
    o_ref[...] = _dot(a_ref[...].astype(BF), w_ref[...])


def _matmul(a, w, tm):
    M, K = a.shape
    N = w.shape[1]
    return pl.pallas_call(
        _matmul_kernel,
        grid=(M // tm,),
        in_specs=[pl.BlockSpec((tm, K), lambda i: (i, 0)), pl.BlockSpec((K, N), lambda i: (0, 0))],
        out_specs=pl.BlockSpec((tm, N), lambda i: (i, 0)),
        out_shape=jax.ShapeDtypeStruct((M, N), F32),
        compiler_params=_params(("parallel",)),
        name="kv_proj",
    )(a, w)


def _xattn_kernel(x_ref, kv_ref, wq_ref, wo_ref, g_ref, b_ref, o_ref, *, alpha):
    x = x_ref[...]
    D = x.shape[1]
    dh = D // X_HEADS
    q = _dot(x.astype(BF), wq_ref[...])
    outs = []
    for h in range(X_HEADS):
        k = kv_ref[:, h * dh:(h + 1) * dh].astype(BF)
        v = kv_ref[:, D + h * dh:D + (h + 1) * dh].astype(BF)
        s = _dot_nt(q[:, h * dh:(h + 1) * dh].astype(BF), k) * (dh ** -0.5)
        e = jnp.exp(s - jnp.max(s, -1, keepdims=True))
        outs.append(_dot(e.astype(BF), v) / jnp.sum(e, -1, keepdims=True))
    att = jnp.concatenate(outs, axis=-1).astype(BF)
    o_ref[...] = _layer_norm(alpha * x + _dot(att, wo_ref[...]), g_ref[...], b_ref[...])


def _xattn(x, kv, wq, wo, g, b, alpha, S, M, tm=512):
    T, D = x.shape
    per = S // tm
    full = lambda shape: pl.BlockSpec(shape, lambda i: (0,) * len(shape))
    return pl.pallas_call(
        functools.partial(_xattn_kernel, alpha=alpha),
        grid=(T // tm,),
        in_specs=[pl.BlockSpec((tm, D), lambda i: (i, 0)),
                  pl.BlockSpec((M, 2 * D), lambda i: (i // per, 0)),
                  full(wq.shape), full(wo.shape), full((1, D)), full((1, D))],
        out_specs=pl.BlockSpec((tm, D), lambda i: (i, 0)),
        out_shape=jax.ShapeDtypeStruct((T, D), F32),
        compiler_params=_params(("parallel",)),
        name="xattn",
    )(x, kv, wq, wo, g, b)


def _rope_tables(S):
    pos = jnp.arange(S).astype(F32)

    def base(rot, theta):
        half = rot // 2
        inv = jnp.power(jnp.float32(theta), -2.0 * jnp.arange(half, dtype=F32) / rot)
        ang = pos[:, None] * inv[None, :]
        return jnp.cos(ang), jnp.sin(ang)

    def tile(hd, rot, theta, width, scale=1.0):
        c, s = base(rot, theta)
        lane = np.arange(LANE)
        jj = lane % hd
        half = rot // 2
        first = (jj < half) & (lane < width)
        second = (jj >= half) & (jj < rot) & (lane < width)
        idx = jj % half
        ct = jnp.where((first | second)[None, :], c[:, idx], 1.0) * scale
        s_hi = jnp.where(first[None, :], -s[:, idx], 0.0) * scale
        s_lo = jnp.where(second[None, :], s[:, idx], 0.0) * scale
        return ct, s_hi, s_lo

    q_scale = DSA_DH ** -0.5 * LOG2E
    tabs = [
        tile(RET_DK, RET_DK, RET_THETA, LANE),
        tile(RET_DK, RET_DK, RET_THETA, LANE, RET_DK ** -0.5),
        tile(DSA_DH, DSA_DH // ROPE_FRAC, ROPE_THETA, LANE, q_scale),
        tile(DSA_DH, DSA_DH // ROPE_FRAC, ROPE_THETA, DSA_DH),
        tile(IDX_DH, IDX_DH // ROPE_FRAC, ROPE_THETA, LANE),
        tile(IDX_DH, IDX_DH // ROPE_FRAC, ROPE_THETA, IDX_DH),
    ]
    return tuple(jnp.concatenate([t[i] for t in tabs], 1) for i in range(3))


def _pack_w_in(w):
    D = w.shape[0]
    wb = w.astype(BF)
    main = []
    for _, segs, width in _GROUPS:
        used = 0
        for s in segs:
            start, n = _SEG[s]
            main.append(wb[:, start:start + n])
            used += n
        main.append(jnp.zeros((D, width - used), BF))
    return jnp.concatenate(main, 1), wb[:, GATE_START:]


def kernel(x, mem, ln_g, ln_b, ffn1_w_gu, ffn1_w_down, w_in, cmp_w1, cmp_w2, cmp_pos, conv_w, conv_b,
           dt_bias, a_log, d_skip, ssm_norm_g, w_branch, w_out, xattn_wq, xattn_wkv, xattn_wo,
           ffn2_w_gu, ffn2_w_down):
    B, S, D = x.shape
    M = mem.shape[1]
    depth = ln_g.shape[0]
    alpha = (2 * depth) ** 0.25
    T = B * S
    tables = _rope_tables(S)
    h = x.reshape(T, D)
    mem2 = mem.reshape(B * M, D)
    for l in range(depth):
        lg = lambda i: ln_g[l, i][None, :]
        lb = lambda i: ln_b[l, i][None, :]
        h = _ffn(h, ffn1_w_gu[l].astype(BF), ffn1_w_down[l].astype(BF), lg(0), lb(0), alpha)

        wm, w_gate = _pack_w_in(w_in[l])
        ret, dsa, nsa, ssd, dsa_aux, nsa_aux = _inproj(h, wm, tables, S)
        y_ret = _retention(ret, B, S)
        y_dsa = _dsa(dsa, dsa_aux, B, S)
        g2 = jnp.stack([nsa[:, 2 * LANE:2 * LANE + NSA_DH], nsa[:, 2 * LANE + NSA_DH:3 * LANE]])
        kvc = _compress(g2.reshape(2, B, S // CMP_STRIDE, CMP_STRIDE * NSA_DH), cmp_w1[l], cmp_w2[l], cmp_pos[l])
        y_nsa = _nsa(nsa, nsa_aux, kvc, B, S)
        dt0 = SSM_INNER + SSM_CONV_DIM
        dt_rows = jnp.pad(ssd[:, dt0:dt0 + SSM_HEADS].reshape(B, S // SSM_CHUNK, SSM_CHUNK, SSM_HEADS)
                          .transpose(0, 1, 3, 2), ((0, 0), (0, 0), (0, CONV_PAD - SSM_HEADS), (0, 0)))
        y_ssd = _ssd(ssd, dt_rows, conv_w[l], conv_b[l], dt_bias[l], a_log[l], d_skip[l], ssm_norm_g[l], B, S)
        h = _merge(h, (y_ret, y_dsa, y_nsa, y_ssd), w_gate, w_branch[l].astype(BF), w_out[l].astype(BF),
                   lg(1), lb(1), alpha)

        kv = _matmul(mem2, xattn_wkv[l].astype(BF), tm=min(256, B * M))
        h = _xattn(h, kv, xattn_wq[l].astype(BF), xattn_wo[l].astype(BF), lg(2), lb(2), alpha, S, M)
        h = _ffn(h, ffn2_w_gu[l].astype(BF), ffn2_w_down[l].astype(BF), lg(3), lb(3), alpha)
    return h.reshape(B, S, D)
```

```python
import functools
import math

import numpy as np
import jax
import jax.numpy as jnp
from jax import lax
from jax.experimental import pallas as pl
from jax.experimental.pallas import tpu as pltpu

F32 = jnp.float32
BF = jnp.bfloat16
NEG_INF = float("-inf")

DSA_QB = 128
NSA_QB = 256
ROPE_THETA = 500000.0
ROPE_FRAC = 4
LN_EPS = 1e-5
RET_HEADS, RET_DK, RET_DV, RET_CHUNK, RET_THETA = 4, 32, 64, 128, 10000.0
DSA_HEADS, DSA_DH, IDX_HEADS, IDX_DH, DSA_TOPK = 4, 64, 8, 32, 256
NSA_HEADS, NSA_DH, CMP_LEN, CMP_STRIDE, SEL_LEN, SEL_TOPN, WINDOW = 4, 64, 32, 16, 64, 16, 512
SSM_HEADS, SSM_HEADDIM, SSM_GROUPS, SSM_STATE, SSM_CONV, SSM_CHUNK = 4, 64, 2, 128, 4, 128
SSM_INNER = SSM_HEADS * SSM_HEADDIM
SSM_CONV_DIM = SSM_INNER + 2 * SSM_GROUPS * SSM_STATE
N_BRANCH = 4
BRANCH_W = 256
X_HEADS = 4

LANE = 128
CONV_PAD = 8
VMEM_LIMIT = 56 * 1024 * 1024

_SEG_NAMES = ("r_q", "r_k", "r_v", "r_g", "d_q", "d_k", "d_v", "i_q", "i_k", "i_w",
              "n_q", "n_kc", "n_vc", "n_ks", "n_vs", "n_kw", "n_vw", "n_g", "s_z", "s_xbc", "s_dt")
_SEG_WIDTHS = (RET_HEADS * RET_DK, RET_HEADS * RET_DK, RET_HEADS * RET_DV, RET_HEADS * RET_DV,
               DSA_HEADS * DSA_DH, DSA_DH, DSA_DH, IDX_HEADS * IDX_DH, IDX_DH, IDX_HEADS,
               NSA_HEADS * NSA_DH, NSA_DH, NSA_DH, NSA_DH, NSA_DH, NSA_DH, NSA_DH, NSA_HEADS * 3,
               SSM_INNER, SSM_CONV_DIM, SSM_HEADS)
_SEG = {}
_o = 0
for _n, _w in zip(_SEG_NAMES, _SEG_WIDTHS):
    _SEG[_n] = (_o, _w)
    _o += _w
GATE_START = _o

_GROUPS = (
    ("ret", ("r_q", "r_k", "r_v", "r_g"), 768),
    ("dsa", ("d_q", "i_q", "d_k", "d_v", "i_k", "i_w"), 768),
    ("nsa", ("n_q", "n_kc", "n_vc", "n_ks", "n_vs", "n_kw", "n_vw", "n_g"), 768),
    ("ssd", ("s_z", "s_xbc", "s_dt"), 1152),
)
_ROPE = {
    "r_q": (RET_DK, RET_DK, 0), "r_k": (RET_DK, RET_DK, 1),
    "d_q": (DSA_DH, DSA_DH // ROPE_FRAC, 2), "n_q": (NSA_DH, NSA_DH // ROPE_FRAC, 2),
    "d_k": (DSA_DH, DSA_DH // ROPE_FRAC, 3), "n_kc": (NSA_DH, NSA_DH // ROPE_FRAC, 3),
    "n_ks": (NSA_DH, NSA_DH // ROPE_FRAC, 3), "n_kw": (NSA_DH, NSA_DH // ROPE_FRAC, 3),
    "i_q": (IDX_DH, IDX_DH // ROPE_FRAC, 4), "i_k": (IDX_DH, IDX_DH // ROPE_FRAC, 5),
}
N_TABLES = 6
_TABLE_HALF = (RET_DK // 2, RET_DK // 2, DSA_DH // ROPE_FRAC // 2, DSA_DH // ROPE_FRAC // 2,
               IDX_DH // ROPE_FRAC // 2, IDX_DH // ROPE_FRAC // 2)
LOG2E = math.log2(math.e)


def _build_layout():
    tile_seg = []
    group_tiles = []
    for _, segs, width in _GROUPS:
        cols = []
        for s in segs:
            cols += [s] * _SEG[s][1]
        cols += [None] * (width - len(cols))
        tile_seg += [cols[t * LANE:(t + 1) * LANE] for t in range(width // LANE)]
        group_tiles.append(width // LANE)
    roped_tiles = [t for t, segs in enumerate(tile_seg) if any(s in _ROPE for s in segs)]
    tile_table = [_ROPE[tile_seg[t][0]][2] for t in roped_tiles]
    return roped_tiles, tile_table, group_tiles, len(tile_seg) * LANE


_ROPED_TILES, _TILE_TABLE, _GROUP_TILES, N_MAIN = _build_layout()


def _dot(a, b):
    return jnp.dot(a, b, preferred_element_type=F32)


def _dot_nt(a, b):
    return lax.dot_general(a, b, (((1,), (1,)), ((), ())), preferred_element_type=F32)


def _dot_tn(a, b):
    return lax.dot_general(a, b, (((0,), (0,)), ((), ())), preferred_element_type=F32)


def _split3(a):
    hi = a.astype(BF)
    r1 = a - hi.astype(F32)
    mid = r1.astype(BF)
    lo = (r1 - mid.astype(F32)).astype(BF)
    return hi, mid, lo


def _layer_norm(v, g, b):
    mu = jnp.mean(v, -1, keepdims=True)
    d = v - mu
    var = jnp.mean(d * d, -1, keepdims=True)
    return d * lax.rsqrt(var + LN_EPS) * g + b


def _silu(v):
    return v * jax.nn.sigmoid(v)


def _params(sem):
    return pltpu.CompilerParams(dimension_semantics=sem, vmem_limit_bytes=VMEM_LIMIT)


def _ffn_kernel(x_ref, wg_ref, wu_ref, wd_ref, g_ref, b_ref, o_ref, acc_ref, *, alpha):
    j = pl.program_id(1)

    @pl.when(j == 0)
    def _():
        acc_ref[...] = jnp.zeros_like(acc_ref)

    xb = x_ref[...].astype(BF)
    gate = _dot(xb, wg_ref[...])
    up = _dot(xb, wu_ref[...])
    h = (_silu(gate) * up).astype(BF)
    acc_ref[...] += _dot(h, wd_ref[...])

    @pl.when(j == pl.num_programs(1) - 1)
    def _():
        o_ref[...] = _layer_norm(alpha * x_ref[...] + 0.5 * acc_ref[...], g_ref[...], b_ref[...])


def _ffn(x, w_gu, w_down, g, b, alpha, tm=256):
    T, D = x.shape
    F = w_down.shape[0]
    fc = F
    nf = F // fc
    once = pl.Buffered(1)
    return pl.pallas_call(
        functools.partial(_ffn_kernel, alpha=alpha),
        grid=(T // tm, nf),
        in_specs=[
            pl.BlockSpec((tm, D), lambda i, j: (i, 0)),
            pl.BlockSpec((D, fc), lambda i, j: (0, j), pipeline_mode=once),
            pl.BlockSpec((D, fc), lambda i, j: (0, j + nf), pipeline_mode=once),
            pl.BlockSpec((fc, D), lambda i, j: (j, 0), pipeline_mode=once),
            pl.BlockSpec((1, D), lambda i, j: (0, 0)),
            pl.BlockSpec((1, D), lambda i, j: (0, 0)),
        ],
        out_specs=pl.BlockSpec((tm, D), lambda i, j: (i, 0)),
        out_shape=jax.ShapeDtypeStruct((T, D), F32),
        scratch_shapes=[pltpu.VMEM((tm, D), F32)],
        compiler_params=_params(("parallel", "arbitrary")),
        name="ffn",
    )(x, w_gu, w_gu, w_down, g, b)


AUX_TILE = 5


def _inproj_kernel(x_ref, wm_ref, cos_ref, shi_ref, slo_ref, o_ret, o_dsa, o_nsa, o_ssd, o_dsa_aux, o_nsa_aux):
    xb = x_ref[...].astype(BF)
    outs = (o_ret, o_dsa, o_nsa, o_ssd)
    aux = (None, o_dsa_aux, o_nsa_aux, None)
    t0 = 0
    for o_ref, aux_ref, nt in zip(outs, aux, _GROUP_TILES):
        y = _dot(xb, wm_ref[:, t0 * LANE:(t0 + nt) * LANE])
        for t in range(nt):
            yt = y[:, t * LANE:(t + 1) * LANE]
            gt = t0 + t
            if gt in _ROPED_TILES:
                tab = _TILE_TABLE[_ROPED_TILES.index(gt)]
                half = _TABLE_HALF[tab]
                lanes = slice(tab * LANE, (tab + 1) * LANE)
                yt = (yt * cos_ref[:, lanes] + pltpu.roll(yt, LANE - half, 1) * shi_ref[:, lanes]
                      + pltpu.roll(yt, half, 1) * slo_ref[:, lanes])
            o_ref[:, t * LANE:(t + 1) * LANE] = yt.astype(o_ref.dtype)
            if aux_ref is not None and t == AUX_TILE:
                aux_ref[...] = yt
        t0 += nt


def _inproj(x, wm, tables, S, tm=256):
    T, D = x.shape
    ns = S // tm
    widths = [nt * LANE for nt in _GROUP_TILES] + [LANE, LANE]
    dtypes = [F32, BF, BF, F32, F32, F32]
    table_spec = pl.BlockSpec((tm, N_TABLES * LANE), lambda i: (i % ns, 0))
    return pl.pallas_call(
        _inproj_kernel,
        grid=(T // tm,),
        in_specs=[
            pl.BlockSpec((tm, D), lambda i: (i, 0)),
            pl.BlockSpec((D, N_MAIN), lambda i: (0, 0), pipeline_mode=pl.Buffered(1)),
            table_spec, table_spec, table_spec,
        ],
        out_specs=[pl.BlockSpec((tm, w), lambda i: (i, 0)) for w in widths],
        out_shape=[jax.ShapeDtypeStruct((T, w), dt) for w, dt in zip(widths, dtypes)],
        compiler_params=_params(("parallel",)),
        name="in_proj",
    )(x, wm, *tables)


def _ret_kernel(r_ref, o_ref, st_ref):
    C = RET_CHUNK

    @pl.when(pl.program_id(0) == 0)
    def _():
        st_ref[...] = jnp.zeros_like(st_ref)

    hq = RET_HEADS * RET_DK
    rel = (lax.broadcasted_iota(jnp.int32, (C, C), 0) - lax.broadcasted_iota(jnp.int32, (C, C), 1)).astype(F32)
    row = lax.broadcasted_iota(jnp.int32, (C, 1), 0).astype(F32)
    head_consts = []
    for h in range(RET_HEADS):
        lg = math.log1p(-(2.0 ** (-5.0 - h)))
        head_consts.append((jnp.where(rel >= 0, jnp.exp(jnp.maximum(rel, 0.0) * lg), 0.0),
                            jnp.exp((C - 1 - row) * lg), jnp.exp((row + 1.0) * lg), math.exp(C * lg)))
    for b in range(r_ref.shape[0]):
        r = r_ref[b]
        q, k = r[:, :hq], r[:, hq:2 * hq]
        v = r[:, 2 * hq:2 * hq + RET_HEADS * RET_DV]
        g = r[:, 2 * hq + RET_HEADS * RET_DV:]
        outs = []
        for h, (decay, zeta, xi, chunk_decay) in enumerate(head_consts):
            qh = q[:, h * RET_DK:(h + 1) * RET_DK].astype(BF)
            kh = k[:, h * RET_DK:(h + 1) * RET_DK]
            vh = v[:, h * RET_DV:(h + 1) * RET_DV].astype(BF)
            sc = _dot_nt(qh, kh.astype(BF)) * decay
            intra = _dot(sc.astype(BF), vh)
            prev = st_ref[b * RET_HEADS + h]
            cross = _dot(qh, prev.astype(BF)) * xi
            st_ref[b * RET_HEADS + h] = prev * chunk_decay + _dot_tn((kh * zeta).astype(BF), vh)
            o = intra + cross
            mu = jnp.mean(o, -1, keepdims=True)
            d = o - mu
            var = jnp.mean(d * d, -1, keepdims=True)
            outs.append(d * lax.rsqrt(var + LN_EPS))
        o_ref[b] = _silu(g) * jnp.concatenate(outs, axis=-1)


def _retention(ret, B, S):
    n = S // RET_CHUNK
    W = ret.shape[1]
    return pl.pallas_call(
        _ret_kernel,
        grid=(n,),
        in_specs=[pl.BlockSpec((B, RET_CHUNK, W), lambda c: (0, c, 0))],
        out_specs=pl.BlockSpec((B, RET_CHUNK, BRANCH_W), lambda c: (0, c, 0)),
        out_shape=jax.ShapeDtypeStruct((B, S, BRANCH_W), F32),
        scratch_shapes=[pltpu.VMEM((B * RET_HEADS, RET_DK, RET_DV), F32)],
        compiler_params=_params(("arbitrary",)),
        name="retention",
    )(ret.reshape(B, S, W)).reshape(B * S, BRANCH_W)


def _softplus(v):
    return jnp.maximum(v, 0.0) + jnp.log1p(jnp.exp(-jnp.abs(v)))


def _ssd_kernel(s_ref, dtr_ref, cw_ref, cb_ref, dtb_c_ref, a_c_ref, dtb_r_ref, a_r_ref, dsk_ref, ng_ref,
                o_ref, st_ref, xpad_ref):
    Q, H, P, N = SSM_CHUNK, SSM_HEADS, SSM_HEADDIM, SSM_STATE

    @pl.when(pl.program_id(0) == 0)
    def _():
        st_ref[...] = jnp.zeros_like(st_ref)
        xpad_ref[:, 0:CONV_PAD, :] = jnp.zeros((xpad_ref.shape[0], CONV_PAD, SSM_CONV_DIM), F32)

    ri = lax.broadcasted_iota(jnp.int32, (Q, Q), 0)
    ci = lax.broadcasted_iota(jnp.int32, (Q, Q), 1)
    tril = ri >= ci
    lo_tri = jnp.where(tril, 1.0, 0.0).astype(BF)
    up_tri = jnp.where(ci >= ri, 1.0, 0.0).astype(BF)
    gn = SSM_GROUPS * N
    gw = SSM_INNER // SSM_GROUPS
    for b in range(s_ref.shape[0]):
        z = s_ref[b, :, :SSM_INNER]
        xpad_ref[b, CONV_PAD:, :] = s_ref[b, :, SSM_INNER:SSM_INNER + SSM_CONV_DIM]
        dt_raw = s_ref[b, :, SSM_INNER + SSM_CONV_DIM:SSM_INNER + SSM_CONV_DIM + H]
        conv = cb_ref[...]
        for kk in range(SSM_CONV):
            off = CONV_PAD - (SSM_CONV - 1) + kk
            conv = conv + cw_ref[kk:kk + 1, :] * xpad_ref[b, off:off + Q, :]
        xpad_ref[b, 0:CONV_PAD, :] = xpad_ref[b, Q:Q + CONV_PAD, :]
        xc = _silu(conv)
        xs = xc[:, :SSM_INNER]
        bm = xc[:, SSM_INNER:SSM_INNER + gn]
        cm = xc[:, SSM_INNER + gn:]

        dt_c = _softplus(dt_raw + dtb_c_ref[...])
        adt_c = dt_c * a_c_ref[...]
        dt_r = _softplus(dtr_ref[b] + dtb_r_ref[...])
        adt_r = dt_r * a_r_ref[...]
        acs_c = sum(_dot(lo_tri, t) for t in _split3(adt_c))
        acs_r = sum(_dot(t, up_tri) for t in _split3(adt_r))

        outs = []
        for grp in range(SSM_GROUPS):
            bg = bm[:, grp * N:(grp + 1) * N]
            cg = cm[:, grp * N:(grp + 1) * N].astype(BF)
            cb = _dot_nt(cg, bg.astype(BF))
            for h in range(grp * (H // SSM_GROUPS), (grp + 1) * (H // SSM_GROUPS)):
                a_col = acs_c[:, h:h + 1]
                a_last = acs_c[Q - 1:Q, h:h + 1]
                lm = jnp.exp(jnp.where(tril, a_col - acs_r[h:h + 1, :], NEG_INF))
                xh = xs[:, h * P:(h + 1) * P]
                xdt = (xh * dt_c[:, h:h + 1]).astype(BF)
                y = _dot((cb * lm).astype(BF), xdt)
                prev = st_ref[b * H + h]
                y = y + _dot(cg, prev.astype(BF)) * jnp.exp(a_col)
                st_ref[b * H + h] = (prev * jnp.exp(a_last)
                                     + _dot_tn((bg * jnp.exp(a_last - a_col)).astype(BF), xdt))
                outs.append(y)
        y = jnp.concatenate(outs, axis=-1) + xs * dsk_ref[...]
        y = y * _silu(z)
        normed = []
        for grp in range(SSM_GROUPS):
            yg = y[:, grp * gw:(grp + 1) * gw]
            normed.append(yg * lax.rsqrt(jnp.mean(yg * yg, -1, keepdims=True) + LN_EPS))
        o_ref[b] = jnp.concatenate(normed, axis=-1) * ng_ref[...]


def _ssd(ssd, dt_rows, conv_w, conv_b, dt_bias, a_log, d_skip, norm_g, B, S):
    Q, H = SSM_CHUNK, SSM_HEADS
    n = S // Q
    W = ssd.shape[1]
    a = -jnp.exp(a_log.astype(F32))
    pad_r = lambda v: jnp.broadcast_to(jnp.pad(v, (0, CONV_PAD - H))[:, None], (CONV_PAD, Q)).astype(F32)
    full = lambda shape: pl.BlockSpec(shape, lambda c: (0,) * len(shape))
    return pl.pallas_call(
        _ssd_kernel,
        grid=(n,),
        in_specs=[
            pl.BlockSpec((B, Q, W), lambda c: (0, c, 0)),
            pl.BlockSpec((B, None, CONV_PAD, Q), lambda c: (0, c, 0, 0)),
            full((SSM_CONV, SSM_CONV_DIM)), full((1, SSM_CONV_DIM)),
            full((1, H)), full((1, H)), full((CONV_PAD, Q)), full((CONV_PAD, Q)),
            full((1, SSM_INNER)), full((1, SSM_INNER)),
        ],
        out_specs=pl.BlockSpec((B, Q, BRANCH_W), lambda c: (0, c, 0)),
        out_shape=jax.ShapeDtypeStruct((B, S, BRANCH_W), F32),
        scratch_shapes=[pltpu.VMEM((B * H, SSM_STATE, SSM_HEADDIM), F32),
                        pltpu.VMEM((B, Q + CONV_PAD, SSM_CONV_DIM), F32)],
        compiler_params=_params(("arbitrary",)),
        name="ssd",
    )(ssd.reshape(B, S, W), dt_rows, conv_w, conv_b[None, :], dt_bias[None, :], a[None, :], pad_r(dt_bias),
      pad_r(a), jnp.repeat(d_skip, SSM_HEADDIM)[None, :], norm_g[None, :]).reshape(B * S, BRANCH_W)


def _masked_attention(q_bf, kv_bf, mask, heads, dh):
    k_bf = kv_bf[:, :dh]
    ones_v = jnp.where(lax.broadcasted_iota(jnp.int32, kv_bf.shape, 1) < dh, jnp.ones((), kv_bf.dtype), kv_bf)
    outs = []
    for h in range(heads):
        s = jnp.where(mask, _dot_nt(q_bf[:, h * dh:(h + 1) * dh], k_bf), NEG_INF)
        m = jnp.max(s, -1, keepdims=True)
        m = jnp.where(m > NEG_INF, m, 0.0)
        lv = _dot(jnp.exp2((s - m).astype(BF)), ones_v)
        outs.append(lv[:, dh:] / jnp.maximum(lv[:, 0:1], 1e-30))
    return jnp.concatenate(outs, axis=-1)


def _count(m):
    return jnp.sum(jnp.where(m, 1.0, 0.0), -1, keepdims=True)


IDX_KEY_CHUNK = 256
SEL_BIG = float(2 ** 20)
CAUSAL_SEG = 1024
BISECT_PLAIN_STEPS = 26
BISECT_MAX_STEPS = 400


def _kth_threshold(load, nvalid, vmin, vmax, c_ge0, c_gt0, k):
    kf = float(k)
    short = nvalid <= kf
    up0 = c_ge0 >= kf
    tie0 = up0 & (c_gt0 < kf) & (c_ge0 > kf) & (nvalid > kf)
    lo0 = jnp.where(short, jnp.finfo(F32).min, jnp.where(up0, 0.0, vmin))
    hi0 = jnp.where(up0, vmax, 0.0)
    done0 = jnp.where(short | tie0 | (c_ge0 == kf), 1.0, 0.0)

    def split(lo, hi, done):
        mid = jnp.where(done > 0.5, lo, lo + (hi - lo) * 0.5)
        c = _count(load() >= mid)
        up = (c >= kf) & (done < 0.5)
        return jnp.where(up, mid, lo), jnp.where(up | (done > 0.5), hi, mid), jnp.where(up & (c == kf), 1.0, done)

    def plain_cond(st):
        return (st[0] < BISECT_PLAIN_STEPS) & (jnp.min(st[3]) < 0.5)

    def plain_step(st):
        it, lo, hi, done = st
        lo, hi, done = split(lo, hi, done)
        lo, hi, done = split(lo, hi, done)
        return it + 2, lo, hi, done

    _, lo, hi, done = lax.while_loop(plain_cond, plain_step, (jnp.int32(0), lo0, hi0, done0))

    def exact_cond(st):
        return (st[0] < BISECT_MAX_STEPS) & (jnp.min(st[4]) < 0.5)

    def exact_step(st):
        it, lo, hi, tie, done = st
        sc = load()
        cand = jnp.min(jnp.where(sc >= lo, sc, jnp.inf), -1, keepdims=True)
        fin = (_count(sc > cand) < kf) & (done < 0.5)
        lo = jnp.where(fin, cand, lo)
        tie = jnp.where(fin, 1.0, tie)
        done = jnp.where(fin, 1.0, done)
        lo, hi, done = split(lo, hi, done)
        return it + 1, lo, hi, tie, done

    _, thr, _, tie, _ = lax.while_loop(
        exact_cond, exact_step, (jnp.int32(0), lo, hi, jnp.where(tie0, 1.0, 0.0), done))
    return thr, tie


def _dsa_body(q_ref, aux_ref, kv_ref, ik_ref, o_ref, sc_ref, *, n_keep, E):
    Qb, CH = DSA_QB, IDX_KEY_CHUNK
    bi = pl.program_id(1)
    qpos = bi * Qb + lax.broadcasted_iota(jnp.int32, (Qb, 1), 0)
    hq = DSA_HEADS * DSA_DH
    iw = aux_ref[:, IDX_DH:IDX_DH + IDX_HEADS] * (IDX_HEADS ** -0.5) * (IDX_DH ** -0.5)
    iq = [q_ref[:, hq + h * IDX_DH:hq + (h + 1) * IDX_DH] for h in range(IDX_HEADS)]
    fold = lambda f, a: functools.reduce(f, [a[:, j:j + LANE] for j in range(0, CH, LANE)])
    hi_acc = jnp.full((Qb, LANE), NEG_INF, F32)
    lo_acc = jnp.full((Qb, LANE), jnp.inf, F32)
    for c0 in range(0, E, CH):
        ik = ik_ref[c0:c0 + CH, :IDX_DH]
        acc = jnp.maximum(_dot_nt(iq[0], ik), 0.0) * iw[:, 0:1]
        for h in range(1, IDX_HEADS):
            acc = acc + jnp.maximum(_dot_nt(iq[h], ik), 0.0) * iw[:, h:h + 1]
        causal = c0 + lax.broadcasted_iota(jnp.int32, (1, CH), 1) <= qpos
        masked = jnp.where(causal, acc, NEG_INF)
        sc_ref[:, c0:c0 + CH] = masked
        hi_acc = jnp.maximum(hi_acc, fold(jnp.maximum, masked))
        lo_acc = jnp.minimum(lo_acc, fold(jnp.minimum, jnp.where(causal, acc, jnp.inf)))
    load = lambda: sc_ref[:, 0:E]
    thr, tie = _kth_threshold(load, (qpos + 1).astype(F32), jnp.min(lo_acc, -1, keepdims=True),
                              jnp.max(hi_acc, -1, keepdims=True), _count(load() >= 0.0), _count(load() > 0.0),
                              n_keep)

    @pl.when(jnp.max(tie) > 0.5)
    def resolve_ties():
        sc = load()
        gt = sc > thr
        eqf = jnp.where(sc == thr, 1.0, 0.0)
        need = n_keep - _count(gt)
        ch = 256 if E % 256 == 0 else LANE
        before = jnp.where(lax.broadcasted_iota(jnp.int32, (ch, ch), 0)
                           < lax.broadcasted_iota(jnp.int32, (ch, ch), 1), 1.0, 0.0).astype(BF)
        run = jnp.zeros((Qb, 1), F32)
        take = []
        for c0 in range(0, E, ch):
            eqc = eqf[:, c0:c0 + ch]
            prefix = _dot(eqc.astype(BF), before) + run
            take.append(jnp.where(prefix < need, eqc, 0.0))
            run = run + jnp.sum(eqc, -1, keepdims=True)
        sc_ref[:, 0:E] = jnp.where(gt | (jnp.concatenate(take, axis=-1) > 0.5), jnp.inf, NEG_INF)

    o_ref[...] = _masked_attention(q_ref[:, :hq], kv_ref[0:E, :], load() >= thr, DSA_HEADS, DSA_DH)


def _for_causal_extent(S, seg, qb, body):
    bi = pl.program_id(1)
    assert S % seg == 0 and seg % qb == 0
    for e in range(seg, S + 1, seg):
        pl.when((bi >= (e - seg) // qb) & (bi < e // qb))(functools.partial(body, e))


def _dsa_kernel(q_ref, aux_ref, kv_ref, ik_ref, o_ref, sc_ref, *, n_keep, seg):
    _for_causal_extent(kv_ref.shape[0], seg, DSA_QB,
                       lambda e: _dsa_body(q_ref, aux_ref, kv_ref, ik_ref, o_ref, sc_ref, n_keep=n_keep, E=e))


def _dsa(dsa, dsa_aux, B, S):
    nb = S // DSA_QB
    n_keep = min(DSA_TOPK, S // 4)
    return pl.pallas_call(
        functools.partial(_dsa_kernel, n_keep=n_keep, seg=min(CAUSAL_SEG, S)),
        grid=(B, nb),
        in_specs=[
            pl.BlockSpec((DSA_QB, 4 * LANE), lambda b, i: (b * nb + i, 0)),
            pl.BlockSpec((DSA_QB, LANE), lambda b, i: (b * nb + i, 0)),
            pl.BlockSpec((S, LANE), lambda b, i: (b, 4)),
            pl.BlockSpec((S, LANE), lambda b, i: (b, 5)),
        ],
        out_specs=pl.BlockSpec((DSA_QB, BRANCH_W), lambda b, i: (b * nb + i, 0)),
        out_shape=jax.ShapeDtypeStruct((B * S, BRANCH_W), F32),
        scratch_shapes=[pltpu.VMEM((DSA_QB, S), F32)],
        compiler_params=_params(("parallel", "arbitrary")),
        name="dsa",
    )(dsa, dsa_aux, dsa, dsa)


def _cmp_kernel(g_ref, w1a_ref, w1b_ref, pos_ref, w1_ref, w2_ref, o_ref):
    g = g_ref[...].astype(BF)
    n = g.shape[0]
    a = _dot(g, w1a_ref[...])
    b = _dot(g, w1b_ref[...])
    posterm = _dot(pos_ref[...].astype(BF), w1_ref[...])[0:1, :]
    h = jax.nn.gelu(a + pltpu.roll(b, n - 1, 0) + posterm)
    o_ref[...] = _dot(h.astype(BF), w2_ref[...]).astype(o_ref.dtype)


def _compress(g2, cmp_w1, cmp_w2, cmp_pos):
    _, B, n, W = g2.shape
    Dh = NSA_DH
    half = W
    w1 = cmp_w1.astype(BF)
    pos8 = jnp.broadcast_to(cmp_pos.reshape(2, 1, CMP_LEN * Dh), (2, 8, CMP_LEN * Dh))
    return pl.pallas_call(
        _cmp_kernel,
        grid=(2, B),
        in_specs=[
            pl.BlockSpec((None, None, n, W), lambda i, b: (i, b, 0, 0)),
            pl.BlockSpec((None, half, Dh), lambda i, b: (i, 0, 0)),
            pl.BlockSpec((None, half, Dh), lambda i, b: (i, 1, 0)),
            pl.BlockSpec((None, 8, CMP_LEN * Dh), lambda i, b: (i, 0, 0)),
            pl.BlockSpec((None, CMP_LEN * Dh, Dh), lambda i, b: (i, 0, 0)),
            pl.BlockSpec((None, Dh, Dh), lambda i, b: (i, 0, 0)),
        ],
        out_specs=pl.BlockSpec((None, None, n, Dh), lambda i, b: (i, b, 0, 0)),
        out_shape=jax.ShapeDtypeStruct((2, B, n, Dh), BF),
        compiler_params=_params(("parallel", "parallel")),
        name="nsa_compress",
    )(g2, w1, w1, pos8, w1, cmp_w2.astype(BF))


def _nsa_kernel(q_ref, gt_ref, kvc_ref, sel_ref, win_ref, exp_ref, o_ref, osel_ref, *, n_top, seg):
    Qb, H, Dh = NSA_QB, NSA_HEADS, NSA_DH
    S = sel_ref.shape[0]
    n_cmp = kvc_ref.shape[1]
    n_blk = S // SEL_LEN
    bi = pl.program_id(1)
    qpos = bi * Qb + lax.broadcasted_iota(jnp.int32, (Qb, 1), 0)
    q = q_ref[...]

    kc = kvc_ref[0]
    vc = kvc_ref[1]
    cidx = lax.broadcasted_iota(jnp.int32, (1, n_cmp), 1)
    vis = cidx * CMP_STRIDE + (CMP_LEN - 1) <= qpos
    o_cmp = []
    psum = jnp.zeros((Qb, n_cmp), F32)
    for h in range(H):
        s = jnp.where(vis, _dot_nt(q[:, h * Dh:(h + 1) * Dh], kc), NEG_INF)
        m = jnp.max(s, -1, keepdims=True)
        m = jnp.where(m > NEG_INF, m, 0.0)
        e = jnp.exp2(s - m)
        p = e / jnp.maximum(jnp.sum(e, -1, keepdims=True), 1e-30)
        psum = psum + p
        o_cmp.append(_dot(p.astype(BF), vc))

    js = lax.broadcasted_iota(jnp.int32, (n_blk, 1), 0) * SEL_LEN
    cs = lax.broadcasted_iota(jnp.int32, (1, n_cmp), 1) * CMP_STRIDE
    ov = jnp.maximum(jnp.minimum(cs + CMP_LEN, js + SEL_LEN) - jnp.maximum(cs, js), 0).astype(F32) / CMP_LEN
    ov = ov.astype(BF)
    imp = sum(_dot_nt(ov, t) for t in _split3(psum))
    blk = lax.broadcasted_iota(jnp.int32, (n_blk, 1), 0)
    sel_shift = SEL_LEN.bit_length() - 1
    cur = jnp.right_shift(bi * Qb + lax.broadcasted_iota(jnp.int32, (1, Qb), 1), sel_shift)
    forced = (blk == 0) | (blk == cur) | (blk == cur - 1)
    imp = jnp.where(blk <= cur, jnp.where(forced, jnp.inf, imp), NEG_INF)
    rank = jnp.zeros((n_blk, Qb), F32)
    for j in range(n_blk):
        row = imp[j:j + 1, :]
        rank = rank + jnp.where((row > imp) | ((row == imp) & (blk > j)), 1.0, 0.0)
    sub = lax.broadcasted_iota(jnp.int32, (LANE - n_blk, Qb), 0)
    chosen = jnp.concatenate([jnp.where(rank < n_top, SEL_BIG, 0.0), jnp.where(sub < 2, 1.0, 0.0)], axis=0)
    chosen = chosen.T.astype(BF)
    bound = SEL_BIG - 0.5 - qpos.astype(F32)

    def selected(e):
        mask = _dot(chosen, exp_ref[:, 0:e]) > bound
        osel_ref[...] = _masked_attention(q, sel_ref[0:e, :], mask, H, Dh)

    _for_causal_extent(S, seg, Qb, selected)
    o_sel = osel_ref[...]

    wlen = WINDOW + Qb
    start = pl.multiple_of(jnp.maximum(bi * Qb - WINDOW, 0), Qb)
    kwin = win_ref[pl.ds(start, wlen), :]
    dlt = qpos - (start + lax.broadcasted_iota(jnp.int32, (1, wlen), 1))
    o_win = _masked_attention(q, kwin, (dlt >= 0) & (dlt < WINDOW), H, Dh)

    g = jax.nn.sigmoid(gt_ref[:, :3 * H])
    outs = []
    for h in range(H):
        outs.append(g[:, 3 * h:3 * h + 1] * o_cmp[h]
                    + g[:, 3 * h + 1:3 * h + 2] * o_sel[:, h * Dh:(h + 1) * Dh]
                    + g[:, 3 * h + 2:3 * h + 3] * o_win[:, h * Dh:(h + 1) * Dh])
    o_ref[...] = jnp.concatenate(outs, axis=-1)


def _nsa(nsa, nsa_aux, kvc, B, S):
    nb = S // NSA_QB
    n_cmp = kvc.shape[2]
    n_blk = S // SEL_LEN
    n_top = min(SEL_TOPN, n_blk)
    assert S >= WINDOW + NSA_QB and n_blk + 2 <= LANE and S < SEL_BIG and n_blk <= 256
    kpos = np.arange(S)
    expand = np.zeros((LANE, S), np.float32)
    expand[:n_blk] = kpos[None, :] // SEL_LEN == np.arange(n_blk)[:, None]
    expand[n_blk] = -(kpos // SEL_LEN * SEL_LEN)
    expand[n_blk + 1] = -(kpos % SEL_LEN)
    expand = jnp.asarray(expand, BF)
    return pl.pallas_call(
        functools.partial(_nsa_kernel, n_top=n_top, seg=min(CAUSAL_SEG, S)),
        grid=(B, nb),
        in_specs=[
            pl.BlockSpec((NSA_QB, 2 * LANE), lambda b, i: (b * nb + i, 0)),
            pl.BlockSpec((NSA_QB, LANE), lambda b, i: (b * nb + i, 0)),
            pl.BlockSpec((2, None, n_cmp, NSA_DH), lambda b, i: (0, b, 0, 0)),
            pl.BlockSpec((S, LANE), lambda b, i: (b, 3)),
            pl.BlockSpec((S, LANE), lambda b, i: (b, 4)),
            pl.BlockSpec((LANE, S), lambda b, i: (0, 0)),
        ],
        out_specs=pl.BlockSpec((NSA_QB, BRANCH_W), lambda b, i: (b * nb + i, 0)),
        out_shape=jax.ShapeDtypeStruct((B * S, BRANCH_W), F32),
        scratch_shapes=[pltpu.VMEM((NSA_QB, BRANCH_W), F32)],
        compiler_params=_params(("parallel", "arbitrary")),
        name="nsa",
    )(nsa, nsa_aux, kvc, nsa, nsa, expand)


def _merge_kernel(x_ref, y0, y1, y2, y3, wg_ref, wb_ref, wo_ref, g_ref, b_ref, o_ref, *, alpha):
    x = x_ref[...]
    xb = x.astype(BF)
    D = x.shape[1]
    merged = jnp.zeros_like(x)
    for n, y_ref in enumerate((y0, y1, y2, y3)):
        gate = jax.nn.sigmoid(_dot(xb, wg_ref[:, n * D:(n + 1) * D]))
        merged = merged + gate * _dot(y_ref[...].astype(BF), wb_ref[n])
    o_ref[...] = _layer_norm(alpha * x + _dot(merged.astype(BF), wo_ref[...]), g_ref[...], b_ref[...])


def _merge(x, ys, w_gate, w_branch, w_out, g, b, alpha, tm=512):
    T, D = x.shape
    full = lambda shape: pl.BlockSpec(shape, lambda i: (0,) * len(shape), pipeline_mode=pl.Buffered(1))
    return pl.pallas_call(
        functools.partial(_merge_kernel, alpha=alpha),
        grid=(T // tm,),
        in_specs=[pl.BlockSpec((tm, D), lambda i: (i, 0))]
        + [pl.BlockSpec((tm, BRANCH_W), lambda i: (i, 0))] * N_BRANCH
        + [full(w_gate.shape), full(w_branch.shape), full(w_out.shape), full((1, D)), full((1, D))],
        out_specs=pl.BlockSpec((tm, D), lambda i: (i, 0)),
        out_shape=jax.ShapeDtypeStruct((T, D), F32),
        compiler_params=_params(("parallel",)),
        name="merge",
    )(x, *ys, w_gate, w_branch, w_out, g, b)


def _matmul_kernel(a_ref, w_ref, o_ref):
    o_ref[...] = _dot(a_ref[...].astype(BF), w_ref[...])


def _matmul(a, w, tm):
    M, K = a.shape
    N = w.shape[1]
    return pl.pallas_call(
        _matmul_kernel,
        grid=(M // tm,),
        in_specs=[pl.BlockSpec((tm, K), lambda i: (i, 0)), pl.BlockSpec((K, N), lambda i: (0, 0))],
        out_specs=pl.BlockSpec((tm, N), lambda i: (i, 0)),
        out_shape=jax.ShapeDtypeStruct((M, N), F32),
        compiler_params=_params(("parallel",)),
        name="kv_proj",
    )(a, w)


def _xattn_kernel(x_ref, kv_ref, wq_ref, wo_ref, g_ref, b_ref, o_ref, *, alpha):
    x = x_ref[...]
    D = x.shape[1]
    dh = D // X_HEADS
    q = _dot(x.astype(BF), wq_ref[...])
    outs = []
    for h in range(X_HEADS):
        k = kv_ref[:, h * dh:(h + 1) * dh].astype(BF)
        v = kv_ref[:, D + h * dh:D + (h + 1) * dh].astype(BF)
        s = _dot_nt(q[:, h * dh:(h + 1) * dh].astype(BF), k) * (dh ** -0.5)
        e = jnp.exp(s - jnp.max(s, -1, keepdims=True))
        outs.append(_dot(e.astype(BF), v) / jnp.sum(e, -1, keepdims=True))
    att = jnp.concatenate(outs, axis=-1).astype(BF)
    o_ref[...] = _layer_norm(alpha * x + _dot(att, wo_ref[...]), g_ref[...], b_ref[...])


def _xattn(x, kv, wq, wo, g, b, alpha, S, M, tm=512):
    T, D = x.shape
    per = S // tm
    full = lambda shape: pl.BlockSpec(shape, lambda i: (0,) * len(shape))
    return pl.pallas_call(
        functools.partial(_xattn_kernel, alpha=alpha),
        grid=(T // tm,),
        in_specs=[pl.BlockSpec((tm, D), lambda i: (i, 0)),
                  pl.BlockSpec((M, 2 * D), lambda i: (i // per, 0)),
                  full(wq.shape), full(wo.shape), full((1, D)), full((1, D))],
        out_specs=pl.BlockSpec((tm, D), lambda i: (i, 0)),
        out_shape=jax.ShapeDtypeStruct((T, D), F32),
        compiler_params=_params(("parallel",)),
        name="xattn",
    )(x, kv, wq, wo, g, b)


def _rope_tables(S):
    pos = jnp.arange(S).astype(F32)

    def base(rot, theta):
        half = rot // 2
        inv = jnp.power(jnp.float32(theta), -2.0 * jnp.arange(half, dtype=F32) / rot)
        ang = pos[:, None] * inv[None, :]
        return jnp.cos(ang), jnp.sin(ang)

    def tile(hd, rot, theta, width, scale=1.0):
        c, s = base(rot, theta)
        lane = np.arange(LANE)
        jj = lane % hd
        half = rot // 2
        first = (jj < half) & (lane < width)
        second = (jj >= half) & (jj < rot) & (lane < width)
        idx = jj % half
        ct = jnp.where((first | second)[None, :], c[:, idx], 1.0) * scale
        s_hi = jnp.where(first[None, :], -s[:, idx], 0.0) * scale
        s_lo = jnp.where(second[None, :], s[:, idx], 0.0) * scale
        return ct, s_hi, s_lo

    q_scale = DSA_DH ** -0.5 * LOG2E
    tabs = [
        tile(RET_DK, RET_DK, RET_THETA, LANE),
        tile(RET_DK, RET_DK, RET_THETA, LANE, RET_DK ** -0.5),
        tile(DSA_DH, DSA_DH // ROPE_FRAC, ROPE_THETA, LANE, q_scale),
        tile(DSA_DH, DSA_DH // ROPE_FRAC, ROPE_THETA, DSA_DH),
        tile(IDX_DH, IDX_DH // ROPE_FRAC, ROPE_THETA, LANE),
        tile(IDX_DH, IDX_DH // ROPE_FRAC, ROPE_THETA, IDX_DH),
    ]
    return tuple(jnp.concatenate([t[i] for t in tabs], 1) for i in range(3))


def _pack_w_in(w):
    D = w.shape[0]
    wb = w.astype(BF)
    main = []
    for _, segs, width in _GROUPS:
        used = 0
        for s in segs:
            start, n = _SEG[s]
            main.append(wb[:, start:start + n])
            used += n
        main.append(jnp.zeros((D, width - used), BF))
    return jnp.concatenate(main, 1), wb[:, GATE_START:]


def kernel(x, mem, ln_g, ln_b, ffn1_w_gu, ffn1_w_down, w_in, cmp_w1, cmp_w2, cmp_pos, conv_w, conv_b,
           dt_bias, a_log, d_skip, ssm_norm_g, w_branch, w_out, xattn_wq, xattn_wkv, xattn_wo,
           ffn2_w_gu, ffn2_w_down):
    B, S, D = x.shape
    M = mem.shape[1]
    depth = ln_g.shape[0]
    alpha = (2 * depth) ** 0.25
    T = B * S
    tables = _rope_tables(S)
    h = x.reshape(T, D)
    mem2 = mem.reshape(B * M, D)
    for l in range(depth):
        lg = lambda i: ln_g[l, i][None, :]
        lb = lambda i: ln_b[l, i][None, :]
        h = _ffn(h, ffn1_w_gu[l].astype(BF), ffn1_w_down[l].astype(BF), lg(0), lb(0), alpha)

        wm, w_gate = _pack_w_in(w_in[l])
        ret, dsa, nsa, ssd, dsa_aux, nsa_aux = _inproj(h, wm, tables, S)
        y_ret = _retention(ret, B, S)
        y_dsa = _dsa(dsa, dsa_aux, B, S)
        g2 = jnp.stack([nsa[:, 2 * LANE:2 * LANE + NSA_DH], nsa[:, 2 * LANE + NSA_DH:3 * LANE]])
        kvc = _compress(g2.reshape(2, B, S // CMP_STRIDE, CMP_STRIDE * NSA_DH), cmp_w1[l], cmp_w2[l], cmp_pos[l])
        y_nsa = _nsa(nsa, nsa_aux, kvc, B, S)
        dt0 = SSM_INNER + SSM_CONV_DIM
        dt_rows = jnp.pad(ssd[:, dt0:dt0 + SSM_HEADS].reshape(B, S // SSM_CHUNK, SSM_CHUNK, SSM_HEADS)
                          .transpose(0, 1, 3, 2), ((0, 0), (0, 0), (0, CONV_PAD - SSM_HEADS), (0, 0)))
        y_ssd = _ssd(ssd, dt_rows, conv_w[l], conv_b[l], dt_bias[l], a_log[l], d_skip[l], ssm_norm_g[l], B, S)
        h = _merge(h, (y_ret, y_dsa, y_nsa, y_ssd), w_gate, w_branch[l].astype(BF), w_out[l].astype(BF),
                   lg(1), lb(1), alpha)

        kv = _matmul(mem2, xattn_wkv[l].astype(BF), tm=min(256, B * M))
        h = _xattn(h, kv, xattn_wq[l].astype(BF), xattn_wo[l].astype(BF), lg(2), lb(2), alpha, S, M)
        h = _ffn(h, ffn2_w_gu[l].astype(BF), ffn2_w_down[l].astype(BF), lg(3), lb(3), alpha)
    return h.reshape(B, S, D)
```

```python
import functools
import math

import numpy as np
import jax
import jax.numpy as jnp
from jax import lax
from jax.experimental import pallas as pl
from jax.experimental.pallas import tpu as pltpu

F32 = jnp.float32
BF = jnp.bfloat16
NEG_INF = float("-inf")

DSA_QB = 128
NSA_QB = 256
ROPE_THETA = 500000.0
ROPE_FRAC = 4
LN_EPS = 1e-5
RET_HEADS, RET_DK, RET_DV, RET_CHUNK, RET_THETA = 4, 32, 64, 128, 10000.0
DSA_HEADS, DSA_DH, IDX_HEADS, IDX_DH, DSA_TOPK = 4, 64, 8, 32, 256
NSA_HEADS, NSA_DH, CMP_LEN, CMP_STRIDE, SEL_LEN, SEL_TOPN, WINDOW = 4, 64, 32, 16, 64, 16, 512
SSM_HEADS, SSM_HEADDIM, SSM_GROUPS, SSM_STATE, SSM_CONV, SSM_CHUNK = 4, 64, 2, 128, 4, 128
SSM_INNER = SSM_HEADS * SSM_HEADDIM
SSM_CONV_DIM = SSM_INNER + 2 * SSM_GROUPS * SSM_STATE
N_BRANCH = 4
BRANCH_W = 256
X_HEADS = 4

LANE = 128
SUBLANE = 8
CONV_PAD = SUBLANE
VMEM_LIMIT = 56 * 1024 * 1024

_SEG_NAMES = ("r_q", "r_k", "r_v", "r_g", "d_q", "d_k", "d_v", "i_q", "i_k", "i_w",
              "n_q", "n_kc", "n_vc", "n_ks", "n_vs", "n_kw", "n_vw", "n_g", "s_z", "s_xbc", "s_dt")
_SEG_WIDTHS = (RET_HEADS * RET_DK, RET_HEADS * RET_DK, RET_HEADS * RET_DV, RET_HEADS * RET_DV,
               DSA_HEADS * DSA_DH, DSA_DH, DSA_DH, IDX_HEADS * IDX_DH, IDX_DH, IDX_HEADS,
               NSA_HEADS * NSA_DH, NSA_DH, NSA_DH, NSA_DH, NSA_DH, NSA_DH, NSA_DH, NSA_HEADS * 3,
               SSM_INNER, SSM_CONV_DIM, SSM_HEADS)
_SEG = {}
_o = 0
for _n, _w in zip(_SEG_NAMES, _SEG_WIDTHS):
    _SEG[_n] = (_o, _w)
    _o += _w
GATE_START = _o

_GROUPS = (
    ("ret", ("r_q", "r_k", "r_v", "r_g"), 768),
    ("dsa", ("d_q", "i_q", "d_k", "d_v", "i_k", "i_w"), 768),
    ("nsa", ("n_q", "n_kc", "n_vc", "n_ks", "n_vs", "n_kw", "n_vw", "n_g"), 768),
    ("ssd", ("s_z", "s_xbc", "s_dt"), 1152),
)
_ROPE = {
    "r_q": (RET_DK, RET_DK, 0), "r_k": (RET_DK, RET_DK, 1),
    "d_q": (DSA_DH, DSA_DH // ROPE_FRAC, 2), "n_q": (NSA_DH, NSA_DH // ROPE_FRAC, 2),
    "d_k": (DSA_DH, DSA_DH // ROPE_FRAC, 3), "n_kc": (NSA_DH, NSA_DH // ROPE_FRAC, 3),
    "n_ks": (NSA_DH, NSA_DH // ROPE_FRAC, 3), "n_kw": (NSA_DH, NSA_DH // ROPE_FRAC, 3),
    "i_q": (IDX_DH, IDX_DH // ROPE_FRAC, 4), "i_k": (IDX_DH, IDX_DH // ROPE_FRAC, 5),
}
N_TABLES = 6
_TABLE_HALF = (RET_DK // 2, RET_DK // 2, DSA_DH // ROPE_FRAC // 2, DSA_DH // ROPE_FRAC // 2,
               IDX_DH // ROPE_FRAC // 2, IDX_DH // ROPE_FRAC // 2)
LOG2E = math.log2(math.e)


def _build_layout():
    tile_seg = []
    group_tiles = []
    for _, segs, width in _GROUPS:
        cols = []
        for s in segs:
            cols += [s] * _SEG[s][1]
        cols += [None] * (width - len(cols))
        tile_seg += [cols[t * LANE:(t + 1) * LANE] for t in range(width // LANE)]
        group_tiles.append(width // LANE)
    roped_tiles = [t for t, segs in enumerate(tile_seg) if any(s in _ROPE for s in segs)]
    tile_table = [_ROPE[tile_seg[t][0]][2] for t in roped_tiles]
    return roped_tiles, tile_table, group_tiles, len(tile_seg) * LANE


_ROPED_TILES, _TILE_TABLE, _GROUP_TILES, N_MAIN = _build_layout()


def _dot(a, b):
    return jnp.dot(a, b, preferred_element_type=F32)


def _dot_nt(a, b):
    return lax.dot_general(a, b, (((1,), (1,)), ((), ())), preferred_element_type=F32)


def _dot_tn(a, b):
    return lax.dot_general(a, b, (((0,), (0,)), ((), ())), preferred_element_type=F32)


def _split3(a):
    hi = a.astype(BF)
    r1 = a - hi.astype(F32)
    mid = r1.astype(BF)
    lo = (r1 - mid.astype(F32)).astype(BF)
    return hi, mid, lo


def _layer_norm(v, g, b):
    mu = jnp.mean(v, -1, keepdims=True)
    d = v - mu
    var = jnp.mean(d * d, -1, keepdims=True)
    return d * lax.rsqrt(var + LN_EPS) * g + b


def _silu(v):
    return v * jax.nn.sigmoid(v)


def _params(sem):
    return pltpu.CompilerParams(dimension_semantics=sem, vmem_limit_bytes=VMEM_LIMIT)


def _ffn_kernel(x_ref, wg_ref, wu_ref, wd_ref, g_ref, b_ref, o_ref, *, alpha):
    x = x_ref[...]
    xb = x.astype(BF)
    h = (_silu(_dot(xb, wg_ref[...])) * _dot(xb, wu_ref[...])).astype(BF)
    o_ref[...] = _layer_norm(alpha * x + 0.5 * _dot(h, wd_ref[...]), g_ref[...], b_ref[...])


def _ffn(x, w_gu, w_down, g, b, alpha, tm=256, out_batch=None):
    D = x.shape[-1]
    T = x.size // D
    F = w_down.shape[0]
    once = pl.Buffered(1)

    def rows(batch):
        if batch is None:
            return pl.BlockSpec((tm, D), lambda i: (i, 0)), (T, D)
        per = T // batch // tm
        return pl.BlockSpec((None, tm, D), lambda i: (i // per, i % per, 0)), (batch, T // batch, D)

    x_spec, _ = rows(x.shape[0] if x.ndim == 3 else None)
    o_spec, o_shape = rows(out_batch)
    return pl.pallas_call(
        functools.partial(_ffn_kernel, alpha=alpha),
        grid=(T // tm,),
        in_specs=[
            x_spec,
            pl.BlockSpec((D, F), lambda i: (0, 0), pipeline_mode=once),
            pl.BlockSpec((D, F), lambda i: (0, 1), pipeline_mode=once),
            pl.BlockSpec((F, D), lambda i: (0, 0), pipeline_mode=once),
            pl.BlockSpec((1, D), lambda i: (0, 0)),
            pl.BlockSpec((1, D), lambda i: (0, 0)),
        ],
        out_specs=o_spec,
        out_shape=jax.ShapeDtypeStruct(o_shape, F32),
        compiler_params=_params(("parallel",)),
        name="ffn",
    )(x, w_gu, w_gu, w_down, g, b)


AUX_TILE = 5


def _inproj_kernel(x_ref, wm_ref, cos_ref, shi_ref, slo_ref, o_ret, o_dsa, o_nsa, o_ssd, o_dsa_aux, o_nsa_aux):
    xb = x_ref[...].astype(BF)
    outs = (o_ret, o_dsa, o_nsa, o_ssd)
    aux = (None, o_dsa_aux, o_nsa_aux, None)
    t0 = 0
    for o_ref, aux_ref, nt in zip(outs, aux, _GROUP_TILES):
        y = _dot(xb, wm_ref[:, t0 * LANE:(t0 + nt) * LANE])
        for t in range(nt):
            yt = y[:, t * LANE:(t + 1) * LANE]
            gt = t0 + t
            if gt in _ROPED_TILES:
                tab = _TILE_TABLE[_ROPED_TILES.index(gt)]
                half = _TABLE_HALF[tab]
                lanes = slice(tab * LANE, (tab + 1) * LANE)
                yt = (yt * cos_ref[:, lanes] + pltpu.roll(yt, LANE - half, 1) * shi_ref[:, lanes]
                      + pltpu.roll(yt, half, 1) * slo_ref[:, lanes])
            o_ref[:, t * LANE:(t + 1) * LANE] = yt.astype(o_ref.dtype)
            if aux_ref is not None and t == AUX_TILE:
                aux_ref[...] = yt
        t0 += nt


def _inproj(x, wm, tables, S, tm=256):
    T, D = x.shape
    ns = S // tm
    widths = [nt * LANE for nt in _GROUP_TILES] + [LANE, LANE]
    dtypes = [F32, BF, BF, F32, F32, F32]
    table_spec = pl.BlockSpec((tm, N_TABLES * LANE), lambda i: (i % ns, 0))
    return pl.pallas_call(
        _inproj_kernel,
        grid=(T // tm,),
        in_specs=[
            pl.BlockSpec((tm, D), lambda i: (i, 0)),
            pl.BlockSpec((D, N_MAIN), lambda i: (0, 0), pipeline_mode=pl.Buffered(1)),
            table_spec, table_spec, table_spec,
        ],
        out_specs=[pl.BlockSpec((tm, w), lambda i: (i, 0)) for w in widths],
        out_shape=[jax.ShapeDtypeStruct((T, w), dt) for w, dt in zip(widths, dtypes)],
        compiler_params=_params(("parallel",)),
        name="in_proj",
    )(x, wm, *tables)


def _ret_kernel(r_ref, o_ref, st_ref):
    C = RET_CHUNK

    @pl.when(pl.program_id(0) == 0)
    def _():
        st_ref[...] = jnp.zeros_like(st_ref)

    hq = RET_HEADS * RET_DK
    rel = (lax.broadcasted_iota(jnp.int32, (C, C), 0) - lax.broadcasted_iota(jnp.int32, (C, C), 1)).astype(F32)
    row = lax.broadcasted_iota(jnp.int32, (C, 1), 0).astype(F32)
    head_consts = []
    for h in range(RET_HEADS):
        lg = math.log1p(-(2.0 ** (-5.0 - h)))
        head_consts.append((jnp.where(rel >= 0, jnp.exp(jnp.maximum(rel, 0.0) * lg), 0.0),
                            jnp.exp((C - 1 - row) * lg), jnp.exp((row + 1.0) * lg), math.exp(C * lg)))
    for b in range(r_ref.shape[0]):
        r = r_ref[b]
        q, k = r[:, :hq], r[:, hq:2 * hq]
        v = r[:, 2 * hq:2 * hq + RET_HEADS * RET_DV]
        g = r[:, 2 * hq + RET_HEADS * RET_DV:]
        outs = []
        for h, (decay, zeta, xi, chunk_decay) in enumerate(head_consts):
            qh = q[:, h * RET_DK:(h + 1) * RET_DK].astype(BF)
            kh = k[:, h * RET_DK:(h + 1) * RET_DK]
            vh = v[:, h * RET_DV:(h + 1) * RET_DV].astype(BF)
            sc = _dot_nt(qh, kh.astype(BF)) * decay
            intra = _dot(sc.astype(BF), vh)
            prev = st_ref[b * RET_HEADS + h]
            cross = _dot(qh, prev.astype(BF)) * xi
            st_ref[b * RET_HEADS + h] = prev * chunk_decay + _dot_tn((kh * zeta).astype(BF), vh)
            o = intra + cross
            mu = jnp.mean(o, -1, keepdims=True)
            d = o - mu
            var = jnp.mean(d * d, -1, keepdims=True)
            outs.append(d * lax.rsqrt(var + LN_EPS))
        o_ref[b] = _silu(g) * jnp.concatenate(outs, axis=-1)


def _retention(ret, B, S):
    n = S // RET_CHUNK
    W = ret.shape[1]
    return pl.pallas_call(
        _ret_kernel,
        grid=(n,),
        in_specs=[pl.BlockSpec((B, RET_CHUNK, W), lambda c: (0, c, 0))],
        out_specs=pl.BlockSpec((B, RET_CHUNK, BRANCH_W), lambda c: (0, c, 0)),
        out_shape=jax.ShapeDtypeStruct((B, S, BRANCH_W), F32),
        scratch_shapes=[pltpu.VMEM((B * RET_HEADS, RET_DK, RET_DV), F32)],
        compiler_params=_params(("arbitrary",)),
        name="retention",
    )(ret.reshape(B, S, W)).reshape(B * S, BRANCH_W)


def _softplus(v):
    return jnp.maximum(v, 0.0) + jnp.log1p(jnp.exp(-jnp.abs(v)))


def _ssd_kernel(s_ref, cw_ref, cb_ref, dtb_c_ref, a_c_ref, dtb_r_ref, a_r_ref, dsk_ref, ng_ref,
                o_ref, st_ref, xpad_ref):
    Q, H, P, N = SSM_CHUNK, SSM_HEADS, SSM_HEADDIM, SSM_STATE

    @pl.when(pl.program_id(0) == 0)
    def _():
        st_ref[...] = jnp.zeros_like(st_ref)
        xpad_ref[:, 0:CONV_PAD, :] = jnp.zeros((xpad_ref.shape[0], CONV_PAD, SSM_CONV_DIM), F32)

    ri = lax.broadcasted_iota(jnp.int32, (Q, Q), 0)
    ci = lax.broadcasted_iota(jnp.int32, (Q, Q), 1)
    tril = ri >= ci
    lo_tri = jnp.where(tril, 1.0, 0.0).astype(BF)
    up_tri = jnp.where(ci >= ri, 1.0, 0.0).astype(BF)
    gn = SSM_GROUPS * N
    gw = SSM_INNER // SSM_GROUPS
    for b in range(s_ref.shape[0]):
        z = s_ref[b, :, :SSM_INNER]
        xpad_ref[b, CONV_PAD:, :] = s_ref[b, :, SSM_INNER:SSM_INNER + SSM_CONV_DIM]
        dt_raw = s_ref[b, :, SSM_INNER + SSM_CONV_DIM:SSM_INNER + SSM_CONV_DIM + H]
        conv = cb_ref[...]
        for kk in range(SSM_CONV):
            off = CONV_PAD - (SSM_CONV - 1) + kk
            conv = conv + cw_ref[kk:kk + 1, :] * xpad_ref[b, off:off + Q, :]
        xpad_ref[b, 0:CONV_PAD, :] = xpad_ref[b, Q:Q + CONV_PAD, :]
        xc = _silu(conv)
        xs = xc[:, :SSM_INNER]
        bm = xc[:, SSM_INNER:SSM_INNER + gn]
        cm = xc[:, SSM_INNER + gn:]

        dt_c = _softplus(dt_raw + dtb_c_ref[...])
        adt_c = dt_c * a_c_ref[...]
        dt_tile = s_ref[b, :, SSM_INNER + SSM_CONV_DIM:SSM_INNER + SSM_CONV_DIM + LANE]
        dt_r = _softplus(dt_tile.T[0:CONV_PAD, :] + dtb_r_ref[...])
        adt_r = dt_r * a_r_ref[...]
        acs_c = sum(_dot(lo_tri, t) for t in _split3(adt_c))
        acs_r = sum(_dot(t, up_tri) for t in _split3(adt_r))

        outs = []
        for grp in range(SSM_GROUPS):
            bg = bm[:, grp * N:(grp + 1) * N]
            cg = cm[:, grp * N:(grp + 1) * N].astype(BF)
            cb = _dot_nt(cg, bg.astype(BF))
            for h in range(grp * (H // SSM_GROUPS), (grp + 1) * (H // SSM_GROUPS)):
                a_col = acs_c[:, h:h + 1]
                a_last = acs_c[Q - 1:Q, h:h + 1]
                lm = jnp.exp(jnp.where(tril, a_col - acs_r[h:h + 1, :], NEG_INF))
                xh = xs[:, h * P:(h + 1) * P]
                xdt = (xh * dt_c[:, h:h + 1]).astype(BF)
                y = _dot((cb * lm).astype(BF), xdt)
                prev = st_ref[b * H + h]
                y = y + _dot(cg, prev.astype(BF)) * jnp.exp(a_col)
                st_ref[b * H + h] = (prev * jnp.exp(a_last)
                                     + _dot_tn((bg * jnp.exp(a_last - a_col)).astype(BF), xdt))
                outs.append(y)
        y = jnp.concatenate(outs, axis=-1) + xs * dsk_ref[...]
        y = y * _silu(z)
        normed = []
        for grp in range(SSM_GROUPS):
            yg = y[:, grp * gw:(grp + 1) * gw]
            normed.append(yg * lax.rsqrt(jnp.mean(yg * yg, -1, keepdims=True) + LN_EPS))
        o_ref[b] = jnp.concatenate(normed, axis=-1) * ng_ref[...]


def _ssd(ssd, conv_w, conv_b, dt_bias, a_log, d_skip, norm_g, B, S):
    Q, H = SSM_CHUNK, SSM_HEADS
    n = S // Q
    W = ssd.shape[1]
    assert W >= SSM_INNER + SSM_CONV_DIM + LANE
    a = -jnp.exp(a_log.astype(F32))
    pad_r = lambda v: jnp.broadcast_to(jnp.pad(v, (0, CONV_PAD - H))[:, None], (CONV_PAD, Q)).astype(F32)
    full = lambda shape: pl.BlockSpec(shape, lambda c: (0,) * len(shape))
    return pl.pallas_call(
        _ssd_kernel,
        grid=(n,),
        in_specs=[
            pl.BlockSpec((B, Q, W), lambda c: (0, c, 0)),
            full((SSM_CONV, SSM_CONV_DIM)), full((1, SSM_CONV_DIM)),
            full((1, H)), full((1, H)), full((CONV_PAD, Q)), full((CONV_PAD, Q)),
            full((1, SSM_INNER)), full((1, SSM_INNER)),
        ],
        out_specs=pl.BlockSpec((B, Q, BRANCH_W), lambda c: (0, c, 0)),
        out_shape=jax.ShapeDtypeStruct((B, S, BRANCH_W), F32),
        scratch_shapes=[pltpu.VMEM((B * H, SSM_STATE, SSM_HEADDIM), F32),
                        pltpu.VMEM((B, Q + CONV_PAD, SSM_CONV_DIM), F32)],
        compiler_params=_params(("arbitrary",)),
        name="ssd",
    )(ssd.reshape(B, S, W), conv_w, conv_b[None, :], dt_bias[None, :], a[None, :], pad_r(dt_bias),
      pad_r(a), jnp.repeat(d_skip, SSM_HEADDIM)[None, :], norm_g[None, :]).reshape(B * S, BRANCH_W)


def _masked_attention(q_bf, kv_bf, mask, heads, dh):
    k_bf = kv_bf[:, :dh]
    ones_v = jnp.where(lax.broadcasted_iota(jnp.int32, kv_bf.shape, 1) < dh, jnp.ones((), kv_bf.dtype), kv_bf)
    outs = []
    for h in range(heads):
        s = jnp.where(mask, _dot_nt(q_bf[:, h * dh:(h + 1) * dh], k_bf), NEG_INF)
        m = jnp.max(s, -1, keepdims=True)
        m = jnp.where(m > NEG_INF, m, 0.0)
        lv = _dot(jnp.exp2((s - m).astype(BF)), ones_v)
        outs.append(lv[:, dh:] / jnp.maximum(lv[:, 0:1], 1e-30))
    return jnp.concatenate(outs, axis=-1)


def _count(m):
    return jnp.sum(jnp.where(m, 1.0, 0.0), -1, keepdims=True)


IDX_KEY_CHUNK = 256
SEL_BIG = float(2 ** 20)
CAUSAL_SEG = 1024
BISECT_PLAIN_STEPS = 26
BISECT_MAX_STEPS = 400


def _kth_threshold(load, nvalid, vmin, vmax, c_ge0, c_gt0, k):
    kf = float(k)
    short = nvalid <= kf
    up0 = c_ge0 >= kf
    tie0 = up0 & (c_gt0 < kf) & (c_ge0 > kf) & (nvalid > kf)
    lo0 = jnp.where(short, jnp.finfo(F32).min, jnp.where(up0, 0.0, vmin))
    hi0 = jnp.where(up0, vmax, 0.0)
    done0 = jnp.where(short | tie0 | (c_ge0 == kf), 1.0, 0.0)

    def split(lo, hi, done):
        mid = jnp.where(done > 0.5, lo, lo + (hi - lo) * 0.5)
        c = _count(load() >= mid)
        up = (c >= kf) & (done < 0.5)
        return jnp.where(up, mid, lo), jnp.where(up | (done > 0.5), hi, mid), jnp.where(up & (c == kf), 1.0, done)

    def plain_cond(st):
        return (st[0] < BISECT_PLAIN_STEPS) & (jnp.min(st[3]) < 0.5)

    def plain_step(st):
        it, lo, hi, done = st
        lo, hi, done = split(lo, hi, done)
        lo, hi, done = split(lo, hi, done)
        return it + 2, lo, hi, done

    _, lo, hi, done = lax.while_loop(plain_cond, plain_step, (jnp.int32(0), lo0, hi0, done0))

    def exact_cond(st):
        return (st[0] < BISECT_MAX_STEPS) & (jnp.min(st[4]) < 0.5)

    def exact_step(st):
        it, lo, hi, tie, done = st
        sc = load()
        cand = jnp.min(jnp.where(sc >= lo, sc, jnp.inf), -1, keepdims=True)
        fin = (_count(sc > cand) < kf) & (done < 0.5)
        lo = jnp.where(fin, cand, lo)
        tie = jnp.where(fin, 1.0, tie)
        done = jnp.where(fin, 1.0, done)
        lo, hi, done = split(lo, hi, done)
        return it + 1, lo, hi, tie, done

    _, thr, _, tie, _ = lax.while_loop(
        exact_cond, exact_step, (jnp.int32(0), lo, hi, jnp.where(tie0, 1.0, 0.0), done))
    return thr, tie


def _dsa_body(q_ref, aux_ref, kv_ref, ik_ref, o_ref, sc_ref, *, n_keep, E):
    Qb, CH = DSA_QB, IDX_KEY_CHUNK
    bi = pl.program_id(1)
    qpos = bi * Qb + lax.broadcasted_iota(jnp.int32, (Qb, 1), 0)
    hq = DSA_HEADS * DSA_DH
    iw = aux_ref[:, IDX_DH:IDX_DH + IDX_HEADS] * (IDX_HEADS ** -0.5) * (IDX_DH ** -0.5)
    iq = [q_ref[:, hq + h * IDX_DH:hq + (h + 1) * IDX_DH] for h in range(IDX_HEADS)]
    fold = lambda f, a: functools.reduce(f, [a[:, j:j + LANE] for j in range(0, CH, LANE)])
    hi_acc = jnp.full((Qb, LANE), NEG_INF, F32)
    lo_acc = jnp.full((Qb, LANE), jnp.inf, F32)
    for c0 in range(0, E, CH):
        ik = ik_ref[c0:c0 + CH, :IDX_DH]
        acc = jnp.maximum(_dot_nt(iq[0], ik), 0.0) * iw[:, 0:1]
        for h in range(1, IDX_HEADS):
            acc = acc + jnp.maximum(_dot_nt(iq[h], ik), 0.0) * iw[:, h:h + 1]
        causal = c0 + lax.broadcasted_iota(jnp.int32, (1, CH), 1) <= qpos
        masked = jnp.where(causal, acc, NEG_INF)
        sc_ref[:, c0:c0 + CH] = masked
        hi_acc = jnp.maximum(hi_acc, fold(jnp.maximum, masked))
        lo_acc = jnp.minimum(lo_acc, fold(jnp.minimum, jnp.where(causal, acc, jnp.inf)))
    load = lambda: sc_ref[:, 0:E]
    thr, tie = _kth_threshold(load, (qpos + 1).astype(F32), jnp.min(lo_acc, -1, keepdims=True),
                              jnp.max(hi_acc, -1, keepdims=True), _count(load() >= 0.0), _count(load() > 0.0),
                              n_keep)

    @pl.when(jnp.max(tie) > 0.5)
    def resolve_ties():
        sc = load()
        gt = sc > thr
        eqf = jnp.where(sc == thr, 1.0, 0.0)
        need = n_keep - _count(gt)
        ch = 256 if E % 256 == 0 else LANE
        before = jnp.where(lax.broadcasted_iota(jnp.int32, (ch, ch), 0)
                           < lax.broadcasted_iota(jnp.int32, (ch, ch), 1), 1.0, 0.0).astype(BF)
        run = jnp.zeros((Qb, 1), F32)
        take = []
        for c0 in range(0, E, ch):
            eqc = eqf[:, c0:c0 + ch]
            prefix = _dot(eqc.astype(BF), before) + run
            take.append(jnp.where(prefix < need, eqc, 0.0))
            run = run + jnp.sum(eqc, -1, keepdims=True)
        sc_ref[:, 0:E] = jnp.where(gt | (jnp.concatenate(take, axis=-1) > 0.5), jnp.inf, NEG_INF)

    o_ref[...] = _masked_attention(q_ref[:, :hq], kv_ref[0:E, :], load() >= thr, DSA_HEADS, DSA_DH)


def _for_causal_extent(S, seg, qb, body):
    bi = pl.program_id(1)
    assert S % seg == 0 and seg % qb == 0
    for e in range(seg, S + 1, seg):
        pl.when((bi >= (e - seg) // qb) & (bi < e // qb))(functools.partial(body, e))


def _dsa_kernel(q_ref, aux_ref, kv_ref, ik_ref, o_ref, sc_ref, *, n_keep, seg):
    _for_causal_extent(kv_ref.shape[0], seg, DSA_QB,
                       lambda e: _dsa_body(q_ref, aux_ref, kv_ref, ik_ref, o_ref, sc_ref, n_keep=n_keep, E=e))


def _dsa(dsa, dsa_aux, B, S):
    nb = S // DSA_QB
    n_keep = min(DSA_TOPK, S // 4)
    return pl.pallas_call(
        functools.partial(_dsa_kernel, n_keep=n_keep, seg=min(CAUSAL_SEG, S)),
        grid=(B, nb),
        in_specs=[
            pl.BlockSpec((DSA_QB, 4 * LANE), lambda b, i: (b * nb + i, 0)),
            pl.BlockSpec((DSA_QB, LANE), lambda b, i: (b * nb + i, 0)),
            pl.BlockSpec((S, LANE), lambda b, i: (b, 4)),
            pl.BlockSpec((S, LANE), lambda b, i: (b, 5)),
        ],
        out_specs=pl.BlockSpec((DSA_QB, BRANCH_W), lambda b, i: (b * nb + i, 0)),
        out_shape=jax.ShapeDtypeStruct((B * S, BRANCH_W), F32),
        scratch_shapes=[pltpu.VMEM((DSA_QB, S), F32)],
        compiler_params=_params(("parallel", "arbitrary")),
        name="dsa",
    )(dsa, dsa_aux, dsa, dsa)


def _cmp_kernel(g_ref, w1a_ref, w1b_ref, pos_ref, w1_ref, w2_ref, o_ref):
    g = g_ref[...].astype(BF)
    n = g.shape[0]
    a = _dot(g, w1a_ref[...])
    b = _dot(g, w1b_ref[...])
    posterm = _dot(pos_ref[...].astype(BF), w1_ref[...])[0:1, :]
    h = jax.nn.gelu(a + pltpu.roll(b, n - 1, 0) + posterm)
    o_ref[...] = _dot(h.astype(BF), w2_ref[...]).astype(o_ref.dtype)


def _compress(g2, cmp_w1, cmp_w2, cmp_pos):
    _, B, n, W = g2.shape
    Dh = NSA_DH
    half = W
    w1 = cmp_w1.astype(BF)
    pos8 = jnp.broadcast_to(cmp_pos.reshape(2, 1, CMP_LEN * Dh), (2, 8, CMP_LEN * Dh))
    return pl.pallas_call(
        _cmp_kernel,
        grid=(2, B),
        in_specs=[
            pl.BlockSpec((None, None, n, W), lambda i, b: (i, b, 0, 0)),
            pl.BlockSpec((None, half, Dh), lambda i, b: (i, 0, 0)),
            pl.BlockSpec((None, half, Dh), lambda i, b: (i, 1, 0)),
            pl.BlockSpec((None, 8, CMP_LEN * Dh), lambda i, b: (i, 0, 0)),
            pl.BlockSpec((None, CMP_LEN * Dh, Dh), lambda i, b: (i, 0, 0)),
            pl.BlockSpec((None, Dh, Dh), lambda i, b: (i, 0, 0)),
        ],
        out_specs=pl.BlockSpec((None, None, n, Dh), lambda i, b: (i, b, 0, 0)),
        out_shape=jax.ShapeDtypeStruct((2, B, n, Dh), BF),
        compiler_params=_params(("parallel", "parallel")),
        name="nsa_compress",
    )(g2, w1, w1, pos8, w1, cmp_w2.astype(BF))


def _nsa_kernel(q_ref, gt_ref, kvc_ref, sel_ref, win_ref, exp_ref, o_ref, osel_ref, *, n_top, seg):
    Qb, H, Dh = NSA_QB, NSA_HEADS, NSA_DH
    S = sel_ref.shape[0]
    n_cmp = kvc_ref.shape[1]
    n_blk = S // SEL_LEN
    bi = pl.program_id(1)
    qpos = bi * Qb + lax.broadcasted_iota(jnp.int32, (Qb, 1), 0)
    q = q_ref[...]

    kc = kvc_ref[0]
    vc = kvc_ref[1]
    cidx = lax.broadcasted_iota(jnp.int32, (1, n_cmp), 1)
    vis = cidx * CMP_STRIDE + (CMP_LEN - 1) <= qpos
    o_cmp = []
    psum = jnp.zeros((Qb, n_cmp), F32)
    for h in range(H):
        s = jnp.where(vis, _dot_nt(q[:, h * Dh:(h + 1) * Dh], kc), NEG_INF)
        m = jnp.max(s, -1, keepdims=True)
        m = jnp.where(m > NEG_INF, m, 0.0)
        e = jnp.exp2(s - m)
        p = e / jnp.maximum(jnp.sum(e, -1, keepdims=True), 1e-30)
        psum = psum + p
        o_cmp.append(_dot(p.astype(BF), vc))

    js = lax.broadcasted_iota(jnp.int32, (n_blk, 1), 0) * SEL_LEN
    cs = lax.broadcasted_iota(jnp.int32, (1, n_cmp), 1) * CMP_STRIDE
    ov = jnp.maximum(jnp.minimum(cs + CMP_LEN, js + SEL_LEN) - jnp.maximum(cs, js), 0).astype(F32) / CMP_LEN
    ov = ov.astype(BF)
    imp = sum(_dot_nt(ov, t) for t in _split3(psum))
    blk = lax.broadcasted_iota(jnp.int32, (n_blk, 1), 0)
    sel_shift = SEL_LEN.bit_length() - 1
    cur = jnp.right_shift(bi * Qb + lax.broadcasted_iota(jnp.int32, (1, Qb), 1), sel_shift)
    forced = (blk == 0) | (blk == cur) | (blk == cur - 1)
    imp = jnp.where(blk <= cur, jnp.where(forced, jnp.inf, imp), NEG_INF)
    rank = jnp.zeros((n_blk, Qb), F32)
    for j in range(n_blk):
        row = imp[j:j + 1, :]
        rank = rank + jnp.where((row > imp) | ((row == imp) & (blk > j)), 1.0, 0.0)
    sub = lax.broadcasted_iota(jnp.int32, (LANE - n_blk, Qb), 0)
    chosen = jnp.concatenate([jnp.where(rank < n_top, SEL_BIG, 0.0), jnp.where(sub < 2, 1.0, 0.0)], axis=0)
    chosen = chosen.T.astype(BF)
    bound = SEL_BIG - 0.5 - qpos.astype(F32)

    def selected(e):
        mask = _dot(chosen, exp_ref[:, 0:e]) > bound
        osel_ref[...] = _masked_attention(q, sel_ref[0:e, :], mask, H, Dh)

    _for_causal_extent(S, seg, Qb, selected)
    o_sel = osel_ref[...]

    wlen = WINDOW + Qb
    start = pl.multiple_of(jnp.maximum(bi * Qb - WINDOW, 0), Qb)
    kwin = win_ref[pl.ds(start, wlen), :]
    dlt = qpos - (start + lax.broadcasted_iota(jnp.int32, (1, wlen), 1))
    o_win = _masked_attention(q, kwin, (dlt >= 0) & (dlt < WINDOW), H, Dh)

    g = jax.nn.sigmoid(gt_ref[:, :3 * H])
    outs = []
    for h in range(H):
        outs.append(g[:, 3 * h:3 * h + 1] * o_cmp[h]
                    + g[:, 3 * h + 1:3 * h + 2] * o_sel[:, h * Dh:(h + 1) * Dh]
                    + g[:, 3 * h + 2:3 * h + 3] * o_win[:, h * Dh:(h + 1) * Dh])
    o_ref[...] = jnp.concatenate(outs, axis=-1)


def _nsa(nsa, nsa_aux, kvc, B, S):
    nb = S // NSA_QB
    n_cmp = kvc.shape[2]
    n_blk = S // SEL_LEN
    n_top = min(SEL_TOPN, n_blk)
    assert S >= WINDOW + NSA_QB and n_blk + 2 <= LANE and S < SEL_BIG and n_blk <= 256
    kpos = np.arange(S)
    expand = np.zeros((LANE, S), np.float32)
    expand[:n_blk] = kpos[None, :] // SEL_LEN == np.arange(n_blk)[:, None]
    expand[n_blk] = -(kpos // SEL_LEN * SEL_LEN)
    expand[n_blk + 1] = -(kpos % SEL_LEN)
    expand = jnp.asarray(expand, BF)
    return pl.pallas_call(
        functools.partial(_nsa_kernel, n_top=n_top, seg=min(CAUSAL_SEG, S)),
        grid=(B, nb),
        in_specs=[
            pl.BlockSpec((NSA_QB, 2 * LANE), lambda b, i: (b * nb + i, 0)),
            pl.BlockSpec((NSA_QB, LANE), lambda b, i: (b * nb + i, 0)),
            pl.BlockSpec((2, None, n_cmp, NSA_DH), lambda b, i: (0, b, 0, 0)),
            pl.BlockSpec((S, LANE), lambda b, i: (b, 3)),
            pl.BlockSpec((S, LANE), lambda b, i: (b, 4)),
            pl.BlockSpec((LANE, S), lambda b, i: (0, 0)),
        ],
        out_specs=pl.BlockSpec((NSA_QB, BRANCH_W), lambda b, i: (b * nb + i, 0)),
        out_shape=jax.ShapeDtypeStruct((B * S, BRANCH_W), F32),
        scratch_shapes=[pltpu.VMEM((NSA_QB, BRANCH_W), F32)],
        compiler_params=_params(("parallel", "arbitrary")),
        name="nsa",
    )(nsa, nsa_aux, kvc, nsa, nsa, expand)


def _merge_kernel(x_ref, y0, y1, y2, y3, wg_ref, wb_ref, wo_ref, g_ref, b_ref, o_ref, *, alpha):
    x = x_ref[...]
    xb = x.astype(BF)
    D = x.shape[1]
    merged = jnp.zeros_like(x)
    for n, y_ref in enumerate((y0, y1, y2, y3)):
        gate = jax.nn.sigmoid(_dot(xb, wg_ref[:, n * D:(n + 1) * D]))
        merged = merged + gate * _dot(y_ref[...].astype(BF), wb_ref[n])
    o_ref[...] = _layer_norm(alpha * x + _dot(merged.astype(BF), wo_ref[...]), g_ref[...], b_ref[...])


def _merge(x, ys, w_gate, w_branch, w_out, g, b, alpha, tm=512):
    T, D = x.shape
    full = lambda shape: pl.BlockSpec(shape, lambda i: (0,) * len(shape), pipeline_mode=pl.Buffered(1))
    return pl.pallas_call(
        functools.partial(_merge_kernel, alpha=alpha),
        grid=(T // tm,),
        in_specs=[pl.BlockSpec((tm, D), lambda i: (i, 0))]
        + [pl.BlockSpec((tm, BRANCH_W), lambda i: (i, 0))] * N_BRANCH
        + [full(w_gate.shape), full(w_branch.shape), full(w_out.shape), full((1, D)), full((1, D))],
        out_specs=pl.BlockSpec((tm, D), lambda i: (i, 0)),
        out_shape=jax.ShapeDtypeStruct((T, D), F32),
        compiler_params=_params(("parallel",)),
        name="merge",
    )(x, *ys, w_gate, w_branch, w_out, g, b)


def _matmul_kernel(a_ref, w_ref, o_ref):
    o_ref[...] = _dot(a_ref[...].astype(BF), w_ref[...])


def _matmul(a, w, tm):
    M, K = a.shape
    N = w.shape[1]
    return pl.pallas_call(
        _matmul_kernel,
        grid=(M // tm,),
        in_specs=[pl.BlockSpec((tm, K), lambda i: (i, 0)), pl.BlockSpec((K, N), lambda i: (0, 0))],
        out_specs=pl.BlockSpec((tm, N), lambda i: (i, 0)),
        out_shape=jax.ShapeDtypeStruct((M, N), F32),
        compiler_params=_params(("parallel",)),
        name="kv_proj",
    )(a, w)


def _xattn_kernel(x_ref, kv_ref, wq_ref, wo_ref, g_ref, b_ref, o_ref, *, alpha):
    x = x_ref[...]
    D = x.shape[1]
    dh = D // X_HEADS
    q = _dot(x.astype(BF), wq_ref[...])
    outs = []
    for h in range(X_HEADS):
        k = kv_ref[:, h * dh:(h + 1) * dh].astype(BF)
        v = kv_ref[:, D + h * dh:D + (h + 1) * dh].astype(BF)
        s = _dot_nt(q[:, h * dh:(h + 1) * dh].astype(BF), k) * (dh ** -0.5)
        e = jnp.exp(s - jnp.max(s, -1, keepdims=True))
        outs.append(_dot(e.astype(BF), v) / jnp.sum(e, -1, keepdims=True))
    att = jnp.concatenate(outs, axis=-1).astype(BF)
    o_ref[...] = _layer_norm(alpha * x + _dot(att, wo_ref[...]), g_ref[...], b_ref[...])


def _xattn(x, kv, wq, wo, g, b, alpha, S, M, tm=512):
    T, D = x.shape
    per = S // tm
    full = lambda shape: pl.BlockSpec(shape, lambda i: (0,) * len(shape))
    return pl.pallas_call(
        functools.partial(_xattn_kernel, alpha=alpha),
        grid=(T // tm,),
        in_specs=[pl.BlockSpec((tm, D), lambda i: (i, 0)),
                  pl.BlockSpec((M, 2 * D), lambda i: (i // per, 0)),
                  full(wq.shape), full(wo.shape), full((1, D)), full((1, D))],
        out_specs=pl.BlockSpec((tm, D), lambda i: (i, 0)),
        out_shape=jax.ShapeDtypeStruct((T, D), F32),
        compiler_params=_params(("parallel",)),
        name="xattn",
    )(x, kv, wq, wo, g, b)


def _rope_tables(S):
    pos = jnp.arange(S).astype(F32)

    def base(rot, theta):
        half = rot // 2
        inv = jnp.power(jnp.float32(theta), -2.0 * jnp.arange(half, dtype=F32) / rot)
        ang = pos[:, None] * inv[None, :]
        return jnp.cos(ang), jnp.sin(ang)

    def tile(hd, rot, theta, width, scale=1.0):
        c, s = base(rot, theta)
        lane = np.arange(LANE)
        jj = lane % hd
        half = rot // 2
        first = (jj < half) & (lane < width)
        second = (jj >= half) & (jj < rot) & (lane < width)
        idx = jj % half
        ct = jnp.where((first | second)[None, :], c[:, idx], 1.0) * scale
        s_hi = jnp.where(first[None, :], -s[:, idx], 0.0) * scale
        s_lo = jnp.where(second[None, :], s[:, idx], 0.0) * scale
        return ct, s_hi, s_lo

    q_scale = DSA_DH ** -0.5 * LOG2E
    tabs = [
        tile(RET_DK, RET_DK, RET_THETA, LANE),
        tile(RET_DK, RET_DK, RET_THETA, LANE, RET_DK ** -0.5),
        tile(DSA_DH, DSA_DH // ROPE_FRAC, ROPE_THETA, LANE, q_scale),
        tile(DSA_DH, DSA_DH // ROPE_FRAC, ROPE_THETA, DSA_DH),
        tile(IDX_DH, IDX_DH // ROPE_FRAC, ROPE_THETA, LANE),
        tile(IDX_DH, IDX_DH // ROPE_FRAC, ROPE_THETA, IDX_DH),
    ]
    return tuple(jnp.concatenate([t[i] for t in tabs], 1) for i in range(3))


def _pack_w_in(w):
    D = w.shape[0]
    wb = w.astype(BF)
    main = []
    for _, segs, width in _GROUPS:
        used = 0
        for s in segs:
            start, n = _SEG[s]
            main.append(wb[:, start:start + n])
            used += n
        main.append(jnp.zeros((D, width - used), BF))
    return jnp.concatenate(main, 1), wb[:, GATE_START:]


def kernel(x, mem, ln_g, ln_b, ffn1_w_gu, ffn1_w_down, w_in, cmp_w1, cmp_w2, cmp_pos, conv_w, conv_b,
           dt_bias, a_log, d_skip, ssm_norm_g, w_branch, w_out, xattn_wq, xattn_wkv, xattn_wo,
           ffn2_w_gu, ffn2_w_down):
    B, S, D = x.shape
    M = mem.shape[1]
    depth = ln_g.shape[0]
    alpha = (2 * depth) ** 0.25
    T = B * S
    tables = _rope_tables(S)
    h = x
    mem2 = mem.reshape(B * M, D)
    for l in range(depth):
        lg = lambda i: ln_g[l, i][None, :]
        lb = lambda i: ln_b[l, i][None, :]
        h = _ffn(h, ffn1_w_gu[l].astype(BF), ffn1_w_down[l].astype(BF), lg(0), lb(0), alpha)

        wm, w_gate = _pack_w_in(w_in[l])
        ret, dsa, nsa, ssd, dsa_aux, nsa_aux = _inproj(h, wm, tables, S)
        y_ret = _retention(ret, B, S)
        y_dsa = _dsa(dsa, dsa_aux, B, S)
        g2 = jnp.stack([nsa[:, 2 * LANE:2 * LANE + NSA_DH], nsa[:, 2 * LANE + NSA_DH:3 * LANE]])
        kvc = _compress(g2.reshape(2, B, S // CMP_STRIDE, CMP_STRIDE * NSA_DH), cmp_w1[l], cmp_w2[l], cmp_pos[l])
        y_nsa = _nsa(nsa, nsa_aux, kvc, B, S)
        y_ssd = _ssd(ssd, conv_w[l], conv_b[l], dt_bias[l], a_log[l], d_skip[l], ssm_norm_g[l], B, S)
        h = _merge(h, (y_ret, y_dsa, y_nsa, y_ssd), w_gate, w_branch[l].astype(BF), w_out[l].astype(BF),
                   lg(1), lb(1), alpha)

        kv = _matmul(mem2, xattn_wkv[l].astype(BF), tm=min(256, B * M))
        h = _xattn(h, kv, xattn_wq[l].astype(BF), xattn_wo[l].astype(BF), lg(2), lb(2), alpha, S, M)
        h = _ffn(h, ffn2_w_gu[l].astype(BF), ffn2_w_down[l].astype(BF), lg(3), lb(3), alpha,
                 out_batch=B if l == depth - 1 else None)
    return h
```

```python
import functools
import math

import numpy as np
import jax
import jax.numpy as jnp
from jax import lax
from jax.experimental import pallas as pl
from jax.experimental.pallas import tpu as pltpu

F32 = jnp.float32
BF = jnp.bfloat16
NEG_INF = float("-inf")

DSA_QB = 128
NSA_QB = 256
ROPE_THETA = 500000.0
ROPE_FRAC = 4
LN_EPS = 1e-5
RET_HEADS, RET_DK, RET_DV, RET_CHUNK, RET_THETA = 4, 32, 64, 128, 10000.0
DSA_HEADS, DSA_DH, IDX_HEADS, IDX_DH, DSA_TOPK = 4, 64, 8, 32, 256
NSA_HEADS, NSA_DH, CMP_LEN, CMP_STRIDE, SEL_LEN, SEL_TOPN, WINDOW = 4, 64, 32, 16, 64, 16, 512
SSM_HEADS, SSM_HEADDIM, SSM_GROUPS, SSM_STATE, SSM_CONV, SSM_CHUNK = 4, 64, 2, 128, 4, 128
SSM_INNER = SSM_HEADS * SSM_HEADDIM
SSM_CONV_DIM = SSM_INNER + 2 * SSM_GROUPS * SSM_STATE
N_BRANCH = 4
BRANCH_W = 256
X_HEADS = 4

LANE = 128
SUBLANE = 8
CONV_PAD = SUBLANE
VMEM_LIMIT = 56 * 1024 * 1024

_SEG_NAMES = ("r_q", "r_k", "r_v", "r_g", "d_q", "d_k", "d_v", "i_q", "i_k", "i_w",
              "n_q", "n_kc", "n_vc", "n_ks", "n_vs", "n_kw", "n_vw", "n_g", "s_z", "s_xbc", "s_dt")
_SEG_WIDTHS = (RET_HEADS * RET_DK, RET_HEADS * RET_DK, RET_HEADS * RET_DV, RET_HEADS * RET_DV,
               DSA_HEADS * DSA_DH, DSA_DH, DSA_DH, IDX_HEADS * IDX_DH, IDX_DH, IDX_HEADS,
               NSA_HEADS * NSA_DH, NSA_DH, NSA_DH, NSA_DH, NSA_DH, NSA_DH, NSA_DH, NSA_HEADS * 3,
               SSM_INNER, SSM_CONV_DIM, SSM_HEADS)
_SEG = {}
_o = 0
for _n, _w in zip(_SEG_NAMES, _SEG_WIDTHS):
    _SEG[_n] = (_o, _w)
    _o += _w
GATE_START = _o

_GROUPS = (
    ("ret", ("r_q", "r_k", "r_v", "r_g"), 768),
    ("dsa", ("d_q", "i_q", "d_k", "d_v", "i_k", "i_w"), 768),
    ("nsa", ("n_q", "n_kc", "n_vc", "n_ks", "n_vs", "n_kw", "n_vw", "n_g"), 768),
    ("ssd", ("s_z", "s_xbc", "s_dt"), 1152),
)
_ROPE = {
    "r_q": (RET_DK, RET_DK, 0), "r_k": (RET_DK, RET_DK, 1),
    "d_q": (DSA_DH, DSA_DH // ROPE_FRAC, 2), "n_q": (NSA_DH, NSA_DH // ROPE_FRAC, 2),
    "d_k": (DSA_DH, DSA_DH // ROPE_FRAC, 3), "n_kc": (NSA_DH, NSA_DH // ROPE_FRAC, 3),
    "n_ks": (NSA_DH, NSA_DH // ROPE_FRAC, 3), "n_kw": (NSA_DH, NSA_DH // ROPE_FRAC, 3),
    "i_q": (IDX_DH, IDX_DH // ROPE_FRAC, 4), "i_k": (IDX_DH, IDX_DH // ROPE_FRAC, 5),
}
N_TABLES = 6
_TABLE_HALF = (RET_DK // 2, RET_DK // 2, DSA_DH // ROPE_FRAC // 2, DSA_DH // ROPE_FRAC // 2,
               IDX_DH // ROPE_FRAC // 2, IDX_DH // ROPE_FRAC // 2)
LOG2E = math.log2(math.e)


def _build_layout():
    tile_seg = []
    group_tiles = []
    for _, segs, width in _GROUPS:
        cols = []
        for s in segs:
            cols += [s] * _SEG[s][1]
        cols += [None] * (width - len(cols))
        tile_seg += [cols[t * LANE:(t + 1) * LANE] for t in range(width // LANE)]
        group_tiles.append(width // LANE)
    roped_tiles = [t for t, segs in enumerate(tile_seg) if any(s in _ROPE for s in segs)]
    tile_table = [_ROPE[tile_seg[t][0]][2] for t in roped_tiles]
    return roped_tiles, tile_table, group_tiles, len(tile_seg) * LANE


_ROPED_TILES, _TILE_TABLE, _GROUP_TILES, N_MAIN = _build_layout()


def _dot(a, b):
    return jnp.dot(a, b, preferred_element_type=F32)


def _dot_nt(a, b):
    return lax.dot_general(a, b, (((1,), (1,)), ((), ())), preferred_element_type=F32)


def _dot_tn(a, b):
    return lax.dot_general(a, b, (((0,), (0,)), ((), ())), preferred_element_type=F32)


def _split3(a):
    hi = a.astype(BF)
    r1 = a - hi.astype(F32)
    mid = r1.astype(BF)
    lo = (r1 - mid.astype(F32)).astype(BF)
    return hi, mid, lo


def _layer_norm(v, g, b):
    mu = jnp.mean(v, -1, keepdims=True)
    d = v - mu
    var = jnp.mean(d * d, -1, keepdims=True)
    return d * lax.rsqrt(var + LN_EPS) * g + b


def _silu(v):
    return v * jax.nn.sigmoid(v)


def _params(sem):
    return pltpu.CompilerParams(dimension_semantics=sem, vmem_limit_bytes=VMEM_LIMIT)


def _ffn_kernel(x_ref, wg_ref, wu_ref, wd_ref, g_ref, b_ref, o_ref, *, alpha):
    x = x_ref[...]
    xb = x.astype(BF)
    h = (_silu(_dot(xb, wg_ref[...])) * _dot(xb, wu_ref[...])).astype(BF)
    o_ref[...] = _layer_norm(alpha * x + 0.5 * _dot(h, wd_ref[...]), g_ref[...], b_ref[...])


def _layer_spec(w, layer, block=None, index=None):
    block = tuple(w.shape[1:]) if block is None else block
    index = (0,) * len(block) if index is None else index
    return pl.BlockSpec((None,) + block, lambda *_: (layer,) + index, pipeline_mode=pl.Buffered(1))


def _ffn(x, w_gu, w_down, layer, g, b, alpha, tm=256, out_batch=None):
    D = x.shape[-1]
    T = x.size // D
    F = w_down.shape[1]

    def rows(batch):
        if batch is None:
            return pl.BlockSpec((tm, D), lambda i: (i, 0)), (T, D)
        per = T // batch // tm
        return pl.BlockSpec((None, tm, D), lambda i: (i // per, i % per, 0)), (batch, T // batch, D)

    x_spec, _ = rows(x.shape[0] if x.ndim == 3 else None)
    o_spec, o_shape = rows(out_batch)
    return pl.pallas_call(
        functools.partial(_ffn_kernel, alpha=alpha),
        grid=(T // tm,),
        in_specs=[
            x_spec,
            _layer_spec(w_gu, layer, (D, F), (0, 0)),
            _layer_spec(w_gu, layer, (D, F), (0, 1)),
            _layer_spec(w_down, layer),
            pl.BlockSpec((1, D), lambda i: (0, 0)),
            pl.BlockSpec((1, D), lambda i: (0, 0)),
        ],
        out_specs=o_spec,
        out_shape=jax.ShapeDtypeStruct(o_shape, F32),
        compiler_params=_params(("parallel",)),
        name="ffn",
    )(x, w_gu, w_gu, w_down, g, b)


AUX_TILE = 5


def _inproj_kernel(x_ref, wm_ref, cos_ref, shi_ref, slo_ref, o_ret, o_dsa, o_nsa, o_ssd, o_dsa_aux, o_nsa_aux):
    xb = x_ref[...].astype(BF)
    outs = (o_ret, o_dsa, o_nsa, o_ssd)
    aux = (None, o_dsa_aux, o_nsa_aux, None)
    t0 = 0
    for o_ref, aux_ref, nt in zip(outs, aux, _GROUP_TILES):
        y = _dot(xb, wm_ref[:, t0 * LANE:(t0 + nt) * LANE])
        for t in range(nt):
            yt = y[:, t * LANE:(t + 1) * LANE]
            gt = t0 + t
            if gt in _ROPED_TILES:
                tab = _TILE_TABLE[_ROPED_TILES.index(gt)]
                half = _TABLE_HALF[tab]
                lanes = slice(tab * LANE, (tab + 1) * LANE)
                yt = (yt * cos_ref[:, lanes] + pltpu.roll(yt, LANE - half, 1) * shi_ref[:, lanes]
                      + pltpu.roll(yt, half, 1) * slo_ref[:, lanes])
            o_ref[:, t * LANE:(t + 1) * LANE] = yt.astype(o_ref.dtype)
            if aux_ref is not None and t == AUX_TILE:
                aux_ref[...] = yt
        t0 += nt


def _inproj(x, wm, tables, S, tm=256):
    T, D = x.shape
    ns = S // tm
    widths = [nt * LANE for nt in _GROUP_TILES] + [LANE, LANE]
    dtypes = [F32, BF, BF, F32, F32, F32]
    table_spec = pl.BlockSpec((tm, N_TABLES * LANE), lambda i: (i % ns, 0))
    return pl.pallas_call(
        _inproj_kernel,
        grid=(T // tm,),
        in_specs=[
            pl.BlockSpec((tm, D), lambda i: (i, 0)),
            pl.BlockSpec((D, N_MAIN), lambda i: (0, 0), pipeline_mode=pl.Buffered(1)),
            table_spec, table_spec, table_spec,
        ],
        out_specs=[pl.BlockSpec((tm, w), lambda i: (i, 0)) for w in widths],
        out_shape=[jax.ShapeDtypeStruct((T, w), dt) for w, dt in zip(widths, dtypes)],
        compiler_params=_params(("parallel",)),
        name="in_proj",
    )(x, wm, *tables)


def _ret_kernel(r_ref, o_ref, st_ref):
    C = RET_CHUNK

    @pl.when(pl.program_id(0) == 0)
    def _():
        st_ref[...] = jnp.zeros_like(st_ref)

    hq = RET_HEADS * RET_DK
    rel = (lax.broadcasted_iota(jnp.int32, (C, C), 0) - lax.broadcasted_iota(jnp.int32, (C, C), 1)).astype(F32)
    row = lax.broadcasted_iota(jnp.int32, (C, 1), 0).astype(F32)
    head_consts = []
    for h in range(RET_HEADS):
        lg = math.log1p(-(2.0 ** (-5.0 - h)))
        head_consts.append((jnp.where(rel >= 0, jnp.exp(jnp.maximum(rel, 0.0) * lg), 0.0),
                            jnp.exp((C - 1 - row) * lg), jnp.exp((row + 1.0) * lg), math.exp(C * lg)))
    for b in range(r_ref.shape[0]):
        r = r_ref[b]
        q, k = r[:, :hq], r[:, hq:2 * hq]
        v = r[:, 2 * hq:2 * hq + RET_HEADS * RET_DV]
        g = r[:, 2 * hq + RET_HEADS * RET_DV:]
        outs = []
        for h, (decay, zeta, xi, chunk_decay) in enumerate(head_consts):
            qh = q[:, h * RET_DK:(h + 1) * RET_DK].astype(BF)
            kh = k[:, h * RET_DK:(h + 1) * RET_DK]
            vh = v[:, h * RET_DV:(h + 1) * RET_DV].astype(BF)
            sc = _dot_nt(qh, kh.astype(BF)) * decay
            intra = _dot(sc.astype(BF), vh)
            prev = st_ref[b * RET_HEADS + h]
            cross = _dot(qh, prev.astype(BF)) * xi
            st_ref[b * RET_HEADS + h] = prev * chunk_decay + _dot_tn((kh * zeta).astype(BF), vh)
            o = intra + cross
            mu = jnp.mean(o, -1, keepdims=True)
            d = o - mu
            var = jnp.mean(d * d, -1, keepdims=True)
            outs.append(d * lax.rsqrt(var + LN_EPS))
        o_ref[b] = _silu(g) * jnp.concatenate(outs, axis=-1)


def _retention(ret, B, S):
    n = S // RET_CHUNK
    W = ret.shape[1]
    return pl.pallas_call(
        _ret_kernel,
        grid=(n,),
        in_specs=[pl.BlockSpec((B, RET_CHUNK, W), lambda c: (0, c, 0))],
        out_specs=pl.BlockSpec((B, RET_CHUNK, BRANCH_W), lambda c: (0, c, 0)),
        out_shape=jax.ShapeDtypeStruct((B, S, BRANCH_W), F32),
        scratch_shapes=[pltpu.VMEM((B * RET_HEADS, RET_DK, RET_DV), F32)],
        compiler_params=_params(("arbitrary",)),
        name="retention",
    )(ret.reshape(B, S, W)).reshape(B * S, BRANCH_W)


def _softplus(v):
    return jnp.maximum(v, 0.0) + jnp.log1p(jnp.exp(-jnp.abs(v)))


def _ssd_kernel(s_ref, cw_ref, cb_ref, dtb_c_ref, a_c_ref, dtb_r_ref, a_r_ref, dsk_ref, ng_ref,
                o_ref, st_ref, xpad_ref):
    Q, H, P, N = SSM_CHUNK, SSM_HEADS, SSM_HEADDIM, SSM_STATE

    @pl.when(pl.program_id(0) == 0)
    def _():
        st_ref[...] = jnp.zeros_like(st_ref)
        xpad_ref[:, 0:CONV_PAD, :] = jnp.zeros((xpad_ref.shape[0], CONV_PAD, SSM_CONV_DIM), F32)

    ri = lax.broadcasted_iota(jnp.int32, (Q, Q), 0)
    ci = lax.broadcasted_iota(jnp.int32, (Q, Q), 1)
    tril = ri >= ci
    lo_tri = jnp.where(tril, 1.0, 0.0).astype(BF)
    up_tri = jnp.where(ci >= ri, 1.0, 0.0).astype(BF)
    gn = SSM_GROUPS * N
    gw = SSM_INNER // SSM_GROUPS
    for b in range(s_ref.shape[0]):
        z = s_ref[b, :, :SSM_INNER]
        xpad_ref[b, CONV_PAD:, :] = s_ref[b, :, SSM_INNER:SSM_INNER + SSM_CONV_DIM]
        dt_raw = s_ref[b, :, SSM_INNER + SSM_CONV_DIM:SSM_INNER + SSM_CONV_DIM + H]
        conv = cb_ref[...]
        for kk in range(SSM_CONV):
            off = CONV_PAD - (SSM_CONV - 1) + kk
            conv = conv + cw_ref[kk:kk + 1, :] * xpad_ref[b, off:off + Q, :]
        xpad_ref[b, 0:CONV_PAD, :] = xpad_ref[b, Q:Q + CONV_PAD, :]
        xc = _silu(conv)
        xs = xc[:, :SSM_INNER]
        bm = xc[:, SSM_INNER:SSM_INNER + gn]
        cm = xc[:, SSM_INNER + gn:]

        dt_c = _softplus(dt_raw + dtb_c_ref[...])
        adt_c = dt_c * a_c_ref[...]
        dt_tile = s_ref[b, :, SSM_INNER + SSM_CONV_DIM:SSM_INNER + SSM_CONV_DIM + LANE]
        dt_r = _softplus(dt_tile.T[0:CONV_PAD, :] + dtb_r_ref[...])
        adt_r = dt_r * a_r_ref[...]
        acs_c = sum(_dot(lo_tri, t) for t in _split3(adt_c))
        acs_r = sum(_dot(t, up_tri) for t in _split3(adt_r))

        outs = []
        for grp in range(SSM_GROUPS):
            bg = bm[:, grp * N:(grp + 1) * N]
            cg = cm[:, grp * N:(grp + 1) * N].astype(BF)
            cb = _dot_nt(cg, bg.astype(BF))
            for h in range(grp * (H // SSM_GROUPS), (grp + 1) * (H // SSM_GROUPS)):
                a_col = acs_c[:, h:h + 1]
                a_last = acs_c[Q - 1:Q, h:h + 1]
                lm = jnp.exp(jnp.where(tril, a_col - acs_r[h:h + 1, :], NEG_INF))
                xh = xs[:, h * P:(h + 1) * P]
                xdt = (xh * dt_c[:, h:h + 1]).astype(BF)
                y = _dot((cb * lm).astype(BF), xdt)
                prev = st_ref[b * H + h]
                y = y + _dot(cg, prev.astype(BF)) * jnp.exp(a_col)
                st_ref[b * H + h] = (prev * jnp.exp(a_last)
                                     + _dot_tn((bg * jnp.exp(a_last - a_col)).astype(BF), xdt))
                outs.append(y)
        y = jnp.concatenate(outs, axis=-1) + xs * dsk_ref[...]
        y = y * _silu(z)
        normed = []
        for grp in range(SSM_GROUPS):
            yg = y[:, grp * gw:(grp + 1) * gw]
            normed.append(yg * lax.rsqrt(jnp.mean(yg * yg, -1, keepdims=True) + LN_EPS))
        o_ref[b] = jnp.concatenate(normed, axis=-1) * ng_ref[...]


def _ssd(ssd, conv_w, conv_b, dt_bias, a_log, d_skip, norm_g, B, S):
    Q, H = SSM_CHUNK, SSM_HEADS
    n = S // Q
    W = ssd.shape[1]
    assert W >= SSM_INNER + SSM_CONV_DIM + LANE
    a = -jnp.exp(a_log.astype(F32))
    pad_r = lambda v: jnp.broadcast_to(jnp.pad(v, (0, CONV_PAD - H))[:, None], (CONV_PAD, Q)).astype(F32)
    full = lambda shape: pl.BlockSpec(shape, lambda c: (0,) * len(shape))
    return pl.pallas_call(
        _ssd_kernel,
        grid=(n,),
        in_specs=[
            pl.BlockSpec((B, Q, W), lambda c: (0, c, 0)),
            full((SSM_CONV, SSM_CONV_DIM)), full((1, SSM_CONV_DIM)),
            full((1, H)), full((1, H)), full((CONV_PAD, Q)), full((CONV_PAD, Q)),
            full((1, SSM_INNER)), full((1, SSM_INNER)),
        ],
        out_specs=pl.BlockSpec((B, Q, BRANCH_W), lambda c: (0, c, 0)),
        out_shape=jax.ShapeDtypeStruct((B, S, BRANCH_W), F32),
        scratch_shapes=[pltpu.VMEM((B * H, SSM_STATE, SSM_HEADDIM), F32),
                        pltpu.VMEM((B, Q + CONV_PAD, SSM_CONV_DIM), F32)],
        compiler_params=_params(("arbitrary",)),
        name="ssd",
    )(ssd.reshape(B, S, W), conv_w, conv_b[None, :], dt_bias[None, :], a[None, :], pad_r(dt_bias),
      pad_r(a), jnp.repeat(d_skip, SSM_HEADDIM)[None, :], norm_g[None, :]).reshape(B * S, BRANCH_W)


def _masked_attention(q_bf, kv_bf, mask, heads, dh):
    k_bf = kv_bf[:, :dh]
    ones_v = jnp.where(lax.broadcasted_iota(jnp.int32, kv_bf.shape, 1) < dh, jnp.ones((), kv_bf.dtype), kv_bf)
    outs = []
    for h in range(heads):
        s = jnp.where(mask, _dot_nt(q_bf[:, h * dh:(h + 1) * dh], k_bf), NEG_INF)
        m = jnp.max(s, -1, keepdims=True)
        m = jnp.where(m > NEG_INF, m, 0.0)
        lv = _dot(jnp.exp2((s - m).astype(BF)), ones_v)
        outs.append(lv[:, dh:] / jnp.maximum(lv[:, 0:1], 1e-30))
    return jnp.concatenate(outs, axis=-1)


def _count(m):
    return jnp.sum(jnp.where(m, 1.0, 0.0), -1, keepdims=True)


IDX_KEY_CHUNK = 256
SEL_BIG = float(2 ** 20)
CAUSAL_SEG = 1024
BISECT_PLAIN_STEPS = 26
BISECT_MAX_STEPS = 400


def _kth_threshold(load, nvalid, vmin, vmax, c_ge0, c_gt0, k):
    kf = float(k)
    short = nvalid <= kf
    up0 = c_ge0 >= kf
    tie0 = up0 & (c_gt0 < kf) & (c_ge0 > kf) & (nvalid > kf)
    lo0 = jnp.where(short, jnp.finfo(F32).min, jnp.where(up0, 0.0, vmin))
    hi0 = jnp.where(up0, vmax, 0.0)
    done0 = jnp.where(short | tie0 | (c_ge0 == kf), 1.0, 0.0)

    def split(lo, hi, done):
        mid = jnp.where(done > 0.5, lo, lo + (hi - lo) * 0.5)
        c = _count(load() >= mid)
        up = (c >= kf) & (done < 0.5)
        return jnp.where(up, mid, lo), jnp.where(up | (done > 0.5), hi, mid), jnp.where(up & (c == kf), 1.0, done)

    def plain_cond(st):
        return (st[0] < BISECT_PLAIN_STEPS) & (jnp.min(st[3]) < 0.5)

    def plain_step(st):
        it, lo, hi, done = st
        lo, hi, done = split(lo, hi, done)
        lo, hi, done = split(lo, hi, done)
        return it + 2, lo, hi, done

    _, lo, hi, done = lax.while_loop(plain_cond, plain_step, (jnp.int32(0), lo0, hi0, done0))

    def exact_cond(st):
        return (st[0] < BISECT_MAX_STEPS) & (jnp.min(st[4]) < 0.5)

    def exact_step(st):
        it, lo, hi, tie, done = st
        sc = load()
        cand = jnp.min(jnp.where(sc >= lo, sc, jnp.inf), -1, keepdims=True)
        fin = (_count(sc > cand) < kf) & (done < 0.5)
        lo = jnp.where(fin, cand, lo)
        tie = jnp.where(fin, 1.0, tie)
        done = jnp.where(fin, 1.0, done)
        lo, hi, done = split(lo, hi, done)
        return it + 1, lo, hi, tie, done

    _, thr, _, tie, _ = lax.while_loop(
        exact_cond, exact_step, (jnp.int32(0), lo, hi, jnp.where(tie0, 1.0, 0.0), done))
    return thr, tie


def _dsa_body(q_ref, aux_ref, kv_ref, ik_ref, o_ref, sc_ref, *, n_keep, E):
    Qb, CH = DSA_QB, IDX_KEY_CHUNK
    bi = pl.program_id(1)
    qpos = bi * Qb + lax.broadcasted_iota(jnp.int32, (Qb, 1), 0)
    hq = DSA_HEADS * DSA_DH
    iw = aux_ref[:, IDX_DH:IDX_DH + IDX_HEADS] * (IDX_HEADS ** -0.5) * (IDX_DH ** -0.5)
    iq = [q_ref[:, hq + h * IDX_DH:hq + (h + 1) * IDX_DH] for h in range(IDX_HEADS)]
    fold = lambda f, a: functools.reduce(f, [a[:, j:j + LANE] for j in range(0, CH, LANE)])
    hi_acc = jnp.full((Qb, LANE), NEG_INF, F32)
    lo_acc = jnp.full((Qb, LANE), jnp.inf, F32)
    for c0 in range(0, E, CH):
        ik = ik_ref[c0:c0 + CH, :IDX_DH]
        acc = jnp.maximum(_dot_nt(iq[0], ik), 0.0) * iw[:, 0:1]
        for h in range(1, IDX_HEADS):
            acc = acc + jnp.maximum(_dot_nt(iq[h], ik), 0.0) * iw[:, h:h + 1]
        causal = c0 + lax.broadcasted_iota(jnp.int32, (1, CH), 1) <= qpos
        masked = jnp.where(causal, acc, NEG_INF)
        sc_ref[:, c0:c0 + CH] = masked
        hi_acc = jnp.maximum(hi_acc, fold(jnp.maximum, masked))
        lo_acc = jnp.minimum(lo_acc, fold(jnp.minimum, jnp.where(causal, acc, jnp.inf)))
    load = lambda: sc_ref[:, 0:E]
    thr, tie = _kth_threshold(load, (qpos + 1).astype(F32), jnp.min(lo_acc, -1, keepdims=True),
                              jnp.max(hi_acc, -1, keepdims=True), _count(load() >= 0.0), _count(load() > 0.0),
                              n_keep)

    @pl.when(jnp.max(tie) > 0.5)
    def resolve_ties():
        sc = load()
        gt = sc > thr
        eqf = jnp.where(sc == thr, 1.0, 0.0)
        need = n_keep - _count(gt)
        ch = 256 if E % 256 == 0 else LANE
        before = jnp.where(lax.broadcasted_iota(jnp.int32, (ch, ch), 0)
                           < lax.broadcasted_iota(jnp.int32, (ch, ch), 1), 1.0, 0.0).astype(BF)
        run = jnp.zeros((Qb, 1), F32)
        take = []
        for c0 in range(0, E, ch):
            eqc = eqf[:, c0:c0 + ch]
            prefix = _dot(eqc.astype(BF), before) + run
            take.append(jnp.where(prefix < need, eqc, 0.0))
            run = run + jnp.sum(eqc, -1, keepdims=True)
        sc_ref[:, 0:E] = jnp.where(gt | (jnp.concatenate(take, axis=-1) > 0.5), jnp.inf, NEG_INF)

    o_ref[...] = _masked_attention(q_ref[:, :hq], kv_ref[0:E, :], load() >= thr, DSA_HEADS, DSA_DH)


def _for_causal_extent(S, seg, qb, body):
    bi = pl.program_id(1)
    assert S % seg == 0 and seg % qb == 0
    for e in range(seg, S + 1, seg):
        pl.when((bi >= (e - seg) // qb) & (bi < e // qb))(functools.partial(body, e))


def _dsa_kernel(q_ref, aux_ref, kv_ref, ik_ref, o_ref, sc_ref, *, n_keep, seg):
    _for_causal_extent(kv_ref.shape[0], seg, DSA_QB,
                       lambda e: _dsa_body(q_ref, aux_ref, kv_ref, ik_ref, o_ref, sc_ref, n_keep=n_keep, E=e))


def _dsa(dsa, dsa_aux, B, S):
    nb = S // DSA_QB
    n_keep = min(DSA_TOPK, S // 4)
    return pl.pallas_call(
        functools.partial(_dsa_kernel, n_keep=n_keep, seg=min(CAUSAL_SEG, S)),
        grid=(B, nb),
        in_specs=[
            pl.BlockSpec((DSA_QB, 4 * LANE), lambda b, i: (b * nb + i, 0)),
            pl.BlockSpec((DSA_QB, LANE), lambda b, i: (b * nb + i, 0)),
            pl.BlockSpec((S, LANE), lambda b, i: (b, 4)),
            pl.BlockSpec((S, LANE), lambda b, i: (b, 5)),
        ],
        out_specs=pl.BlockSpec((DSA_QB, BRANCH_W), lambda b, i: (b * nb + i, 0)),
        out_shape=jax.ShapeDtypeStruct((B * S, BRANCH_W), F32),
        scratch_shapes=[pltpu.VMEM((DSA_QB, S), F32)],
        compiler_params=_params(("parallel", "arbitrary")),
        name="dsa",
    )(dsa, dsa_aux, dsa, dsa)


def _cmp_kernel(g_ref, w1a_ref, w1b_ref, pos_ref, w1_ref, w2_ref, o_ref):
    g = g_ref[...].astype(BF)
    n = g.shape[0]
    a = _dot(g, w1a_ref[...])
    b = _dot(g, w1b_ref[...])
    posterm = _dot(pos_ref[...].astype(BF), w1_ref[...])[0:1, :]
    h = jax.nn.gelu(a + pltpu.roll(b, n - 1, 0) + posterm)
    o_ref[...] = _dot(h.astype(BF), w2_ref[...]).astype(o_ref.dtype)


def _compress(g2, cmp_w1, cmp_w2, cmp_pos):
    _, B, n, W = g2.shape
    Dh = NSA_DH
    half = W
    w1 = cmp_w1.astype(BF)
    pos8 = jnp.broadcast_to(cmp_pos.reshape(2, 1, CMP_LEN * Dh), (2, 8, CMP_LEN * Dh))
    return pl.pallas_call(
        _cmp_kernel,
        grid=(2, B),
        in_specs=[
            pl.BlockSpec((None, None, n, W), lambda i, b: (i, b, 0, 0)),
            pl.BlockSpec((None, half, Dh), lambda i, b: (i, 0, 0)),
            pl.BlockSpec((None, half, Dh), lambda i, b: (i, 1, 0)),
            pl.BlockSpec((None, 8, CMP_LEN * Dh), lambda i, b: (i, 0, 0)),
            pl.BlockSpec((None, CMP_LEN * Dh, Dh), lambda i, b: (i, 0, 0)),
            pl.BlockSpec((None, Dh, Dh), lambda i, b: (i, 0, 0)),
        ],
        out_specs=pl.BlockSpec((None, None, n, Dh), lambda i, b: (i, b, 0, 0)),
        out_shape=jax.ShapeDtypeStruct((2, B, n, Dh), BF),
        compiler_params=_params(("parallel", "parallel")),
        name="nsa_compress",
    )(g2, w1, w1, pos8, w1, cmp_w2.astype(BF))


def _nsa_kernel(q_ref, gt_ref, kvc_ref, sel_ref, win_ref, exp_ref, o_ref, osel_ref, *, n_top, seg):
    Qb, H, Dh = NSA_QB, NSA_HEADS, NSA_DH
    S = sel_ref.shape[0]
    n_cmp = kvc_ref.shape[1]
    n_blk = S // SEL_LEN
    bi = pl.program_id(1)
    qpos = bi * Qb + lax.broadcasted_iota(jnp.int32, (Qb, 1), 0)
    q = q_ref[...]

    kc = kvc_ref[0]
    vc = kvc_ref[1]
    cidx = lax.broadcasted_iota(jnp.int32, (1, n_cmp), 1)
    vis = cidx * CMP_STRIDE + (CMP_LEN - 1) <= qpos
    o_cmp = []
    psum = jnp.zeros((Qb, n_cmp), F32)
    for h in range(H):
        s = jnp.where(vis, _dot_nt(q[:, h * Dh:(h + 1) * Dh], kc), NEG_INF)
        m = jnp.max(s, -1, keepdims=True)
        m = jnp.where(m > NEG_INF, m, 0.0)
        e = jnp.exp2(s - m)
        p = e / jnp.maximum(jnp.sum(e, -1, keepdims=True), 1e-30)
        psum = psum + p
        o_cmp.append(_dot(p.astype(BF), vc))

    js = lax.broadcasted_iota(jnp.int32, (n_blk, 1), 0) * SEL_LEN
    cs = lax.broadcasted_iota(jnp.int32, (1, n_cmp), 1) * CMP_STRIDE
    ov = jnp.maximum(jnp.minimum(cs + CMP_LEN, js + SEL_LEN) - jnp.maximum(cs, js), 0).astype(F32) / CMP_LEN
    ov = ov.astype(BF)
    imp = sum(_dot_nt(ov, t) for t in _split3(psum))
    blk = lax.broadcasted_iota(jnp.int32, (n_blk, 1), 0)
    sel_shift = SEL_LEN.bit_length() - 1
    cur = jnp.right_shift(bi * Qb + lax.broadcasted_iota(jnp.int32, (1, Qb), 1), sel_shift)
    forced = (blk == 0) | (blk == cur) | (blk == cur - 1)
    imp = jnp.where(blk <= cur, jnp.where(forced, jnp.inf, imp), NEG_INF)
    rank = jnp.zeros((n_blk, Qb), F32)
    for j in range(n_blk):
        row = imp[j:j + 1, :]
        rank = rank + jnp.where((row > imp) | ((row == imp) & (blk > j)), 1.0, 0.0)
    sub = lax.broadcasted_iota(jnp.int32, (LANE - n_blk, Qb), 0)
    chosen = jnp.concatenate([jnp.where(rank < n_top, SEL_BIG, 0.0), jnp.where(sub < 2, 1.0, 0.0)], axis=0)
    chosen = chosen.T.astype(BF)
    bound = SEL_BIG - 0.5 - qpos.astype(F32)

    def selected(e):
        mask = _dot(chosen, exp_ref[:, 0:e]) > bound
        osel_ref[...] = _masked_attention(q, sel_ref[0:e, :], mask, H, Dh)

    _for_causal_extent(S, seg, Qb, selected)
    o_sel = osel_ref[...]

    wlen = WINDOW + Qb
    start = pl.multiple_of(jnp.maximum(bi * Qb - WINDOW, 0), Qb)
    kwin = win_ref[pl.ds(start, wlen), :]
    dlt = qpos - (start + lax.broadcasted_iota(jnp.int32, (1, wlen), 1))
    o_win = _masked_attention(q, kwin, (dlt >= 0) & (dlt < WINDOW), H, Dh)

    g = jax.nn.sigmoid(gt_ref[:, :3 * H])
    outs = []
    for h in range(H):
        outs.append(g[:, 3 * h:3 * h + 1] * o_cmp[h]
                    + g[:, 3 * h + 1:3 * h + 2] * o_sel[:, h * Dh:(h + 1) * Dh]
                    + g[:, 3 * h + 2:3 * h + 3] * o_win[:, h * Dh:(h + 1) * Dh])
    o_ref[...] = jnp.concatenate(outs, axis=-1)


def _nsa(nsa, nsa_aux, kvc, B, S):
    nb = S // NSA_QB
    n_cmp = kvc.shape[2]
    n_blk = S // SEL_LEN
    n_top = min(SEL_TOPN, n_blk)
    assert S >= WINDOW + NSA_QB and n_blk + 2 <= LANE and S < SEL_BIG and n_blk <= 256
    kpos = np.arange(S)
    expand = np.zeros((LANE, S), np.float32)
    expand[:n_blk] = kpos[None, :] // SEL_LEN == np.arange(n_blk)[:, None]
    expand[n_blk] = -(kpos // SEL_LEN * SEL_LEN)
    expand[n_blk + 1] = -(kpos % SEL_LEN)
    expand = jnp.asarray(expand, BF)
    return pl.pallas_call(
        functools.partial(_nsa_kernel, n_top=n_top, seg=min(CAUSAL_SEG, S)),
        grid=(B, nb),
        in_specs=[
            pl.BlockSpec((NSA_QB, 2 * LANE), lambda b, i: (b * nb + i, 0)),
            pl.BlockSpec((NSA_QB, LANE), lambda b, i: (b * nb + i, 0)),
            pl.BlockSpec((2, None, n_cmp, NSA_DH), lambda b, i: (0, b, 0, 0)),
            pl.BlockSpec((S, LANE), lambda b, i: (b, 3)),
            pl.BlockSpec((S, LANE), lambda b, i: (b, 4)),
            pl.BlockSpec((LANE, S), lambda b, i: (0, 0)),
        ],
        out_specs=pl.BlockSpec((NSA_QB, BRANCH_W), lambda b, i: (b * nb + i, 0)),
        out_shape=jax.ShapeDtypeStruct((B * S, BRANCH_W), F32),
        scratch_shapes=[pltpu.VMEM((NSA_QB, BRANCH_W), F32)],
        compiler_params=_params(("parallel", "arbitrary")),
        name="nsa",
    )(nsa, nsa_aux, kvc, nsa, nsa, expand)


def _merge_kernel(x_ref, y0, y1, y2, y3, wg_ref, wb_ref, wo_ref, g_ref, b_ref, o_ref, *, alpha):
    x = x_ref[...]
    xb = x.astype(BF)
    D = x.shape[1]
    merged = jnp.zeros_like(x)
    for n, y_ref in enumerate((y0, y1, y2, y3)):
        gate = jax.nn.sigmoid(_dot(xb, wg_ref[:, n * D:(n + 1) * D]))
        merged = merged + gate * _dot(y_ref[...].astype(BF), wb_ref[n])
    o_ref[...] = _layer_norm(alpha * x + _dot(merged.astype(BF), wo_ref[...]), g_ref[...], b_ref[...])


def _merge(x, ys, w_gate, w_branch, w_out, layer, g, b, alpha, tm=512):
    T, D = x.shape
    full = lambda shape: pl.BlockSpec(shape, lambda i: (0,) * len(shape), pipeline_mode=pl.Buffered(1))
    return pl.pallas_call(
        functools.partial(_merge_kernel, alpha=alpha),
        grid=(T // tm,),
        in_specs=[pl.BlockSpec((tm, D), lambda i: (i, 0))]
        + [pl.BlockSpec((tm, BRANCH_W), lambda i: (i, 0))] * N_BRANCH
        + [full(w_gate.shape), _layer_spec(w_branch, layer), _layer_spec(w_out, layer), full((1, D)), full((1, D))],
        out_specs=pl.BlockSpec((tm, D), lambda i: (i, 0)),
        out_shape=jax.ShapeDtypeStruct((T, D), F32),
        compiler_params=_params(("parallel",)),
        name="merge",
    )(x, *ys, w_gate, w_branch, w_out, g, b)


def _matmul_kernel(a_ref, w_ref, o_ref):
    o_ref[...] = _dot(a_ref[...].astype(BF), w_ref[...]).astype(o_ref.dtype)


def _matmul(a, w, layer, tm):
    M, K = a.shape
    N = w.shape[2]
    return pl.pallas_call(
        _matmul_kernel,
        grid=(M // tm,),
        in_specs=[pl.BlockSpec((tm, K), lambda i: (i, 0)), _layer_spec(w, layer)],
        out_specs=pl.BlockSpec((tm, N), lambda i: (i, 0)),
        out_shape=jax.ShapeDtypeStruct((M, N), BF),
        compiler_params=_params(("parallel",)),
        name="kv_proj",
    )(a, w)


def _xattn_kernel(x_ref, kv_ref, wq_ref, wo_ref, g_ref, b_ref, o_ref, *, alpha):
    x = x_ref[...]
    D = x.shape[1]
    dh = D // X_HEADS
    q = _dot(x.astype(BF), wq_ref[...])
    outs = []
    for h in range(X_HEADS):
        k = kv_ref[:, h * dh:(h + 1) * dh]
        v = kv_ref[:, D + h * dh:D + (h + 1) * dh]
        s = _dot_nt(q[:, h * dh:(h + 1) * dh].astype(BF), k) * (dh ** -0.5)
        e = jnp.exp(s - jnp.max(s, -1, keepdims=True))
        outs.append(_dot(e.astype(BF), v) / jnp.sum(e, -1, keepdims=True))
    att = jnp.concatenate(outs, axis=-1).astype(BF)
    o_ref[...] = _layer_norm(alpha * x + _dot(att, wo_ref[...]), g_ref[...], b_ref[...])


def _xattn(x, kv, wq, wo, layer, g, b, alpha, S, M, tm=512):
    T, D = x.shape
    per = S // tm
    full = lambda shape: pl.BlockSpec(shape, lambda i: (0,) * len(shape))
    return pl.pallas_call(
        functools.partial(_xattn_kernel, alpha=alpha),
        grid=(T // tm,),
        in_specs=[pl.BlockSpec((tm, D), lambda i: (i, 0)),
                  pl.BlockSpec((M, 2 * D), lambda i: (i // per, 0)),
                  _layer_spec(wq, layer), _layer_spec(wo, layer), full((1, D)), full((1, D))],
        out_specs=pl.BlockSpec((tm, D), lambda i: (i, 0)),
        out_shape=jax.ShapeDtypeStruct((T, D), F32),
        compiler_params=_params(("parallel",)),
        name="xattn",
    )(x, kv, wq, wo, g, b)


def _rope_tables(S):
    pos = jnp.arange(S).astype(F32)

    def base(rot, theta):
        half = rot // 2
        inv = jnp.power(jnp.float32(theta), -2.0 * jnp.arange(half, dtype=F32) / rot)
        ang = pos[:, None] * inv[None, :]
        return jnp.cos(ang), jnp.sin(ang)

    def tile(hd, rot, theta, width, scale=1.0):
        c, s = base(rot, theta)
        lane = np.arange(LANE)
        jj = lane % hd
        half = rot // 2
        first = (jj < half) & (lane < width)
        second = (jj >= half) & (jj < rot) & (lane < width)
        idx = jj % half
        ct = jnp.where((first | second)[None, :], c[:, idx], 1.0) * scale
        s_hi = jnp.where(first[None, :], -s[:, idx], 0.0) * scale
        s_lo = jnp.where(second[None, :], s[:, idx], 0.0) * scale
        return ct, s_hi, s_lo

    q_scale = DSA_DH ** -0.5 * LOG2E
    tabs = [
        tile(RET_DK, RET_DK, RET_THETA, LANE),
        tile(RET_DK, RET_DK, RET_THETA, LANE, RET_DK ** -0.5),
        tile(DSA_DH, DSA_DH // ROPE_FRAC, ROPE_THETA, LANE, q_scale),
        tile(DSA_DH, DSA_DH // ROPE_FRAC, ROPE_THETA, DSA_DH),
        tile(IDX_DH, IDX_DH // ROPE_FRAC, ROPE_THETA, LANE),
        tile(IDX_DH, IDX_DH // ROPE_FRAC, ROPE_THETA, IDX_DH),
    ]
    return tuple(jnp.concatenate([t[i] for t in tabs], 1) for i in range(3))


def _pack_w_in(wb, layer):
    D = wb.shape[1]
    main = []
    for _, segs, width in _GROUPS:
        used = 0
        for s in segs:
            start, n = _SEG[s]
            main.append(wb[layer, :, start:start + n])
            used += n
        main.append(jnp.zeros((D, width - used), BF))
    return jnp.concatenate(main, 1), wb[layer, :, GATE_START:]


def kernel(x, mem, ln_g, ln_b, ffn1_w_gu, ffn1_w_down, w_in, cmp_w1, cmp_w2, cmp_pos, conv_w, conv_b,
           dt_bias, a_log, d_skip, ssm_norm_g, w_branch, w_out, xattn_wq, xattn_wkv, xattn_wo,
           ffn2_w_gu, ffn2_w_down):
    B, S, D = x.shape
    M = mem.shape[1]
    depth = ln_g.shape[0]
    alpha = (2 * depth) ** 0.25
    tables = _rope_tables(S)
    h = x
    mem2 = mem.reshape(B * M, D)
    bf = lambda w: w.astype(BF)
    ffn1_w_gu, ffn1_w_down, ffn2_w_gu, ffn2_w_down = bf(ffn1_w_gu), bf(ffn1_w_down), bf(ffn2_w_gu), bf(ffn2_w_down)
    w_in, w_branch, w_out = bf(w_in), bf(w_branch), bf(w_out)
    xattn_wq, xattn_wkv, xattn_wo = bf(xattn_wq), bf(xattn_wkv), bf(xattn_wo)
    for l in range(depth):
        lg = lambda i: ln_g[l, i][None, :]
        lb = lambda i: ln_b[l, i][None, :]
        h = _ffn(h, ffn1_w_gu, ffn1_w_down, l, lg(0), lb(0), alpha)

        wm, w_gate = _pack_w_in(w_in, l)
        ret, dsa, nsa, ssd, dsa_aux, nsa_aux = _inproj(h, wm, tables, S)
        y_ret = _retention(ret, B, S)
        y_dsa = _dsa(dsa, dsa_aux, B, S)
        g2 = jnp.stack([nsa[:, 2 * LANE:2 * LANE + NSA_DH], nsa[:, 2 * LANE + NSA_DH:3 * LANE]])
        kvc = _compress(g2.reshape(2, B, S // CMP_STRIDE, CMP_STRIDE * NSA_DH), cmp_w1[l], cmp_w2[l], cmp_pos[l])
        y_nsa = _nsa(nsa, nsa_aux, kvc, B, S)
        y_ssd = _ssd(ssd, conv_w[l], conv_b[l], dt_bias[l], a_log[l], d_skip[l], ssm_norm_g[l], B, S)
        h = _merge(h, (y_ret, y_dsa, y_nsa, y_ssd), w_gate, w_branch, w_out, l, lg(1), lb(1), alpha)

        kv = _matmul(mem2, xattn_wkv, l, tm=min(256, B * M))
        h = _xattn(h, kv, xattn_wq, xattn_wo, l, lg(2), lb(2), alpha, S, M)
        h = _ffn(h, ffn2_w_gu, ffn2_w_down, l, lg(3), lb(3), alpha, out_batch=B if l == depth - 1 else None)
    return h
```

```python
import functools
import math

import numpy as np
import jax
import jax.numpy as jnp
from jax import lax
from jax.experimental import pallas as pl
from jax.experimental.pallas import tpu as pltpu

F32 = jnp.float32
BF = jnp.bfloat16
NEG_INF = float("-inf")

DSA_QB = 128
NSA_QB = 256
ROPE_THETA = 500000.0
ROPE_FRAC = 4
LN_EPS = 1e-5
RET_HEADS, RET_DK, RET_DV, RET_CHUNK, RET_THETA = 4, 32, 64, 128, 10000.0
DSA_HEADS, DSA_DH, IDX_HEADS, IDX_DH, DSA_TOPK = 4, 64, 8, 32, 256
NSA_HEADS, NSA_DH, CMP_LEN, CMP_STRIDE, SEL_LEN, SEL_TOPN, WINDOW = 4, 64, 32, 16, 64, 16, 512
SSM_HEADS, SSM_HEADDIM, SSM_GROUPS, SSM_STATE, SSM_CONV, SSM_CHUNK = 4, 64, 2, 128, 4, 128
SSM_INNER = SSM_HEADS * SSM_HEADDIM
SSM_CONV_DIM = SSM_INNER + 2 * SSM_GROUPS * SSM_STATE
N_BRANCH = 4
BRANCH_W = 256
X_HEADS = 4

LANE = 128
SUBLANE = 8
CONV_PAD = SUBLANE
VMEM_LIMIT = 56 * 1024 * 1024

_SEG_NAMES = ("r_q", "r_k", "r_v", "r_g", "d_q", "d_k", "d_v", "i_q", "i_k", "i_w",
              "n_q", "n_kc", "n_vc", "n_ks", "n_vs", "n_kw", "n_vw", "n_g", "s_z", "s_xbc", "s_dt")
_SEG_WIDTHS = (RET_HEADS * RET_DK, RET_HEADS * RET_DK, RET_HEADS * RET_DV, RET_HEADS * RET_DV,
               DSA_HEADS * DSA_DH, DSA_DH, DSA_DH, IDX_HEADS * IDX_DH, IDX_DH, IDX_HEADS,
               NSA_HEADS * NSA_DH, NSA_DH, NSA_DH, NSA_DH, NSA_DH, NSA_DH, NSA_DH, NSA_HEADS * 3,
               SSM_INNER, SSM_CONV_DIM, SSM_HEADS)
_SEG = {}
_o = 0
for _n, _w in zip(_SEG_NAMES, _SEG_WIDTHS):
    _SEG[_n] = (_o, _w)
    _o += _w
GATE_START = _o

_GROUPS = (
    ("ret", ("r_q", "r_k", "r_v", "r_g"), 768),
    ("dsa", ("d_q", "i_q", "d_k", "d_v", "i_k", "i_w"), 768),
    ("nsa", ("n_q", "n_kc", "n_vc", "n_ks", "n_vs", "n_kw", "n_vw", "n_g"), 768),
    ("ssd", ("s_z", "s_xbc", "s_dt"), 1152),
)
_ROPE = {
    "r_q": (RET_DK, RET_DK, 0), "r_k": (RET_DK, RET_DK, 1),
    "d_q": (DSA_DH, DSA_DH // ROPE_FRAC, 2), "n_q": (NSA_DH, NSA_DH // ROPE_FRAC, 2),
    "d_k": (DSA_DH, DSA_DH // ROPE_FRAC, 3), "n_kc": (NSA_DH, NSA_DH // ROPE_FRAC, 3),
    "n_ks": (NSA_DH, NSA_DH // ROPE_FRAC, 3), "n_kw": (NSA_DH, NSA_DH // ROPE_FRAC, 3),
    "i_q": (IDX_DH, IDX_DH // ROPE_FRAC, 4), "i_k": (IDX_DH, IDX_DH // ROPE_FRAC, 5),
}
N_TABLES = 6
_TABLE_HALF = (RET_DK // 2, RET_DK // 2, DSA_DH // ROPE_FRAC // 2, DSA_DH // ROPE_FRAC // 2,
               IDX_DH // ROPE_FRAC // 2, IDX_DH // ROPE_FRAC // 2)
LOG2E = math.log2(math.e)


def _build_layout():
    tile_seg = []
    group_tiles = []
    for _, segs, width in _GROUPS:
        cols = []
        for s in segs:
            cols += [s] * _SEG[s][1]
        cols += [None] * (width - len(cols))
        tile_seg += [cols[t * LANE:(t + 1) * LANE] for t in range(width // LANE)]
        group_tiles.append(width // LANE)
    roped_tiles = [t for t, segs in enumerate(tile_seg) if any(s in _ROPE for s in segs)]
    tile_table = [_ROPE[tile_seg[t][0]][2] for t in roped_tiles]
    return roped_tiles, tile_table, group_tiles, len(tile_seg) * LANE


_ROPED_TILES, _TILE_TABLE, _GROUP_TILES, N_MAIN = _build_layout()


def _dot(a, b):
    return jnp.dot(a, b, preferred_element_type=F32)


def _dot_nt(a, b):
    return lax.dot_general(a, b, (((1,), (1,)), ((), ())), preferred_element_type=F32)


def _dot_tn(a, b):
    return lax.dot_general(a, b, (((0,), (0,)), ((), ())), preferred_element_type=F32)


def _split3(a):
    hi = a.astype(BF)
    r1 = a - hi.astype(F32)
    mid = r1.astype(BF)
    lo = (r1 - mid.astype(F32)).astype(BF)
    return hi, mid, lo


def _layer_norm(v, g, b):
    mu = jnp.mean(v, -1, keepdims=True)
    d = v - mu
    var = jnp.mean(d * d, -1, keepdims=True)
    return d * lax.rsqrt(var + LN_EPS) * g + b


def _silu(v):
    return v * jax.nn.sigmoid(v)


def _params(sem):
    return pltpu.CompilerParams(dimension_semantics=sem, vmem_limit_bytes=VMEM_LIMIT)


def _ffn_kernel(x_ref, wg_ref, wu_ref, wd_ref, g_ref, b_ref, o_ref, *, alpha):
    x = x_ref[...]
    xb = x.astype(BF)
    h = (_silu(_dot(xb, wg_ref[...])) * _dot(xb, wu_ref[...])).astype(BF)
    o_ref[...] = _layer_norm(alpha * x + 0.5 * _dot(h, wd_ref[...]), g_ref[...], b_ref[...])


def _layer_spec(w, layer, block=None, index=None):
    block = tuple(w.shape[1:]) if block is None else block
    index = (0,) * len(block) if index is None else index
    return pl.BlockSpec((None,) + block, lambda *_: (layer,) + index, pipeline_mode=pl.Buffered(1))


def _ffn(x, w_gu, w_down, layer, g, b, alpha, tm=256, out_batch=None):
    D = x.shape[-1]
    T = x.size // D
    F = w_down.shape[1]

    def rows(batch):
        if batch is None:
            return pl.BlockSpec((tm, D), lambda i: (i, 0)), (T, D)
        per = T // batch // tm
        return pl.BlockSpec((None, tm, D), lambda i: (i // per, i % per, 0)), (batch, T // batch, D)

    x_spec, _ = rows(x.shape[0] if x.ndim == 3 else None)
    o_spec, o_shape = rows(out_batch)
    return pl.pallas_call(
        functools.partial(_ffn_kernel, alpha=alpha),
        grid=(T // tm,),
        in_specs=[
            x_spec,
            _layer_spec(w_gu, layer, (D, F), (0, 0)),
            _layer_spec(w_gu, layer, (D, F), (0, 1)),
            _layer_spec(w_down, layer),
            pl.BlockSpec((1, D), lambda i: (0, 0)),
            pl.BlockSpec((1, D), lambda i: (0, 0)),
        ],
        out_specs=o_spec,
        out_shape=jax.ShapeDtypeStruct(o_shape, F32),
        compiler_params=_params(("parallel",)),
        name="ffn",
    )(x, w_gu, w_gu, w_down, g, b)


AUX_TILE = 5


def _inproj_kernel(x_ref, wm_ref, cos_ref, shi_ref, slo_ref, o_ret, o_dsa, o_nsa, o_ssd, o_dsa_aux, o_nsa_aux):
    xb = x_ref[...].astype(BF)
    outs = (o_ret, o_dsa, o_nsa, o_ssd)
    aux = (None, o_dsa_aux, o_nsa_aux, None)
    t0 = 0
    for o_ref, aux_ref, nt in zip(outs, aux, _GROUP_TILES):
        y = _dot(xb, wm_ref[:, t0 * LANE:(t0 + nt) * LANE])
        for t in range(nt):
            yt = y[:, t * LANE:(t + 1) * LANE]
            gt = t0 + t
            if gt in _ROPED_TILES:
                tab = _TILE_TABLE[_ROPED_TILES.index(gt)]
                half = _TABLE_HALF[tab]
                lanes = slice(tab * LANE, (tab + 1) * LANE)
                yt = (yt * cos_ref[:, lanes] + pltpu.roll(yt, LANE - half, 1) * shi_ref[:, lanes]
                      + pltpu.roll(yt, half, 1) * slo_ref[:, lanes])
            o_ref[:, t * LANE:(t + 1) * LANE] = yt.astype(o_ref.dtype)
            if aux_ref is not None and t == AUX_TILE:
                aux_ref[...] = yt
        t0 += nt


def _inproj(x, wm, tables, S, tm=256):
    T, D = x.shape
    ns = S // tm
    widths = [nt * LANE for nt in _GROUP_TILES] + [LANE, LANE]
    dtypes = [F32, BF, BF, F32, F32, F32]
    table_spec = pl.BlockSpec((tm, N_TABLES * LANE), lambda p, b: (p, 0))
    rows = lambda width: pl.BlockSpec((tm, width), lambda p, b: (b * ns + p, 0))
    return pl.pallas_call(
        _inproj_kernel,
        grid=(ns, T // S),
        in_specs=[
            rows(D),
            pl.BlockSpec((D, N_MAIN), lambda p, b: (0, 0), pipeline_mode=pl.Buffered(1)),
            table_spec, table_spec, table_spec,
        ],
        out_specs=[rows(w) for w in widths],
        out_shape=[jax.ShapeDtypeStruct((T, w), dt) for w, dt in zip(widths, dtypes)],
        compiler_params=_params(("parallel", "arbitrary")),
        name="in_proj",
    )(x, wm, *tables)


def _ret_kernel(r_ref, o_ref, st_ref):
    C = RET_CHUNK

    @pl.when(pl.program_id(0) == 0)
    def _():
        st_ref[...] = jnp.zeros_like(st_ref)

    hq = RET_HEADS * RET_DK
    rel = (lax.broadcasted_iota(jnp.int32, (C, C), 0) - lax.broadcasted_iota(jnp.int32, (C, C), 1)).astype(F32)
    row = lax.broadcasted_iota(jnp.int32, (C, 1), 0).astype(F32)
    head_consts = []
    for h in range(RET_HEADS):
        lg = math.log1p(-(2.0 ** (-5.0 - h)))
        head_consts.append((jnp.where(rel >= 0, jnp.exp(jnp.maximum(rel, 0.0) * lg), 0.0),
                            jnp.exp((C - 1 - row) * lg), jnp.exp((row + 1.0) * lg), math.exp(C * lg)))
    for b in range(r_ref.shape[0]):
        r = r_ref[b]
        q, k = r[:, :hq], r[:, hq:2 * hq]
        v = r[:, 2 * hq:2 * hq + RET_HEADS * RET_DV]
        g = r[:, 2 * hq + RET_HEADS * RET_DV:]
        outs = []
        for h, (decay, zeta, xi, chunk_decay) in enumerate(head_consts):
            qh = q[:, h * RET_DK:(h + 1) * RET_DK].astype(BF)
            kh = k[:, h * RET_DK:(h + 1) * RET_DK]
            vh = v[:, h * RET_DV:(h + 1) * RET_DV].astype(BF)
            sc = _dot_nt(qh, kh.astype(BF)) * decay
            intra = _dot(sc.astype(BF), vh)
            prev = st_ref[b * RET_HEADS + h]
            cross = _dot(qh, prev.astype(BF)) * xi
            st_ref[b * RET_HEADS + h] = prev * chunk_decay + _dot_tn((kh * zeta).astype(BF), vh)
            o = intra + cross
            mu = jnp.mean(o, -1, keepdims=True)
            d = o - mu
            var = jnp.mean(d * d, -1, keepdims=True)
            outs.append(d * lax.rsqrt(var + LN_EPS))
        o_ref[b] = _silu(g) * jnp.concatenate(outs, axis=-1)


def _retention(ret, B, S):
    n = S // RET_CHUNK
    W = ret.shape[1]
    return pl.pallas_call(
        _ret_kernel,
        grid=(n,),
        in_specs=[pl.BlockSpec((B, RET_CHUNK, W), lambda c: (0, c, 0))],
        out_specs=pl.BlockSpec((B, RET_CHUNK, BRANCH_W), lambda c: (0, c, 0)),
        out_shape=jax.ShapeDtypeStruct((B, S, BRANCH_W), F32),
        scratch_shapes=[pltpu.VMEM((B * RET_HEADS, RET_DK, RET_DV), F32)],
        compiler_params=_params(("arbitrary",)),
        name="retention",
    )(ret.reshape(B, S, W)).reshape(B * S, BRANCH_W)


def _softplus(v):
    return jnp.maximum(v, 0.0) + jnp.log1p(jnp.exp(-jnp.abs(v)))


def _ssd_kernel(s_ref, cw_ref, cb_ref, dtb_c_ref, a_c_ref, dtb_r_ref, a_r_ref, dsk_ref, ng_ref,
                o_ref, st_ref, xpad_ref):
    Q, H, P, N = SSM_CHUNK, SSM_HEADS, SSM_HEADDIM, SSM_STATE

    @pl.when(pl.program_id(0) == 0)
    def _():
        st_ref[...] = jnp.zeros_like(st_ref)
        xpad_ref[:, 0:CONV_PAD, :] = jnp.zeros((xpad_ref.shape[0], CONV_PAD, SSM_CONV_DIM), F32)

    ri = lax.broadcasted_iota(jnp.int32, (Q, Q), 0)
    ci = lax.broadcasted_iota(jnp.int32, (Q, Q), 1)
    tril = ri >= ci
    lo_tri = jnp.where(tril, 1.0, 0.0).astype(BF)
    up_tri = jnp.where(ci >= ri, 1.0, 0.0).astype(BF)
    gn = SSM_GROUPS * N
    gw = SSM_INNER // SSM_GROUPS
    for b in range(s_ref.shape[0]):
        z = s_ref[b, :, :SSM_INNER]
        xpad_ref[b, CONV_PAD:, :] = s_ref[b, :, SSM_INNER:SSM_INNER + SSM_CONV_DIM]
        dt_raw = s_ref[b, :, SSM_INNER + SSM_CONV_DIM:SSM_INNER + SSM_CONV_DIM + H]
        conv = cb_ref[...]
        for kk in range(SSM_CONV):
            off = CONV_PAD - (SSM_CONV - 1) + kk
            conv = conv + cw_ref[kk:kk + 1, :] * xpad_ref[b, off:off + Q, :]
        xpad_ref[b, 0:CONV_PAD, :] = xpad_ref[b, Q:Q + CONV_PAD, :]
        xc = _silu(conv)
        xs = xc[:, :SSM_INNER]
        bm = xc[:, SSM_INNER:SSM_INNER + gn]
        cm = xc[:, SSM_INNER + gn:]

        dt_c = _softplus(dt_raw + dtb_c_ref[...])
        adt_c = dt_c * a_c_ref[...]
        dt_tile = s_ref[b, :, SSM_INNER + SSM_CONV_DIM:SSM_INNER + SSM_CONV_DIM + LANE]
        dt_r = _softplus(dt_tile.T[0:CONV_PAD, :] + dtb_r_ref[...])
        adt_r = dt_r * a_r_ref[...]
        acs_c = sum(_dot(lo_tri, t) for t in _split3(adt_c))
        acs_r = sum(_dot(t, up_tri) for t in _split3(adt_r))

        outs = []
        for grp in range(SSM_GROUPS):
            bg = bm[:, grp * N:(grp + 1) * N]
            cg = cm[:, grp * N:(grp + 1) * N].astype(BF)
            cb = _dot_nt(cg, bg.astype(BF))
            for h in range(grp * (H // SSM_GROUPS), (grp + 1) * (H // SSM_GROUPS)):
                a_col = acs_c[:, h:h + 1]
                a_last = acs_c[Q - 1:Q, h:h + 1]
                lm = jnp.exp(jnp.where(tril, a_col - acs_r[h:h + 1, :], NEG_INF))
                xh = xs[:, h * P:(h + 1) * P]
                xdt = (xh * dt_c[:, h:h + 1]).astype(BF)
                y = _dot((cb * lm).astype(BF), xdt)
                prev = st_ref[b * H + h]
                y = y + _dot(cg, prev.astype(BF)) * jnp.exp(a_col)
                st_ref[b * H + h] = (prev * jnp.exp(a_last)
                                     + _dot_tn((bg * jnp.exp(a_last - a_col)).astype(BF), xdt))
                outs.append(y)
        y = jnp.concatenate(outs, axis=-1) + xs * dsk_ref[...]
        y = y * _silu(z)
        normed = []
        for grp in range(SSM_GROUPS):
            yg = y[:, grp * gw:(grp + 1) * gw]
            normed.append(yg * lax.rsqrt(jnp.mean(yg * yg, -1, keepdims=True) + LN_EPS))
        o_ref[b] = jnp.concatenate(normed, axis=-1) * ng_ref[...]


def _ssd(ssd, conv_w, conv_b, dt_bias, a_log, d_skip, norm_g, B, S):
    Q, H = SSM_CHUNK, SSM_HEADS
    n = S // Q
    W = ssd.shape[1]
    assert W >= SSM_INNER + SSM_CONV_DIM + LANE
    a = -jnp.exp(a_log.astype(F32))
    pad_r = lambda v: jnp.broadcast_to(jnp.pad(v, (0, CONV_PAD - H))[:, None], (CONV_PAD, Q)).astype(F32)
    full = lambda shape: pl.BlockSpec(shape, lambda c: (0,) * len(shape))
    return pl.pallas_call(
        _ssd_kernel,
        grid=(n,),
        in_specs=[
            pl.BlockSpec((B, Q, W), lambda c: (0, c, 0)),
            full((SSM_CONV, SSM_CONV_DIM)), full((1, SSM_CONV_DIM)),
            full((1, H)), full((1, H)), full((CONV_PAD, Q)), full((CONV_PAD, Q)),
            full((1, SSM_INNER)), full((1, SSM_INNER)),
        ],
        out_specs=pl.BlockSpec((B, Q, BRANCH_W), lambda c: (0, c, 0)),
        out_shape=jax.ShapeDtypeStruct((B, S, BRANCH_W), F32),
        scratch_shapes=[pltpu.VMEM((B * H, SSM_STATE, SSM_HEADDIM), F32),
                        pltpu.VMEM((B, Q + CONV_PAD, SSM_CONV_DIM), F32)],
        compiler_params=_params(("arbitrary",)),
        name="ssd",
    )(ssd.reshape(B, S, W), conv_w, conv_b[None, :], dt_bias[None, :], a[None, :], pad_r(dt_bias),
      pad_r(a), jnp.repeat(d_skip, SSM_HEADDIM)[None, :], norm_g[None, :]).reshape(B * S, BRANCH_W)


def _masked_attention(q_bf, kv_bf, mask, heads, dh):
    k_bf = kv_bf[:, :dh]
    ones_v = jnp.where(lax.broadcasted_iota(jnp.int32, kv_bf.shape, 1) < dh, jnp.ones((), kv_bf.dtype), kv_bf)
    outs = []
    for h in range(heads):
        s = jnp.where(mask, _dot_nt(q_bf[:, h * dh:(h + 1) * dh], k_bf), NEG_INF)
        m = jnp.max(s, -1, keepdims=True)
        m = jnp.where(m > NEG_INF, m, 0.0)
        lv = _dot(jnp.exp2((s - m).astype(BF)), ones_v)
        outs.append(lv[:, dh:] / jnp.maximum(lv[:, 0:1], 1e-30))
    return jnp.concatenate(outs, axis=-1)


def _count(m):
    return jnp.sum(jnp.where(m, 1.0, 0.0), -1, keepdims=True)


IDX_KEY_CHUNK = 256
SEL_BIG = float(2 ** 20)
CAUSAL_SEG = 1024
BISECT_PLAIN_STEPS = 26
BISECT_MAX_STEPS = 400


def _kth_threshold(load, nvalid, vmin, vmax, c_ge0, c_gt0, k):
    kf = float(k)
    short = nvalid <= kf
    up0 = c_ge0 >= kf
    tie0 = up0 & (c_gt0 < kf) & (c_ge0 > kf) & (nvalid > kf)
    lo0 = jnp.where(short, jnp.finfo(F32).min, jnp.where(up0, 0.0, vmin))
    hi0 = jnp.where(up0, vmax, 0.0)
    done0 = jnp.where(short | tie0 | (c_ge0 == kf), 1.0, 0.0)

    def split(lo, hi, done):
        mid = jnp.where(done > 0.5, lo, lo + (hi - lo) * 0.5)
        c = _count(load() >= mid)
        up = (c >= kf) & (done < 0.5)
        return jnp.where(up, mid, lo), jnp.where(up | (done > 0.5), hi, mid), jnp.where(up & (c == kf), 1.0, done)

    def plain_cond(st):
        return (st[0] < BISECT_PLAIN_STEPS) & (jnp.min(st[3]) < 0.5)

    def plain_step(st):
        it, lo, hi, done = st
        lo, hi, done = split(lo, hi, done)
        lo, hi, done = split(lo, hi, done)
        return it + 2, lo, hi, done

    _, lo, hi, done = lax.while_loop(plain_cond, plain_step, (jnp.int32(0), lo0, hi0, done0))

    def exact_cond(st):
        return (st[0] < BISECT_MAX_STEPS) & (jnp.min(st[4]) < 0.5)

    def exact_step(st):
        it, lo, hi, tie, done = st
        sc = load()
        cand = jnp.min(jnp.where(sc >= lo, sc, jnp.inf), -1, keepdims=True)
        fin = (_count(sc > cand) < kf) & (done < 0.5)
        lo = jnp.where(fin, cand, lo)
        tie = jnp.where(fin, 1.0, tie)
        done = jnp.where(fin, 1.0, done)
        lo, hi, done = split(lo, hi, done)
        return it + 1, lo, hi, tie, done

    _, thr, _, tie, _ = lax.while_loop(
        exact_cond, exact_step, (jnp.int32(0), lo, hi, jnp.where(tie0, 1.0, 0.0), done))
    return thr, tie


def _dsa_body(q_ref, aux_ref, kv_ref, ik_ref, o_ref, sc_ref, *, n_keep, E):
    Qb, CH = DSA_QB, IDX_KEY_CHUNK
    bi = pl.program_id(1)
    qpos = bi * Qb + lax.broadcasted_iota(jnp.int32, (Qb, 1), 0)
    hq = DSA_HEADS * DSA_DH
    iw = aux_ref[:, IDX_DH:IDX_DH + IDX_HEADS] * (IDX_HEADS ** -0.5) * (IDX_DH ** -0.5)
    iq = [q_ref[:, hq + h * IDX_DH:hq + (h + 1) * IDX_DH] for h in range(IDX_HEADS)]
    fold = lambda f, a: functools.reduce(f, [a[:, j:j + LANE] for j in range(0, CH, LANE)])
    hi_acc = jnp.full((Qb, LANE), NEG_INF, F32)
    lo_acc = jnp.full((Qb, LANE), jnp.inf, F32)
    for c0 in range(0, E, CH):
        ik = ik_ref[c0:c0 + CH, :IDX_DH]
        acc = jnp.maximum(_dot_nt(iq[0], ik), 0.0) * iw[:, 0:1]
        for h in range(1, IDX_HEADS):
            acc = acc + jnp.maximum(_dot_nt(iq[h], ik), 0.0) * iw[:, h:h + 1]
        causal = c0 + lax.broadcasted_iota(jnp.int32, (1, CH), 1) <= qpos
        masked = jnp.where(causal, acc, NEG_INF)
        sc_ref[:, c0:c0 + CH] = masked
        hi_acc = jnp.maximum(hi_acc, fold(jnp.maximum, masked))
        lo_acc = jnp.minimum(lo_acc, fold(jnp.minimum, jnp.where(causal, acc, jnp.inf)))
    load = lambda: sc_ref[:, 0:E]
    thr, tie = _kth_threshold(load, (qpos + 1).astype(F32), jnp.min(lo_acc, -1, keepdims=True),
                              jnp.max(hi_acc, -1, keepdims=True), _count(load() >= 0.0), _count(load() > 0.0),
                              n_keep)

    @pl.when(jnp.max(tie) > 0.5)
    def resolve_ties():
        sc = load()
        gt = sc > thr
        eqf = jnp.where(sc == thr, 1.0, 0.0)
        need = n_keep - _count(gt)
        ch = 256 if E % 256 == 0 else LANE
        before = jnp.where(lax.broadcasted_iota(jnp.int32, (ch, ch), 0)
                           < lax.broadcasted_iota(jnp.int32, (ch, ch), 1), 1.0, 0.0).astype(BF)
        run = jnp.zeros((Qb, 1), F32)
        take = []
        for c0 in range(0, E, ch):
            eqc = eqf[:, c0:c0 + ch]
            prefix = _dot(eqc.astype(BF), before) + run
            take.append(jnp.where(prefix < need, eqc, 0.0))
            run = run + jnp.sum(eqc, -1, keepdims=True)
        sc_ref[:, 0:E] = jnp.where(gt | (jnp.concatenate(take, axis=-1) > 0.5), jnp.inf, NEG_INF)

    o_ref[...] = _masked_attention(q_ref[:, :hq], kv_ref[0:E, :], load() >= thr, DSA_HEADS, DSA_DH)


def _for_causal_extent(S, seg, qb, body):
    bi = pl.program_id(1)
    assert S % seg == 0 and seg % qb == 0
    for e in range(seg, S + 1, seg):
        pl.when((bi >= (e - seg) // qb) & (bi < e // qb))(functools.partial(body, e))


def _dsa_kernel(q_ref, aux_ref, kv_ref, ik_ref, o_ref, sc_ref, *, n_keep, seg):
    _for_causal_extent(kv_ref.shape[0], seg, DSA_QB,
                       lambda e: _dsa_body(q_ref, aux_ref, kv_ref, ik_ref, o_ref, sc_ref, n_keep=n_keep, E=e))


def _dsa(dsa, dsa_aux, B, S):
    nb = S // DSA_QB
    n_keep = min(DSA_TOPK, S // 4)
    return pl.pallas_call(
        functools.partial(_dsa_kernel, n_keep=n_keep, seg=min(CAUSAL_SEG, S)),
        grid=(B, nb),
        in_specs=[
            pl.BlockSpec((DSA_QB, 4 * LANE), lambda b, i: (b * nb + i, 0)),
            pl.BlockSpec((DSA_QB, LANE), lambda b, i: (b * nb + i, 0)),
            pl.BlockSpec((S, LANE), lambda b, i: (b, 4)),
            pl.BlockSpec((S, LANE), lambda b, i: (b, 5)),
        ],
        out_specs=pl.BlockSpec((DSA_QB, BRANCH_W), lambda b, i: (b * nb + i, 0)),
        out_shape=jax.ShapeDtypeStruct((B * S, BRANCH_W), F32),
        scratch_shapes=[pltpu.VMEM((DSA_QB, S), F32)],
        compiler_params=_params(("parallel", "arbitrary")),
        name="dsa",
    )(dsa, dsa_aux, dsa, dsa)


def _cmp_kernel(g_ref, w1a_ref, w1b_ref, pos_ref, w1_ref, w2_ref, o_ref):
    g = g_ref[...].astype(BF)
    n = g.shape[0]
    a = _dot(g, w1a_ref[...])
    b = _dot(g, w1b_ref[...])
    posterm = _dot(pos_ref[...].astype(BF), w1_ref[...])[0:1, :]
    h = jax.nn.gelu(a + pltpu.roll(b, n - 1, 0) + posterm)
    o_ref[...] = _dot(h.astype(BF), w2_ref[...]).astype(o_ref.dtype)


def _compress(g2, cmp_w1, cmp_w2, cmp_pos):
    _, B, n, W = g2.shape
    Dh = NSA_DH
    half = W
    w1 = cmp_w1.astype(BF)
    pos8 = jnp.broadcast_to(cmp_pos.reshape(2, 1, CMP_LEN * Dh), (2, 8, CMP_LEN * Dh))
    return pl.pallas_call(
        _cmp_kernel,
        grid=(2, B),
        in_specs=[
            pl.BlockSpec((None, None, n, W), lambda i, b: (i, b, 0, 0)),
            pl.BlockSpec((None, half, Dh), lambda i, b: (i, 0, 0)),
            pl.BlockSpec((None, half, Dh), lambda i, b: (i, 1, 0)),
            pl.BlockSpec((None, 8, CMP_LEN * Dh), lambda i, b: (i, 0, 0)),
            pl.BlockSpec((None, CMP_LEN * Dh, Dh), lambda i, b: (i, 0, 0)),
            pl.BlockSpec((None, Dh, Dh), lambda i, b: (i, 0, 0)),
        ],
        out_specs=pl.BlockSpec((None, None, n, Dh), lambda i, b: (i, b, 0, 0)),
        out_shape=jax.ShapeDtypeStruct((2, B, n, Dh), BF),
        compiler_params=_params(("parallel", "parallel")),
        name="nsa_compress",
    )(g2, w1, w1, pos8, w1, cmp_w2.astype(BF))


def _nsa_kernel(q_ref, gt_ref, kvc_ref, sel_ref, win_ref, exp_ref, o_ref, osel_ref, *, n_top, seg):
    Qb, H, Dh = NSA_QB, NSA_HEADS, NSA_DH
    S = sel_ref.shape[0]
    n_cmp = kvc_ref.shape[1]
    n_blk = S // SEL_LEN
    bi = pl.program_id(1)
    qpos = bi * Qb + lax.broadcasted_iota(jnp.int32, (Qb, 1), 0)
    q = q_ref[...]

    kc = kvc_ref[0]
    vc = kvc_ref[1]
    cidx = lax.broadcasted_iota(jnp.int32, (1, n_cmp), 1)
    vis = cidx * CMP_STRIDE + (CMP_LEN - 1) <= qpos
    o_cmp = []
    psum = jnp.zeros((Qb, n_cmp), F32)
    for h in range(H):
        s = jnp.where(vis, _dot_nt(q[:, h * Dh:(h + 1) * Dh], kc), NEG_INF)
        m = jnp.max(s, -1, keepdims=True)
        m = jnp.where(m > NEG_INF, m, 0.0)
        e = jnp.exp2(s - m)
        p = e / jnp.maximum(jnp.sum(e, -1, keepdims=True), 1e-30)
        psum = psum + p
        o_cmp.append(_dot(p.astype(BF), vc))

    js = lax.broadcasted_iota(jnp.int32, (n_blk, 1), 0) * SEL_LEN
    cs = lax.broadcasted_iota(jnp.int32, (1, n_cmp), 1) * CMP_STRIDE
    ov = jnp.maximum(jnp.minimum(cs + CMP_LEN, js + SEL_LEN) - jnp.maximum(cs, js), 0).astype(F32) / CMP_LEN
    ov = ov.astype(BF)
    imp = sum(_dot_nt(ov, t) for t in _split3(psum))
    blk = lax.broadcasted_iota(jnp.int32, (n_blk, 1), 0)
    sel_shift = SEL_LEN.bit_length() - 1
    cur = jnp.right_shift(bi * Qb + lax.broadcasted_iota(jnp.int32, (1, Qb), 1), sel_shift)
    forced = (blk == 0) | (blk == cur) | (blk == cur - 1)
    imp = jnp.where(blk <= cur, jnp.where(forced, jnp.inf, imp), NEG_INF)
    rank = jnp.zeros((n_blk, Qb), F32)
    for j in range(n_blk):
        row = imp[j:j + 1, :]
        rank = rank + jnp.where((row > imp) | ((row == imp) & (blk > j)), 1.0, 0.0)
    sub = lax.broadcasted_iota(jnp.int32, (LANE - n_blk, Qb), 0)
    chosen = jnp.concatenate([jnp.where(rank < n_top, SEL_BIG, 0.0), jnp.where(sub < 2, 1.0, 0.0)], axis=0)
    chosen = chosen.T.astype(BF)
    bound = SEL_BIG - 0.5 - qpos.astype(F32)

    def selected(e):
        mask = _dot(chosen, exp_ref[:, 0:e]) > bound
        osel_ref[...] = _masked_attention(q, sel_ref[0:e, :], mask, H, Dh)

    _for_causal_extent(S, seg, Qb, selected)
    o_sel = osel_ref[...]

    wlen = WINDOW + Qb
    start = pl.multiple_of(jnp.maximum(bi * Qb - WINDOW, 0), Qb)
    kwin = win_ref[pl.ds(start, wlen), :]
    dlt = qpos - (start + lax.broadcasted_iota(jnp.int32, (1, wlen), 1))
    o_win = _masked_attention(q, kwin, (dlt >= 0) & (dlt < WINDOW), H, Dh)

    g = jax.nn.sigmoid(gt_ref[:, :3 * H])
    outs = []
    for h in range(H):
        outs.append(g[:, 3 * h:3 * h + 1] * o_cmp[h]
                    + g[:, 3 * h + 1:3 * h + 2] * o_sel[:, h * Dh:(h + 1) * Dh]
                    + g[:, 3 * h + 2:3 * h + 3] * o_win[:, h * Dh:(h + 1) * Dh])
    o_ref[...] = jnp.concatenate(outs, axis=-1)


def _nsa(nsa, nsa_aux, kvc, B, S):
    nb = S // NSA_QB
    n_cmp = kvc.shape[2]
    n_blk = S // SEL_LEN
    n_top = min(SEL_TOPN, n_blk)
    assert S >= WINDOW + NSA_QB and n_blk + 2 <= LANE and S < SEL_BIG and n_blk <= 256
    kpos = np.arange(S)
    expand = np.zeros((LANE, S), np.float32)
    expand[:n_blk] = kpos[None, :] // SEL_LEN == np.arange(n_blk)[:, None]
    expand[n_blk] = -(kpos // SEL_LEN * SEL_LEN)
    expand[n_blk + 1] = -(kpos % SEL_LEN)
    expand = jnp.asarray(expand, BF)
    return pl.pallas_call(
        functools.partial(_nsa_kernel, n_top=n_top, seg=min(CAUSAL_SEG, S)),
        grid=(B, nb),
        in_specs=[
            pl.BlockSpec((NSA_QB, 2 * LANE), lambda b, i: (b * nb + i, 0)),
            pl.BlockSpec((NSA_QB, LANE), lambda b, i: (b * nb + i, 0)),
            pl.BlockSpec((2, None, n_cmp, NSA_DH), lambda b, i: (0, b, 0, 0)),
            pl.BlockSpec((S, LANE), lambda b, i: (b, 3)),
            pl.BlockSpec((S, LANE), lambda b, i: (b, 4)),
            pl.BlockSpec((LANE, S), lambda b, i: (0, 0)),
        ],
        out_specs=pl.BlockSpec((NSA_QB, BRANCH_W), lambda b, i: (b * nb + i, 0)),
        out_shape=jax.ShapeDtypeStruct((B * S, BRANCH_W), F32),
        scratch_shapes=[pltpu.VMEM((NSA_QB, BRANCH_W), F32)],
        compiler_params=_params(("parallel", "arbitrary")),
        name="nsa",
    )(nsa, nsa_aux, kvc, nsa, nsa, expand)


def _merge_kernel(x_ref, y0, y1, y2, y3, wg_ref, wb_ref, wo_ref, g_ref, b_ref, o_ref, *, alpha):
    x = x_ref[...]
    xb = x.astype(BF)
    D = x.shape[1]
    merged = jnp.zeros_like(x)
    for n, y_ref in enumerate((y0, y1, y2, y3)):
        gate = jax.nn.sigmoid(_dot(xb, wg_ref[:, n * D:(n + 1) * D]))
        merged = merged + gate * _dot(y_ref[...].astype(BF), wb_ref[n])
    o_ref[...] = _layer_norm(alpha * x + _dot(merged.astype(BF), wo_ref[...]), g_ref[...], b_ref[...])


def _merge(x, ys, w_gate, w_branch, w_out, layer, g, b, alpha, tm=512):
    T, D = x.shape
    full = lambda shape: pl.BlockSpec(shape, lambda i: (0,) * len(shape), pipeline_mode=pl.Buffered(1))
    return pl.pallas_call(
        functools.partial(_merge_kernel, alpha=alpha),
        grid=(T // tm,),
        in_specs=[pl.BlockSpec((tm, D), lambda i: (i, 0))]
        + [pl.BlockSpec((tm, BRANCH_W), lambda i: (i, 0))] * N_BRANCH
        + [full(w_gate.shape), _layer_spec(w_branch, layer), _layer_spec(w_out, layer), full((1, D)), full((1, D))],
        out_specs=pl.BlockSpec((tm, D), lambda i: (i, 0)),
        out_shape=jax.ShapeDtypeStruct((T, D), F32),
        compiler_params=_params(("parallel",)),
        name="merge",
    )(x, *ys, w_gate, w_branch, w_out, g, b)


def _matmul_kernel(a_ref, w_ref, o_ref):
    o_ref[...] = _dot(a_ref[...].astype(BF), w_ref[...]).astype(o_ref.dtype)


def _matmul(a, w, layer, tm):
    M, K = a.shape
    N = w.shape[2]
    return pl.pallas_call(
        _matmul_kernel,
        grid=(M // tm,),
        in_specs=[pl.BlockSpec((tm, K), lambda i: (i, 0)), _layer_spec(w, layer)],
        out_specs=pl.BlockSpec((tm, N), lambda i: (i, 0)),
        out_shape=jax.ShapeDtypeStruct((M, N), BF),
        compiler_params=_params(("parallel",)),
        name="kv_proj",
    )(a, w)


def _xattn_kernel(x_ref, kv_ref, wq_ref, wo_ref, g_ref, b_ref, o_ref, *, alpha):
    x = x_ref[...]
    D = x.shape[1]
    dh = D // X_HEADS
    q = _dot(x.astype(BF), wq_ref[...])
    outs = []
    for h in range(X_HEADS):
        k = kv_ref[:, h * dh:(h + 1) * dh]
        v = kv_ref[:, D + h * dh:D + (h + 1) * dh]
        s = _dot_nt(q[:, h * dh:(h + 1) * dh].astype(BF), k) * (dh ** -0.5)
        e = jnp.exp(s - jnp.max(s, -1, keepdims=True))
        outs.append(_dot(e.astype(BF), v) / jnp.sum(e, -1, keepdims=True))
    att = jnp.concatenate(outs, axis=-1).astype(BF)
    o_ref[...] = _layer_norm(alpha * x + _dot(att, wo_ref[...]), g_ref[...], b_ref[...])


def _xattn(x, kv, wq, wo, layer, g, b, alpha, S, M, tm=512):
    T, D = x.shape
    per = S // tm
    full = lambda shape: pl.BlockSpec(shape, lambda i: (0,) * len(shape))
    return pl.pallas_call(
        functools.partial(_xattn_kernel, alpha=alpha),
        grid=(T // tm,),
        in_specs=[pl.BlockSpec((tm, D), lambda i: (i, 0)),
                  pl.BlockSpec((M, 2 * D), lambda i: (i // per, 0)),
                  _layer_spec(wq, layer), _layer_spec(wo, layer), full((1, D)), full((1, D))],
        out_specs=pl.BlockSpec((tm, D), lambda i: (i, 0)),
        out_shape=jax.ShapeDtypeStruct((T, D), F32),
        compiler_params=_params(("parallel",)),
        name="xattn",
    )(x, kv, wq, wo, g, b)


def _rope_tables(S):
    pos = jnp.arange(S).astype(F32)

    def base(rot, theta):
        half = rot // 2
        inv = jnp.power(jnp.float32(theta), -2.0 * jnp.arange(half, dtype=F32) / rot)
        ang = pos[:, None] * inv[None, :]
        return jnp.cos(ang), jnp.sin(ang)

    def tile(hd, rot, theta, width, scale=1.0):
        c, s = base(rot, theta)
        lane = np.arange(LANE)
        jj = lane % hd
        half = rot // 2
        first = (jj < half) & (lane < width)
        second = (jj >= half) & (jj < rot) & (lane < width)
        idx = jj % half
        ct = jnp.where((first | second)[None, :], c[:, idx], 1.0) * scale
        s_hi = jnp.where(first[None, :], -s[:, idx], 0.0) * scale
        s_lo = jnp.where(second[None, :], s[:, idx], 0.0) * scale
        return ct, s_hi, s_lo

    q_scale = DSA_DH ** -0.5 * LOG2E
    tabs = [
        tile(RET_DK, RET_DK, RET_THETA, LANE),
        tile(RET_DK, RET_DK, RET_THETA, LANE, RET_DK ** -0.5),
        tile(DSA_DH, DSA_DH // ROPE_FRAC, ROPE_THETA, LANE, q_scale),
        tile(DSA_DH, DSA_DH // ROPE_FRAC, ROPE_THETA, DSA_DH),
        tile(IDX_DH, IDX_DH // ROPE_FRAC, ROPE_THETA, LANE),
        tile(IDX_DH, IDX_DH // ROPE_FRAC, ROPE_THETA, IDX_DH),
    ]
    return tuple(jnp.concatenate([t[i] for t in tabs], 1) for i in range(3))


def _pack_w_in(wb, layer):
    D = wb.shape[1]
    main = []
    for _, segs, width in _GROUPS:
        used = 0
        for s in segs:
            start, n = _SEG[s]
            main.append(wb[layer, :, start:start + n])
            used += n
        main.append(jnp.zeros((D, width - used), BF))
    return jnp.concatenate(main, 1), wb[layer, :, GATE_START:]


def kernel(x, mem, ln_g, ln_b, ffn1_w_gu, ffn1_w_down, w_in, cmp_w1, cmp_w2, cmp_pos, conv_w, conv_b,
           dt_bias, a_log, d_skip, ssm_norm_g, w_branch, w_out, xattn_wq, xattn_wkv, xattn_wo,
           ffn2_w_gu, ffn2_w_down):
    B, S, D = x.shape
    M = mem.shape[1]
    depth = ln_g.shape[0]
    alpha = (2 * depth) ** 0.25
    tables = _rope_tables(S)
    h = x
    mem2 = mem.reshape(B * M, D)
    bf = lambda w: w.astype(BF)
    ffn1_w_gu, ffn1_w_down, ffn2_w_gu, ffn2_w_down = bf(ffn1_w_gu), bf(ffn1_w_down), bf(ffn2_w_gu), bf(ffn2_w_down)
    w_in, w_branch, w_out = bf(w_in), bf(w_branch), bf(w_out)
    xattn_wq, xattn_wkv, xattn_wo = bf(xattn_wq), bf(xattn_wkv), bf(xattn_wo)
    for l in range(depth):
        lg = lambda i: ln_g[l, i][None, :]
        lb = lambda i: ln_b[l, i][None, :]
        h = _ffn(h, ffn1_w_gu, ffn1_w_down, l, lg(0), lb(0), alpha)

        wm, w_gate = _pack_w_in(w_in, l)
        ret, dsa, nsa, ssd, dsa_aux, nsa_aux = _inproj(h, wm, tables, S)
        y_ret = _retention(ret, B, S)
        y_dsa = _dsa(dsa, dsa_aux, B, S)
        g2 = jnp.stack([nsa[:, 2 * LANE:2 * LANE + NSA_DH], nsa[:, 2 * LANE + NSA_DH:3 * LANE]])
        kvc = _compress(g2.reshape(2, B, S // CMP_STRIDE, CMP_STRIDE * NSA_DH), cmp_w1[l], cmp_w2[l], cmp_pos[l])
        y_nsa = _nsa(nsa, nsa_aux, kvc, B, S)
        y_ssd = _ssd(ssd, conv_w[l], conv_b[l], dt_bias[l], a_log[l], d_skip[l], ssm_norm_g[l], B, S)
        h = _merge(h, (y_ret, y_dsa, y_nsa, y_ssd), w_gate, w_branch, w_out, l, lg(1), lb(1), alpha)

        kv = _matmul(mem2, xattn_wkv, l, tm=min(256, B * M))
        h = _xattn(h, kv, xattn_wq, xattn_wo, l, lg(2), lb(2), alpha, S, M)
        h = _ffn(h, ffn2_w_gu, ffn2_w_down, l, lg(3), lb(3), alpha, out_batch=B if l == depth - 1 else None)
    return h
```

```python
import functools
import math

import numpy as np
import jax
import jax.numpy as jnp
from jax import lax
from jax.experimental import pallas as pl
from jax.experimental.pallas import tpu as pltpu

F32 = jnp.float32
BF = jnp.bfloat16
NEG_INF = float("-inf")

DSA_QB = 128
NSA_QB = 256
ROPE_THETA = 500000.0
ROPE_FRAC = 4
LN_EPS = 1e-5
RET_HEADS, RET_DK, RET_DV, RET_CHUNK, RET_THETA = 4, 32, 64, 128, 10000.0
DSA_HEADS, DSA_DH, IDX_HEADS, IDX_DH, DSA_TOPK = 4, 64, 8, 32, 256
NSA_HEADS, NSA_DH, CMP_LEN, CMP_STRIDE, SEL_LEN, SEL_TOPN, WINDOW = 4, 64, 32, 16, 64, 16, 512
SSM_HEADS, SSM_HEADDIM, SSM_GROUPS, SSM_STATE, SSM_CONV, SSM_CHUNK = 4, 64, 2, 128, 4, 128
SSM_INNER = SSM_HEADS * SSM_HEADDIM
SSM_CONV_DIM = SSM_INNER + 2 * SSM_GROUPS * SSM_STATE
N_BRANCH = 4
BRANCH_W = 256
X_HEADS = 4

LANE = 128
SUBLANE = 8
CONV_PAD = SUBLANE
VMEM_LIMIT = 56 * 1024 * 1024

_SEG_NAMES = ("r_q", "r_k", "r_v", "r_g", "d_q", "d_k", "d_v", "i_q", "i_k", "i_w",
              "n_q", "n_kc", "n_vc", "n_ks", "n_vs", "n_kw", "n_vw", "n_g", "s_z", "s_xbc", "s_dt")
_SEG_WIDTHS = (RET_HEADS * RET_DK, RET_HEADS * RET_DK, RET_HEADS * RET_DV, RET_HEADS * RET_DV,
               DSA_HEADS * DSA_DH, DSA_DH, DSA_DH, IDX_HEADS * IDX_DH, IDX_DH, IDX_HEADS,
               NSA_HEADS * NSA_DH, NSA_DH, NSA_DH, NSA_DH, NSA_DH, NSA_DH, NSA_DH, NSA_HEADS * 3,
               SSM_INNER, SSM_CONV_DIM, SSM_HEADS)
_SEG = {}
_o = 0
for _n, _w in zip(_SEG_NAMES, _SEG_WIDTHS):
    _SEG[_n] = (_o, _w)
    _o += _w
GATE_START = _o

_GROUPS = (
    ("ret", ("r_q", "r_k", "r_v", "r_g"), 768),
    ("dsa", ("d_q", "i_q", "d_k", "d_v", "i_k", "i_w"), 768),
    ("nsa", ("n_q", "n_kc", "n_vc", "n_ks", "n_vs", "n_kw", "n_vw", "n_g"), 768),
    ("ssd", ("s_z", "s_xbc", "s_dt"), 1152),
)
_ROPE = {
    "r_q": (RET_DK, RET_DK, 0), "r_k": (RET_DK, RET_DK, 1),
    "d_q": (DSA_DH, DSA_DH // ROPE_FRAC, 2), "n_q": (NSA_DH, NSA_DH // ROPE_FRAC, 2),
    "d_k": (DSA_DH, DSA_DH // ROPE_FRAC, 3), "n_kc": (NSA_DH, NSA_DH // ROPE_FRAC, 3),
    "n_ks": (NSA_DH, NSA_DH // ROPE_FRAC, 3), "n_kw": (NSA_DH, NSA_DH // ROPE_FRAC, 3),
    "i_q": (IDX_DH, IDX_DH // ROPE_FRAC, 4), "i_k": (IDX_DH, IDX_DH // ROPE_FRAC, 5),
}
N_TABLES = 6
_TABLE_HALF = (RET_DK // 2, RET_DK // 2, DSA_DH // ROPE_FRAC // 2, DSA_DH // ROPE_FRAC // 2,
               IDX_DH // ROPE_FRAC // 2, IDX_DH // ROPE_FRAC // 2)
LOG2E = math.log2(math.e)


def _build_layout():
    tile_seg = []
    group_tiles = []
    for _, segs, width in _GROUPS:
        cols = []
        for s in segs:
            cols += [s] * _SEG[s][1]
        cols += [None] * (width - len(cols))
        tile_seg += [cols[t * LANE:(t + 1) * LANE] for t in range(width // LANE)]
        group_tiles.append(width // LANE)
    roped_tiles = [t for t, segs in enumerate(tile_seg) if any(s in _ROPE for s in segs)]
    tile_table = [_ROPE[tile_seg[t][0]][2] for t in roped_tiles]
    return roped_tiles, tile_table, group_tiles, len(tile_seg) * LANE


_ROPED_TILES, _TILE_TABLE, _GROUP_TILES, N_MAIN = _build_layout()


def _dot(a, b):
    return jnp.dot(a, b, preferred_element_type=F32)


def _dot_nt(a, b):
    return lax.dot_general(a, b, (((1,), (1,)), ((), ())), preferred_element_type=F32)


def _dot_tn(a, b):
    return lax.dot_general(a, b, (((0,), (0,)), ((), ())), preferred_element_type=F32)


def _split3(a):
    hi = a.astype(BF)
    r1 = a - hi.astype(F32)
    mid = r1.astype(BF)
    lo = (r1 - mid.astype(F32)).astype(BF)
    return hi, mid, lo


def _layer_norm(v, g, b):
    mu = jnp.mean(v, -1, keepdims=True)
    d = v - mu
    var = jnp.mean(d * d, -1, keepdims=True)
    return d * lax.rsqrt(var + LN_EPS) * g + b


def _silu(v):
    return v * jax.nn.sigmoid(v)


def _params(sem):
    return pltpu.CompilerParams(dimension_semantics=sem, vmem_limit_bytes=VMEM_LIMIT)


def _ffn_kernel(x_ref, wg_ref, wu_ref, wd_ref, g_ref, b_ref, o_ref, *, alpha):
    x = x_ref[...]
    xb = x.astype(BF)
    h = (_silu(_dot(xb, wg_ref[...])) * _dot(xb, wu_ref[...])).astype(BF)
    o_ref[...] = _layer_norm(alpha * x + 0.5 * _dot(h, wd_ref[...]), g_ref[...], b_ref[...])


def _layer_spec(w, layer, block=None, index=None):
    block = tuple(w.shape[1:]) if block is None else block
    index = (0,) * len(block) if index is None else index
    return pl.BlockSpec((None,) + block, lambda *_: (layer,) + index, pipeline_mode=pl.Buffered(1))


def _ffn(x, w_gu, w_down, layer, g, b, alpha, tm=256, out_batch=None):
    D = x.shape[-1]
    T = x.size // D
    F = w_down.shape[1]

    def rows(batch):
        if batch is None:
            return pl.BlockSpec((tm, D), lambda i: (i, 0)), (T, D)
        per = T // batch // tm
        return pl.BlockSpec((None, tm, D), lambda i: (i // per, i % per, 0)), (batch, T // batch, D)

    x_spec, _ = rows(x.shape[0] if x.ndim == 3 else None)
    o_spec, o_shape = rows(out_batch)
    return pl.pallas_call(
        functools.partial(_ffn_kernel, alpha=alpha),
        grid=(T // tm,),
        in_specs=[
            x_spec,
            _layer_spec(w_gu, layer, (D, F), (0, 0)),
            _layer_spec(w_gu, layer, (D, F), (0, 1)),
            _layer_spec(w_down, layer),
            pl.BlockSpec((1, D), lambda i: (0, 0)),
            pl.BlockSpec((1, D), lambda i: (0, 0)),
        ],
        out_specs=o_spec,
        out_shape=jax.ShapeDtypeStruct(o_shape, F32),
        compiler_params=_params(("parallel",)),
        name="ffn",
    )(x, w_gu, w_gu, w_down, g, b)


AUX_TILE = 5


def _inproj_kernel(x_ref, wm_ref, cos_ref, shi_ref, slo_ref, o_ret, o_dsa, o_nsa, o_ssd, o_dsa_aux, o_nsa_aux):
    xb = x_ref[...].astype(BF)
    outs = (o_ret, o_dsa, o_nsa, o_ssd)
    aux = (None, o_dsa_aux, o_nsa_aux, None)
    t0 = 0
    for o_ref, aux_ref, nt in zip(outs, aux, _GROUP_TILES):
        y = _dot(xb, wm_ref[:, t0 * LANE:(t0 + nt) * LANE])
        for t in range(nt):
            yt = y[:, t * LANE:(t + 1) * LANE]
            gt = t0 + t
            if gt in _ROPED_TILES:
                tab = _TILE_TABLE[_ROPED_TILES.index(gt)]
                half = _TABLE_HALF[tab]
                lanes = slice(tab * LANE, (tab + 1) * LANE)
                yt = (yt * cos_ref[:, lanes] + pltpu.roll(yt, LANE - half, 1) * shi_ref[:, lanes]
                      + pltpu.roll(yt, half, 1) * slo_ref[:, lanes])
            o_ref[:, t * LANE:(t + 1) * LANE] = yt.astype(o_ref.dtype)
            if aux_ref is not None and t == AUX_TILE:
                aux_ref[...] = yt
        t0 += nt


def _inproj(x, wm, tables, S, tm=256):
    T, D = x.shape
    ns = S // tm
    widths = [nt * LANE for nt in _GROUP_TILES] + [LANE, LANE]
    dtypes = [F32, BF, BF, F32, F32, F32]
    table_spec = pl.BlockSpec((tm, N_TABLES * LANE), lambda p, b: (p, 0))
    rows = lambda width: pl.BlockSpec((tm, width), lambda p, b: (b * ns + p, 0))
    return pl.pallas_call(
        _inproj_kernel,
        grid=(ns, T // S),
        in_specs=[
            rows(D),
            pl.BlockSpec((D, N_MAIN), lambda p, b: (0, 0), pipeline_mode=pl.Buffered(1)),
            table_spec, table_spec, table_spec,
        ],
        out_specs=[rows(w) for w in widths],
        out_shape=[jax.ShapeDtypeStruct((T, w), dt) for w, dt in zip(widths, dtypes)],
        compiler_params=_params(("parallel", "arbitrary")),
        name="in_proj",
    )(x, wm, *tables)


def _ret_kernel(r_ref, o_ref, st_ref):
    C = RET_CHUNK

    @pl.when(pl.program_id(0) == 0)
    def _():
        st_ref[...] = jnp.zeros_like(st_ref)

    hq = RET_HEADS * RET_DK
    rel = (lax.broadcasted_iota(jnp.int32, (C, C), 0) - lax.broadcasted_iota(jnp.int32, (C, C), 1)).astype(F32)
    row = lax.broadcasted_iota(jnp.int32, (C, 1), 0).astype(F32)
    head_consts = []
    for h in range(RET_HEADS):
        lg = math.log1p(-(2.0 ** (-5.0 - h)))
        head_consts.append((jnp.where(rel >= 0, jnp.exp(jnp.maximum(rel, 0.0) * lg), 0.0),
                            jnp.exp((C - 1 - row) * lg), jnp.exp((row + 1.0) * lg), math.exp(C * lg)))
    for b in range(r_ref.shape[0]):
        r = r_ref[b]
        q, k = r[:, :hq], r[:, hq:2 * hq]
        v = r[:, 2 * hq:2 * hq + RET_HEADS * RET_DV]
        g = r[:, 2 * hq + RET_HEADS * RET_DV:]
        outs = []
        for h, (decay, zeta, xi, chunk_decay) in enumerate(head_consts):
            qh = q[:, h * RET_DK:(h + 1) * RET_DK].astype(BF)
            kh = k[:, h * RET_DK:(h + 1) * RET_DK]
            vh = v[:, h * RET_DV:(h + 1) * RET_DV].astype(BF)
            sc = _dot_nt(qh, kh.astype(BF)) * decay
            intra = _dot(sc.astype(BF), vh)
            prev = st_ref[b * RET_HEADS + h]
            cross = _dot(qh, prev.astype(BF)) * xi
            st_ref[b * RET_HEADS + h] = prev * chunk_decay + _dot_tn((kh * zeta).astype(BF), vh)
            o = intra + cross
            mu = jnp.mean(o, -1, keepdims=True)
            d = o - mu
            var = jnp.mean(d * d, -1, keepdims=True)
            outs.append(d * lax.rsqrt(var + LN_EPS))
        o_ref[b] = _silu(g) * jnp.concatenate(outs, axis=-1)


def _retention(ret, B, S):
    n = S // RET_CHUNK
    W = ret.shape[1]
    return pl.pallas_call(
        _ret_kernel,
        grid=(n,),
        in_specs=[pl.BlockSpec((B, RET_CHUNK, W), lambda c: (0, c, 0))],
        out_specs=pl.BlockSpec((B, RET_CHUNK, BRANCH_W), lambda c: (0, c, 0)),
        out_shape=jax.ShapeDtypeStruct((B, S, BRANCH_W), F32),
        scratch_shapes=[pltpu.VMEM((B * RET_HEADS, RET_DK, RET_DV), F32)],
        compiler_params=_params(("arbitrary",)),
        name="retention",
    )(ret.reshape(B, S, W)).reshape(B * S, BRANCH_W)


def _softplus(v):
    return jnp.maximum(v, 0.0) + jnp.log1p(jnp.exp(-jnp.abs(v)))


def _ssd_kernel(s_ref, cw_ref, cb_ref, dtb_c_ref, a_c_ref, dtb_r_ref, a_r_ref, dsk_ref, ng_ref,
                o_ref, st_ref, xpad_ref):
    Q, H, P, N = SSM_CHUNK, SSM_HEADS, SSM_HEADDIM, SSM_STATE

    @pl.when(pl.program_id(0) == 0)
    def _():
        st_ref[...] = jnp.zeros_like(st_ref)
        xpad_ref[:, 0:CONV_PAD, :] = jnp.zeros((xpad_ref.shape[0], CONV_PAD, SSM_CONV_DIM), F32)

    ri = lax.broadcasted_iota(jnp.int32, (Q, Q), 0)
    ci = lax.broadcasted_iota(jnp.int32, (Q, Q), 1)
    tril = ri >= ci
    lo_tri = jnp.where(tril, 1.0, 0.0).astype(BF)
    up_tri = jnp.where(ci >= ri, 1.0, 0.0).astype(BF)
    gn = SSM_GROUPS * N
    gw = SSM_INNER // SSM_GROUPS
    for b in range(s_ref.shape[0]):
        z = s_ref[b, :, :SSM_INNER]
        xpad_ref[b, CONV_PAD:, :] = s_ref[b, :, SSM_INNER:SSM_INNER + SSM_CONV_DIM]
        dt_raw = s_ref[b, :, SSM_INNER + SSM_CONV_DIM:SSM_INNER + SSM_CONV_DIM + H]
        conv = cb_ref[...]
        for kk in range(SSM_CONV):
            off = CONV_PAD - (SSM_CONV - 1) + kk
            conv = conv + cw_ref[kk:kk + 1, :] * xpad_ref[b, off:off + Q, :]
        xpad_ref[b, 0:CONV_PAD, :] = xpad_ref[b, Q:Q + CONV_PAD, :]
        xc = _silu(conv)
        xs = xc[:, :SSM_INNER]
        bm = xc[:, SSM_INNER:SSM_INNER + gn]
        cm = xc[:, SSM_INNER + gn:]

        dt_c = _softplus(dt_raw + dtb_c_ref[...])
        adt_c = dt_c * a_c_ref[...]
        dt_tile = s_ref[b, :, SSM_INNER + SSM_CONV_DIM:SSM_INNER + SSM_CONV_DIM + LANE]
        dt_r = _softplus(dt_tile.T[0:CONV_PAD, :] + dtb_r_ref[...])
        adt_r = dt_r * a_r_ref[...]
        acs_c = sum(_dot(lo_tri, t) for t in _split3(adt_c))
        acs_r = sum(_dot(t, up_tri) for t in _split3(adt_r))

        outs = []
        for grp in range(SSM_GROUPS):
            bg = bm[:, grp * N:(grp + 1) * N]
            cg = cm[:, grp * N:(grp + 1) * N].astype(BF)
            cb = _dot_nt(cg, bg.astype(BF))
            for h in range(grp * (H // SSM_GROUPS), (grp + 1) * (H // SSM_GROUPS)):
                a_col = acs_c[:, h:h + 1]
                a_last = acs_c[Q - 1:Q, h:h + 1]
                lm = jnp.exp(jnp.where(tril, a_col - acs_r[h:h + 1, :], NEG_INF))
                xh = xs[:, h * P:(h + 1) * P]
                xdt = (xh * dt_c[:, h:h + 1]).astype(BF)
                y = _dot((cb * lm).astype(BF), xdt)
                prev = st_ref[b * H + h]
                y = y + _dot(cg, prev.astype(BF)) * jnp.exp(a_col)
                st_ref[b * H + h] = (prev * jnp.exp(a_last)
                                     + _dot_tn((bg * jnp.exp(a_last - a_col)).astype(BF), xdt))
                outs.append(y)
        y = jnp.concatenate(outs, axis=-1) + xs * dsk_ref[...]
        y = y * _silu(z)
        normed = []
        for grp in range(SSM_GROUPS):
            yg = y[:, grp * gw:(grp + 1) * gw]
            normed.append(yg * lax.rsqrt(jnp.mean(yg * yg, -1, keepdims=True) + LN_EPS))
        o_ref[b] = jnp.concatenate(normed, axis=-1) * ng_ref[...]


def _ssd(ssd, conv_w, conv_b, dt_bias, a_log, d_skip, norm_g, B, S):
    Q, H = SSM_CHUNK, SSM_HEADS
    n = S // Q
    W = ssd.shape[1]
    assert W >= SSM_INNER + SSM_CONV_DIM + LANE
    a = -jnp.exp(a_log.astype(F32))
    pad_r = lambda v: jnp.broadcast_to(jnp.pad(v, (0, CONV_PAD - H))[:, None], (CONV_PAD, Q)).astype(F32)
    full = lambda shape: pl.BlockSpec(shape, lambda c: (0,) * len(shape))
    return pl.pallas_call(
        _ssd_kernel,
        grid=(n,),
        in_specs=[
            pl.BlockSpec((B, Q, W), lambda c: (0, c, 0)),
            full((SSM_CONV, SSM_CONV_DIM)), full((1, SSM_CONV_DIM)),
            full((1, H)), full((1, H)), full((CONV_PAD, Q)), full((CONV_PAD, Q)),
            full((1, SSM_INNER)), full((1, SSM_INNER)),
        ],
        out_specs=pl.BlockSpec((B, Q, BRANCH_W), lambda c: (0, c, 0)),
        out_shape=jax.ShapeDtypeStruct((B, S, BRANCH_W), F32),
        scratch_shapes=[pltpu.VMEM((B * H, SSM_STATE, SSM_HEADDIM), F32),
                        pltpu.VMEM((B, Q + CONV_PAD, SSM_CONV_DIM), F32)],
        compiler_params=_params(("arbitrary",)),
        name="ssd",
    )(ssd.reshape(B, S, W), conv_w, conv_b[None, :], dt_bias[None, :], a[None, :], pad_r(dt_bias),
      pad_r(a), jnp.repeat(d_skip, SSM_HEADDIM)[None, :], norm_g[None, :]).reshape(B * S, BRANCH_W)


def _masked_attention(q_bf, kv_bf, mask, heads, dh):
    k_bf = kv_bf[:, :dh]
    ones_v = jnp.where(lax.broadcasted_iota(jnp.int32, kv_bf.shape, 1) < dh, jnp.ones((), kv_bf.dtype), kv_bf)
    outs = []
    for h in range(heads):
        s = jnp.where(mask, _dot_nt(q_bf[:, h * dh:(h + 1) * dh], k_bf), NEG_INF)
        m = jnp.max(s, -1, keepdims=True)
        m = jnp.where(m > NEG_INF, m, 0.0)
        lv = _dot(jnp.exp2((s - m).astype(BF)), ones_v)
        outs.append(lv[:, dh:] / jnp.maximum(lv[:, 0:1], 1e-30))
    return jnp.concatenate(outs, axis=-1)


def _count(m):
    return jnp.sum(jnp.where(m, 1.0, 0.0), -1, keepdims=True)


IDX_KEY_CHUNK = 256
SEL_BIG = float(2 ** 20)
CAUSAL_SEG = 1024
BISECT_PLAIN_STEPS = 26
BISECT_MAX_STEPS = 400


def _kth_threshold(load, nvalid, vmin, vmax, c_ge0, c_gt0, k):
    kf = float(k)
    short = nvalid <= kf
    up0 = c_ge0 >= kf
    tie0 = up0 & (c_gt0 < kf) & (c_ge0 > kf) & (nvalid > kf)
    lo0 = jnp.where(short, jnp.finfo(F32).min, jnp.where(up0, 0.0, vmin))
    hi0 = jnp.where(up0, vmax, 0.0)
    done0 = jnp.where(short | tie0 | (c_ge0 == kf), 1.0, 0.0)

    def split(lo, hi, done):
        mid = jnp.where(done > 0.5, lo, lo + (hi - lo) * 0.5)
        c = _count(load() >= mid)
        up = (c >= kf) & (done < 0.5)
        return jnp.where(up, mid, lo), jnp.where(up | (done > 0.5), hi, mid), jnp.where(up & (c == kf), 1.0, done)

    def plain_cond(st):
        return (st[0] < BISECT_PLAIN_STEPS) & (jnp.min(st[3]) < 0.5)

    def plain_step(st):
        it, lo, hi, done = st
        lo, hi, done = split(lo, hi, done)
        lo, hi, done = split(lo, hi, done)
        return it + 2, lo, hi, done

    _, lo, hi, done = lax.while_loop(plain_cond, plain_step, (jnp.int32(0), lo0, hi0, done0))

    def exact_cond(st):
        return (st[0] < BISECT_MAX_STEPS) & (jnp.min(st[4]) < 0.5)

    def exact_step(st):
        it, lo, hi, tie, done = st
        sc = load()
        cand = jnp.min(jnp.where(sc >= lo, sc, jnp.inf), -1, keepdims=True)
        fin = (_count(sc > cand) < kf) & (done < 0.5)
        lo = jnp.where(fin, cand, lo)
        tie = jnp.where(fin, 1.0, tie)
        done = jnp.where(fin, 1.0, done)
        lo, hi, done = split(lo, hi, done)
        return it + 1, lo, hi, tie, done

    _, thr, _, tie, _ = lax.while_loop(
        exact_cond, exact_step, (jnp.int32(0), lo, hi, jnp.where(tie0, 1.0, 0.0), done))
    return thr, tie


def _dsa_body(q_ref, aux_ref, kv_ref, ik_ref, o_ref, sc_ref, *, n_keep, E):
    Qb, CH = DSA_QB, IDX_KEY_CHUNK
    bi = pl.program_id(1)
    qpos = bi * Qb + lax.broadcasted_iota(jnp.int32, (Qb, 1), 0)
    hq = DSA_HEADS * DSA_DH
    iw = aux_ref[:, IDX_DH:IDX_DH + IDX_HEADS] * (IDX_HEADS ** -0.5) * (IDX_DH ** -0.5)
    iq = [q_ref[:, hq + h * IDX_DH:hq + (h + 1) * IDX_DH] for h in range(IDX_HEADS)]
    fold = lambda f, a: functools.reduce(f, [a[:, j:j + LANE] for j in range(0, CH, LANE)])
    hi_acc = jnp.full((Qb, LANE), NEG_INF, F32)
    lo_acc = jnp.full((Qb, LANE), jnp.inf, F32)
    for c0 in range(0, E, CH):
        ik = ik_ref[c0:c0 + CH, :IDX_DH]
        acc = jnp.maximum(_dot_nt(iq[0], ik), 0.0) * iw[:, 0:1]
        for h in range(1, IDX_HEADS):
            acc = acc + jnp.maximum(_dot_nt(iq[h], ik), 0.0) * iw[:, h:h + 1]
        causal = c0 + lax.broadcasted_iota(jnp.int32, (1, CH), 1) <= qpos
        masked = jnp.where(causal, acc, NEG_INF)
        sc_ref[:, c0:c0 + CH] = masked
        hi_acc = jnp.maximum(hi_acc, fold(jnp.maximum, masked))
        lo_acc = jnp.minimum(lo_acc, fold(jnp.minimum, jnp.where(causal, acc, jnp.inf)))
    load = lambda: sc_ref[:, 0:E]
    thr, tie = _kth_threshold(load, (qpos + 1).astype(F32), jnp.min(lo_acc, -1, keepdims=True),
                              jnp.max(hi_acc, -1, keepdims=True), _count(load() >= 0.0), _count(load() > 0.0),
                              n_keep)

    @pl.when(jnp.max(tie) > 0.5)
    def resolve_ties():
        sc = load()
        gt = sc > thr
        eqf = jnp.where(sc == thr, 1.0, 0.0)
        need = n_keep - _count(gt)
        ch = 256 if E % 256 == 0 else LANE
        before = jnp.where(lax.broadcasted_iota(jnp.int32, (ch, ch), 0)
                           < lax.broadcasted_iota(jnp.int32, (ch, ch), 1), 1.0, 0.0).astype(BF)
        run = jnp.zeros((Qb, 1), F32)
        take = []
        for c0 in range(0, E, ch):
            eqc = eqf[:, c0:c0 + ch]
            prefix = _dot(eqc.astype(BF), before) + run
            take.append(jnp.where(prefix < need, eqc, 0.0))
            run = run + jnp.sum(eqc, -1, keepdims=True)
        sc_ref[:, 0:E] = jnp.where(gt | (jnp.concatenate(take, axis=-1) > 0.5), jnp.inf, NEG_INF)

    o_ref[...] = _masked_attention(q_ref[:, :hq], kv_ref[0:E, :], load() >= thr, DSA_HEADS, DSA_DH)


def _for_causal_extent(S, seg, qb, body):
    bi = pl.program_id(1)
    assert S % seg == 0 and seg % qb == 0
    for e in range(seg, S + 1, seg):
        pl.when((bi >= (e - seg) // qb) & (bi < e // qb))(functools.partial(body, e))


def _dsa_kernel(q_ref, aux_ref, kv_ref, ik_ref, o_ref, sc_ref, *, n_keep, seg):
    _for_causal_extent(kv_ref.shape[0], seg, DSA_QB,
                       lambda e: _dsa_body(q_ref, aux_ref, kv_ref, ik_ref, o_ref, sc_ref, n_keep=n_keep, E=e))


def _dsa(dsa, dsa_aux, B, S):
    nb = S // DSA_QB
    n_keep = min(DSA_TOPK, S // 4)
    return pl.pallas_call(
        functools.partial(_dsa_kernel, n_keep=n_keep, seg=min(CAUSAL_SEG, S)),
        grid=(B, nb),
        in_specs=[
            pl.BlockSpec((DSA_QB, 4 * LANE), lambda b, i: (b * nb + i, 0)),
            pl.BlockSpec((DSA_QB, LANE), lambda b, i: (b * nb + i, 0)),
            pl.BlockSpec((S, LANE), lambda b, i: (b, 4)),
            pl.BlockSpec((S, LANE), lambda b, i: (b, 5)),
        ],
        out_specs=pl.BlockSpec((DSA_QB, BRANCH_W), lambda b, i: (b * nb + i, 0)),
        out_shape=jax.ShapeDtypeStruct((B * S, BRANCH_W), F32),
        scratch_shapes=[pltpu.VMEM((DSA_QB, S), F32)],
        compiler_params=_params(("parallel", "arbitrary")),
        name="dsa",
    )(dsa, dsa_aux, dsa, dsa)


def _cmp_kernel(g_ref, w1a_ref, w1b_ref, pos_ref, w1_ref, w2_ref, o_ref):
    g = g_ref[...].astype(BF)
    n = g.shape[0]
    a = _dot(g, w1a_ref[...])
    b = _dot(g, w1b_ref[...])
    posterm = _dot(pos_ref[...].astype(BF), w1_ref[...])[0:1, :]
    h = jax.nn.gelu(a + pltpu.roll(b, n - 1, 0) + posterm)
    o_ref[...] = _dot(h.astype(BF), w2_ref[...]).astype(o_ref.dtype)


def _compress(g2, cmp_w1, cmp_w2, cmp_pos):
    _, B, n, W = g2.shape
    Dh = NSA_DH
    half = W
    w1 = cmp_w1.astype(BF)
    pos8 = jnp.broadcast_to(cmp_pos.reshape(2, 1, CMP_LEN * Dh), (2, 8, CMP_LEN * Dh))
    return pl.pallas_call(
        _cmp_kernel,
        grid=(2, B),
        in_specs=[
            pl.BlockSpec((None, None, n, W), lambda i, b: (i, b, 0, 0)),
            pl.BlockSpec((None, half, Dh), lambda i, b: (i, 0, 0)),
            pl.BlockSpec((None, half, Dh), lambda i, b: (i, 1, 0)),
            pl.BlockSpec((None, 8, CMP_LEN * Dh), lambda i, b: (i, 0, 0)),
            pl.BlockSpec((None, CMP_LEN * Dh, Dh), lambda i, b: (i, 0, 0)),
            pl.BlockSpec((None, Dh, Dh), lambda i, b: (i, 0, 0)),
        ],
        out_specs=pl.BlockSpec((None, None, n, Dh), lambda i, b: (i, b, 0, 0)),
        out_shape=jax.ShapeDtypeStruct((2, B, n, Dh), BF),
        compiler_params=_params(("parallel", "parallel")),
        name="nsa_compress",
    )(g2, w1, w1, pos8, w1, cmp_w2.astype(BF))


def _nsa_kernel(q_ref, gt_ref, kvc_ref, sel_ref, win_ref, exp_ref, o_ref, osel_ref, *, n_top, seg):
    Qb, H, Dh = NSA_QB, NSA_HEADS, NSA_DH
    S = sel_ref.shape[0]
    n_cmp = kvc_ref.shape[1]
    n_blk = S // SEL_LEN
    bi = pl.program_id(1)
    qpos = bi * Qb + lax.broadcasted_iota(jnp.int32, (Qb, 1), 0)
    q = q_ref[...]

    kc = kvc_ref[0]
    vc = kvc_ref[1]
    cidx = lax.broadcasted_iota(jnp.int32, (1, n_cmp), 1)
    vis = cidx * CMP_STRIDE + (CMP_LEN - 1) <= qpos
    o_cmp = []
    psum = jnp.zeros((Qb, n_cmp), F32)
    for h in range(H):
        s = jnp.where(vis, _dot_nt(q[:, h * Dh:(h + 1) * Dh], kc), NEG_INF)
        m = jnp.max(s, -1, keepdims=True)
        m = jnp.where(m > NEG_INF, m, 0.0)
        e = jnp.exp2(s - m)
        p = e / jnp.maximum(jnp.sum(e, -1, keepdims=True), 1e-30)
        psum = psum + p
        o_cmp.append(_dot(p.astype(BF), vc))

    js = lax.broadcasted_iota(jnp.int32, (n_blk, 1), 0) * SEL_LEN
    cs = lax.broadcasted_iota(jnp.int32, (1, n_cmp), 1) * CMP_STRIDE
    ov = jnp.maximum(jnp.minimum(cs + CMP_LEN, js + SEL_LEN) - jnp.maximum(cs, js), 0).astype(F32) / CMP_LEN
    ov = ov.astype(BF)
    imp = sum(_dot_nt(ov, t) for t in _split3(psum))
    blk = lax.broadcasted_iota(jnp.int32, (n_blk, 1), 0)
    sel_shift = SEL_LEN.bit_length() - 1
    cur = jnp.right_shift(bi * Qb + lax.broadcasted_iota(jnp.int32, (1, Qb), 1), sel_shift)
    forced = (blk == 0) | (blk == cur) | (blk == cur - 1)
    imp = jnp.where(blk <= cur, jnp.where(forced, jnp.inf, imp), NEG_INF)
    rank = jnp.zeros((n_blk, Qb), F32)
    for j in range(n_blk):
        row = imp[j:j + 1, :]
        rank = rank + jnp.where((row > imp) | ((row == imp) & (blk > j)), 1.0, 0.0)
    sub = lax.broadcasted_iota(jnp.int32, (LANE - n_blk, Qb), 0)
    chosen = jnp.concatenate([jnp.where(rank < n_top, SEL_BIG, 0.0), jnp.where(sub < 2, 1.0, 0.0)], axis=0)
    chosen = chosen.T.astype(BF)
    bound = SEL_BIG - 0.5 - qpos.astype(F32)

    def selected(e):
        mask = _dot(chosen, exp_ref[:, 0:e]) > bound
        osel_ref[...] = _masked_attention(q, sel_ref[0:e, :], mask, H, Dh)

    _for_causal_extent(S, seg, Qb, selected)
    o_sel = osel_ref[...]

    wlen = WINDOW + Qb
    start = pl.multiple_of(jnp.maximum(bi * Qb - WINDOW, 0), Qb)
    kwin = win_ref[pl.ds(start, wlen), :]
    dlt = qpos - (start + lax.broadcasted_iota(jnp.int32, (1, wlen), 1))
    o_win = _masked_attention(q, kwin, (dlt >= 0) & (dlt < WINDOW), H, Dh)

    g = jax.nn.sigmoid(gt_ref[:, :3 * H])
    outs = []
    for h in range(H):
        outs.append(g[:, 3 * h:3 * h + 1] * o_cmp[h]
                    + g[:, 3 * h + 1:3 * h + 2] * o_sel[:, h * Dh:(h + 1) * Dh]
                    + g[:, 3 * h + 2:3 * h + 3] * o_win[:, h * Dh:(h + 1) * Dh])
    o_ref[...] = jnp.concatenate(outs, axis=-1)


def _nsa(nsa, nsa_aux, kvc, B, S):
    nb = S // NSA_QB
    n_cmp = kvc.shape[2]
    n_blk = S // SEL_LEN
    n_top = min(SEL_TOPN, n_blk)
    assert S >= WINDOW + NSA_QB and n_blk + 2 <= LANE and S < SEL_BIG and n_blk <= 256
    kpos = np.arange(S)
    expand = np.zeros((LANE, S), np.float32)
    expand[:n_blk] = kpos[None, :] // SEL_LEN == np.arange(n_blk)[:, None]
    expand[n_blk] = -(kpos // SEL_LEN * SEL_LEN)
    expand[n_blk + 1] = -(kpos % SEL_LEN)
    expand = jnp.asarray(expand, BF)
    return pl.pallas_call(
        functools.partial(_nsa_kernel, n_top=n_top, seg=min(CAUSAL_SEG, S)),
        grid=(B, nb),
        in_specs=[
            pl.BlockSpec((NSA_QB, 2 * LANE), lambda b, i: (b * nb + i, 0)),
            pl.BlockSpec((NSA_QB, LANE), lambda b, i: (b * nb + i, 0)),
            pl.BlockSpec((2, None, n_cmp, NSA_DH), lambda b, i: (0, b, 0, 0)),
            pl.BlockSpec((S, LANE), lambda b, i: (b, 3)),
            pl.BlockSpec((S, LANE), lambda b, i: (b, 4)),
            pl.BlockSpec((LANE, S), lambda b, i: (0, 0)),
        ],
        out_specs=pl.BlockSpec((NSA_QB, BRANCH_W), lambda b, i: (b * nb + i, 0)),
        out_shape=jax.ShapeDtypeStruct((B * S, BRANCH_W), F32),
        scratch_shapes=[pltpu.VMEM((NSA_QB, BRANCH_W), F32)],
        compiler_params=_params(("parallel", "arbitrary")),
        name="nsa",
    )(nsa, nsa_aux, kvc, nsa, nsa, expand)


def _merge_kernel(x_ref, y0, y1, y2, y3, wg_ref, wb_ref, wo_ref, g_ref, b_ref, o_ref, *, alpha):
    x = x_ref[...]
    xb = x.astype(BF)
    D = x.shape[1]
    merged = jnp.zeros_like(x)
    for n, y_ref in enumerate((y0, y1, y2, y3)):
        gate = jax.nn.sigmoid(_dot(xb, wg_ref[:, n * D:(n + 1) * D]))
        merged = merged + gate * _dot(y_ref[...].astype(BF), wb_ref[n])
    o_ref[...] = _layer_norm(alpha * x + _dot(merged.astype(BF), wo_ref[...]), g_ref[...], b_ref[...])


def _merge(x, ys, w_gate, w_branch, w_out, layer, g, b, alpha, tm=512):
    T, D = x.shape
    full = lambda shape: pl.BlockSpec(shape, lambda i: (0,) * len(shape), pipeline_mode=pl.Buffered(1))
    return pl.pallas_call(
        functools.partial(_merge_kernel, alpha=alpha),
        grid=(T // tm,),
        in_specs=[pl.BlockSpec((tm, D), lambda i: (i, 0))]
        + [pl.BlockSpec((tm, BRANCH_W), lambda i: (i, 0))] * N_BRANCH
        + [full(w_gate.shape), _layer_spec(w_branch, layer), _layer_spec(w_out, layer), full((1, D)), full((1, D))],
        out_specs=pl.BlockSpec((tm, D), lambda i: (i, 0)),
        out_shape=jax.ShapeDtypeStruct((T, D), F32),
        compiler_params=_params(("parallel",)),
        name="merge",
    )(x, *ys, w_gate, w_branch, w_out, g, b)


def _matmul_kernel(a_ref, w_ref, o_ref):
    o_ref[...] = _dot(a_ref[...].astype(BF), w_ref[...]).astype(o_ref.dtype)


def _matmul(a, w, layer, tm):
    M, K = a.shape
    N = w.shape[2]
    return pl.pallas_call(
        _matmul_kernel,
        grid=(M // tm,),
        in_specs=[pl.BlockSpec((tm, K), lambda i: (i, 0)), _layer_spec(w, layer)],
        out_specs=pl.BlockSpec((tm, N), lambda i: (i, 0)),
        out_shape=jax.ShapeDtypeStruct((M, N), BF),
        compiler_params=_params(("parallel",)),
        name="kv_proj",
    )(a, w)


def _xattn_kernel(x_ref, kv_ref, wq_ref, wo_ref, g_ref, b_ref, o_ref, *, alpha):
    x = x_ref[...]
    D = x.shape[1]
    dh = D // X_HEADS
    q = _dot(x.astype(BF), wq_ref[...])
    outs = []
    for h in range(X_HEADS):
        k = kv_ref[:, h * dh:(h + 1) * dh]
        v = kv_ref[:, D + h * dh:D + (h + 1) * dh]
        s = _dot_nt(q[:, h * dh:(h + 1) * dh].astype(BF), k) * (dh ** -0.5)
        e = jnp.exp(s - jnp.max(s, -1, keepdims=True))
        outs.append(_dot(e.astype(BF), v) / jnp.sum(e, -1, keepdims=True))
    att = jnp.concatenate(outs, axis=-1).astype(BF)
    o_ref[...] = _layer_norm(alpha * x + _dot(att, wo_ref[...]), g_ref[...], b_ref[...])


def _xattn(x, kv, wq, wo, layer, g, b, alpha, S, M, tm=1024):
    T, D = x.shape
    per = S // tm
    full = lambda shape: pl.BlockSpec(shape, lambda i: (0,) * len(shape))
    return pl.pallas_call(
        functools.partial(_xattn_kernel, alpha=alpha),
        grid=(T // tm,),
        in_specs=[pl.BlockSpec((tm, D), lambda i: (i, 0)),
                  pl.BlockSpec((M, 2 * D), lambda i: (i // per, 0)),
                  _layer_spec(wq, layer), _layer_spec(wo, layer), full((1, D)), full((1, D))],
        out_specs=pl.BlockSpec((tm, D), lambda i: (i, 0)),
        out_shape=jax.ShapeDtypeStruct((T, D), F32),
        compiler_params=_params(("parallel",)),
        name="xattn",
    )(x, kv, wq, wo, g, b)


def _rope_tables(S):
    pos = jnp.arange(S).astype(F32)

    def base(rot, theta):
        half = rot // 2
        inv = jnp.power(jnp.float32(theta), -2.0 * jnp.arange(half, dtype=F32) / rot)
        ang = pos[:, None] * inv[None, :]
        return jnp.cos(ang), jnp.sin(ang)

    def tile(hd, rot, theta, width, scale=1.0):
        c, s = base(rot, theta)
        lane = np.arange(LANE)
        jj = lane % hd
        half = rot // 2
        first = (jj < half) & (lane < width)
        second = (jj >= half) & (jj < rot) & (lane < width)
        idx = jj % half
        ct = jnp.where((first | second)[None, :], c[:, idx], 1.0) * scale
        s_hi = jnp.where(first[None, :], -s[:, idx], 0.0) * scale
        s_lo = jnp.where(second[None, :], s[:, idx], 0.0) * scale
        return ct, s_hi, s_lo

    q_scale = DSA_DH ** -0.5 * LOG2E
    tabs = [
        tile(RET_DK, RET_DK, RET_THETA, LANE),
        tile(RET_DK, RET_DK, RET_THETA, LANE, RET_DK ** -0.5),
        tile(DSA_DH, DSA_DH // ROPE_FRAC, ROPE_THETA, LANE, q_scale),
        tile(DSA_DH, DSA_DH // ROPE_FRAC, ROPE_THETA, DSA_DH),
        tile(IDX_DH, IDX_DH // ROPE_FRAC, ROPE_THETA, LANE),
        tile(IDX_DH, IDX_DH // ROPE_FRAC, ROPE_THETA, IDX_DH),
    ]
    return tuple(jnp.concatenate([t[i] for t in tabs], 1) for i in range(3))


def _pack_w_in(wb, layer):
    D = wb.shape[1]
    main = []
    for _, segs, width in _GROUPS:
        used = 0
        for s in segs:
            start, n = _SEG[s]
            main.append(wb[layer, :, start:start + n])
            used += n
        main.append(jnp.zeros((D, width - used), BF))
    return jnp.concatenate(main, 1), wb[layer, :, GATE_START:]


def kernel(x, mem, ln_g, ln_b, ffn1_w_gu, ffn1_w_down, w_in, cmp_w1, cmp_w2, cmp_pos, conv_w, conv_b,
           dt_bias, a_log, d_skip, ssm_norm_g, w_branch, w_out, xattn_wq, xattn_wkv, xattn_wo,
           ffn2_w_gu, ffn2_w_down):
    B, S, D = x.shape
    M = mem.shape[1]
    depth = ln_g.shape[0]
    alpha = (2 * depth) ** 0.25
    tables = _rope_tables(S)
    h = x
    mem2 = mem.reshape(B * M, D)
    bf = lambda w: w.astype(BF)
    ffn1_w_gu, ffn1_w_down, ffn2_w_gu, ffn2_w_down = bf(ffn1_w_gu), bf(ffn1_w_down), bf(ffn2_w_gu), bf(ffn2_w_down)
    w_in, w_branch, w_out = bf(w_in), bf(w_branch), bf(w_out)
    xattn_wq, xattn_wkv, xattn_wo = bf(xattn_wq), bf(xattn_wkv), bf(xattn_wo)
    for l in range(depth):
        lg = lambda i: ln_g[l, i][None, :]
        lb = lambda i: ln_b[l, i][None, :]
        h = _ffn(h, ffn1_w_gu, ffn1_w_down, l, lg(0), lb(0), alpha)

        wm, w_gate = _pack_w_in(w_in, l)
        ret, dsa, nsa, ssd, dsa_aux, nsa_aux = _inproj(h, wm, tables, S)
        y_ret = _retention(ret, B, S)
        y_dsa = _dsa(dsa, dsa_aux, B, S)
        g2 = jnp.stack([nsa[:, 2 * LANE:2 * LANE + NSA_DH], nsa[:, 2 * LANE + NSA_DH:3 * LANE]])
        kvc = _compress(g2.reshape(2, B, S // CMP_STRIDE, CMP_STRIDE * NSA_DH), cmp_w1[l], cmp_w2[l], cmp_pos[l])
        y_nsa = _nsa(nsa, nsa_aux, kvc, B, S)
        y_ssd = _ssd(ssd, conv_w[l], conv_b[l], dt_bias[l], a_log[l], d_skip[l], ssm_norm_g[l], B, S)
        h = _merge(h, (y_ret, y_dsa, y_nsa, y_ssd), w_gate, w_branch, w_out, l, lg(1), lb(1), alpha)

        kv = _matmul(mem2, xattn_wkv, l, tm=min(256, B * M))
        h = _xattn(h, kv, xattn_wq, xattn_wo, l, lg(2), lb(2), alpha, S, M)
        h = _ffn(h, ffn2_w_gu, ffn2_w_down, l, lg(3), lb(3), alpha, out_batch=B if l == depth - 1 else None)
    return h
```

```python
import functools
import math

import numpy as np
import jax
import jax.numpy as jnp
from jax import lax
from jax.experimental import pallas as pl
from jax.experimental.pallas import tpu as pltpu

F32 = jnp.float32
BF = jnp.bfloat16
NEG_INF = float("-inf")

DSA_QB = 128
NSA_QB = 256
ROPE_THETA = 500000.0
ROPE_FRAC = 4
LN_EPS = 1e-5
RET_HEADS, RET_DK, RET_DV, RET_CHUNK, RET_THETA = 4, 32, 64, 128, 10000.0
DSA_HEADS, DSA_DH, IDX_HEADS, IDX_DH, DSA_TOPK = 4, 64, 8, 32, 256
NSA_HEADS, NSA_DH, CMP_LEN, CMP_STRIDE, SEL_LEN, SEL_TOPN, WINDOW = 4, 64, 32, 16, 64, 16, 512
SSM_HEADS, SSM_HEADDIM, SSM_GROUPS, SSM_STATE, SSM_CONV, SSM_CHUNK = 4, 64, 2, 128, 4, 128
SSM_INNER = SSM_HEADS * SSM_HEADDIM
SSM_CONV_DIM = SSM_INNER + 2 * SSM_GROUPS * SSM_STATE
N_BRANCH = 4
BRANCH_W = 256
X_HEADS = 4

LANE = 128
SUBLANE = 8
CONV_PAD = SUBLANE
VMEM_LIMIT = 56 * 1024 * 1024

_SEG_NAMES = ("r_q", "r_k", "r_v", "r_g", "d_q", "d_k", "d_v", "i_q", "i_k", "i_w",
              "n_q", "n_kc", "n_vc", "n_ks", "n_vs", "n_kw", "n_vw", "n_g", "s_z", "s_xbc", "s_dt")
_SEG_WIDTHS = (RET_HEADS * RET_DK, RET_HEADS * RET_DK, RET_HEADS * RET_DV, RET_HEADS * RET_DV,
               DSA_HEADS * DSA_DH, DSA_DH, DSA_DH, IDX_HEADS * IDX_DH, IDX_DH, IDX_HEADS,
               NSA_HEADS * NSA_DH, NSA_DH, NSA_DH, NSA_DH, NSA_DH, NSA_DH, NSA_DH, NSA_HEADS * 3,
               SSM_INNER, SSM_CONV_DIM, SSM_HEADS)
_SEG = {}
_o = 0
for _n, _w in zip(_SEG_NAMES, _SEG_WIDTHS):
    _SEG[_n] = (_o, _w)
    _o += _w
GATE_START = _o

_GROUPS = (
    ("ret", ("r_q", "r_k", "r_v", "r_g"), 768),
    ("dsa", ("d_q", "i_q", "d_k", "d_v", "i_k", "i_w"), 768),
    ("nsa", ("n_q", "n_kc", "n_vc", "n_ks", "n_vs", "n_kw", "n_vw", "n_g"), 768),
    ("ssd", ("s_z", "s_xbc", "s_dt"), 1152),
)
_ROPE = {
    "r_q": (RET_DK, RET_DK, 0), "r_k": (RET_DK, RET_DK, 1),
    "d_q": (DSA_DH, DSA_DH // ROPE_FRAC, 2), "n_q": (NSA_DH, NSA_DH // ROPE_FRAC, 2),
    "d_k": (DSA_DH, DSA_DH // ROPE_FRAC, 3), "n_kc": (NSA_DH, NSA_DH // ROPE_FRAC, 3),
    "n_ks": (NSA_DH, NSA_DH // ROPE_FRAC, 3), "n_kw": (NSA_DH, NSA_DH // ROPE_FRAC, 3),
    "i_q": (IDX_DH, IDX_DH // ROPE_FRAC, 4), "i_k": (IDX_DH, IDX_DH // ROPE_FRAC, 5),
}
N_TABLES = 6
_TABLE_HALF = (RET_DK // 2, RET_DK // 2, DSA_DH // ROPE_FRAC // 2, DSA_DH // ROPE_FRAC // 2,
               IDX_DH // ROPE_FRAC // 2, IDX_DH // ROPE_FRAC // 2)
LOG2E = math.log2(math.e)


def _build_layout():
    tile_seg = []
    group_tiles = []
    for _, segs, width in _GROUPS:
        cols = []
        for s in segs:
            cols += [s] * _SEG[s][1]
        cols += [None] * (width - len(cols))
        tile_seg += [cols[t * LANE:(t + 1) * LANE] for t in range(width // LANE)]
        group_tiles.append(width // LANE)
    roped_tiles = [t for t, segs in enumerate(tile_seg) if any(s in _ROPE for s in segs)]
    tile_table = [_ROPE[tile_seg[t][0]][2] for t in roped_tiles]
    return roped_tiles, tile_table, group_tiles, len(tile_seg) * LANE


_ROPED_TILES, _TILE_TABLE, _GROUP_TILES, N_MAIN = _build_layout()


def _dot(a, b):
    return jnp.dot(a, b, preferred_element_type=F32)


def _dot_nt(a, b):
    return lax.dot_general(a, b, (((1,), (1,)), ((), ())), preferred_element_type=F32)


def _dot_tn(a, b):
    return lax.dot_general(a, b, (((0,), (0,)), ((), ())), preferred_element_type=F32)


def _split3(a):
    hi = a.astype(BF)
    r1 = a - hi.astype(F32)
    mid = r1.astype(BF)
    lo = (r1 - mid.astype(F32)).astype(BF)
    return hi, mid, lo


def _layer_norm(v, g, b):
    mu = jnp.mean(v, -1, keepdims=True)
    d = v - mu
    var = jnp.mean(d * d, -1, keepdims=True)
    return d * lax.rsqrt(var + LN_EPS) * g + b


def _silu(v):
    return v * jax.nn.sigmoid(v)


def _params(sem):
    return pltpu.CompilerParams(dimension_semantics=sem, vmem_limit_bytes=VMEM_LIMIT)


def _ffn_kernel(x_ref, wg_ref, wu_ref, wd_ref, g_ref, b_ref, o_ref, *, alpha):
    x = x_ref[...]
    xb = x.astype(BF)
    h = (_silu(_dot(xb, wg_ref[...])) * _dot(xb, wu_ref[...])).astype(BF)
    o_ref[...] = _layer_norm(alpha * x + 0.5 * _dot(h, wd_ref[...]), g_ref[...], b_ref[...])


def _layer_spec(w, layer, block=None, index=None):
    block = tuple(w.shape[1:]) if block is None else block
    index = (0,) * len(block) if index is None else index
    return pl.BlockSpec((None,) + block, lambda *_: (layer,) + index, pipeline_mode=pl.Buffered(1))


def _ffn(x, w_gu, w_down, layer, g, b, alpha, tm=256, out_batch=None):
    D = x.shape[-1]
    T = x.size // D
    F = w_down.shape[1]

    def rows(batch):
        if batch is None:
            return pl.BlockSpec((tm, D), lambda i: (i, 0)), (T, D)
        per = T // batch // tm
        return pl.BlockSpec((None, tm, D), lambda i: (i // per, i % per, 0)), (batch, T // batch, D)

    x_spec, _ = rows(x.shape[0] if x.ndim == 3 else None)
    o_spec, o_shape = rows(out_batch)
    return pl.pallas_call(
        functools.partial(_ffn_kernel, alpha=alpha),
        grid=(T // tm,),
        in_specs=[
            x_spec,
            _layer_spec(w_gu, layer, (D, F), (0, 0)),
            _layer_spec(w_gu, layer, (D, F), (0, 1)),
            _layer_spec(w_down, layer),
            pl.BlockSpec((1, D), lambda i: (0, 0)),
            pl.BlockSpec((1, D), lambda i: (0, 0)),
        ],
        out_specs=o_spec,
        out_shape=jax.ShapeDtypeStruct(o_shape, F32),
        compiler_params=_params(("parallel",)),
        name="ffn",
    )(x, w_gu, w_gu, w_down, g, b)


AUX_TILE = 5


def _inproj_kernel(x_ref, wm_ref, cos_ref, shi_ref, slo_ref, o_ret, o_dsa, o_nsa, o_ssd, o_dsa_aux, o_nsa_aux):
    xb = x_ref[...].astype(BF)
    outs = (o_ret, o_dsa, o_nsa, o_ssd)
    aux = (None, o_dsa_aux, o_nsa_aux, None)
    t0 = 0
    for o_ref, aux_ref, nt in zip(outs, aux, _GROUP_TILES):
        y = _dot(xb, wm_ref[:, t0 * LANE:(t0 + nt) * LANE])
        for t in range(nt):
            yt = y[:, t * LANE:(t + 1) * LANE]
            gt = t0 + t
            if gt in _ROPED_TILES:
                tab = _TILE_TABLE[_ROPED_TILES.index(gt)]
                half = _TABLE_HALF[tab]
                lanes = slice(tab * LANE, (tab + 1) * LANE)
                yt = (yt * cos_ref[:, lanes] + pltpu.roll(yt, LANE - half, 1) * shi_ref[:, lanes]
                      + pltpu.roll(yt, half, 1) * slo_ref[:, lanes])
            o_ref[:, t * LANE:(t + 1) * LANE] = yt.astype(o_ref.dtype)
            if aux_ref is not None and t == AUX_TILE:
                aux_ref[...] = yt
        t0 += nt


def _inproj(x, wm, tables, S, tm=256):
    T, D = x.shape
    ns = S // tm
    widths = [nt * LANE for nt in _GROUP_TILES] + [LANE, LANE]
    dtypes = [F32, BF, BF, F32, F32, F32]
    table_spec = pl.BlockSpec((tm, N_TABLES * LANE), lambda p, b: (p, 0))
    rows = lambda width: pl.BlockSpec((tm, width), lambda p, b: (b * ns + p, 0))
    return pl.pallas_call(
        _inproj_kernel,
        grid=(ns, T // S),
        in_specs=[
            rows(D),
            pl.BlockSpec((D, N_MAIN), lambda p, b: (0, 0), pipeline_mode=pl.Buffered(1)),
            table_spec, table_spec, table_spec,
        ],
        out_specs=[rows(w) for w in widths],
        out_shape=[jax.ShapeDtypeStruct((T, w), dt) for w, dt in zip(widths, dtypes)],
        compiler_params=_params(("parallel", "arbitrary")),
        name="in_proj",
    )(x, wm, *tables)


def _ret_kernel(r_ref, o_ref, st_ref):
    C = RET_CHUNK

    @pl.when(pl.program_id(0) == 0)
    def _():
        st_ref[...] = jnp.zeros_like(st_ref)

    hq = RET_HEADS * RET_DK
    rel = (lax.broadcasted_iota(jnp.int32, (C, C), 0) - lax.broadcasted_iota(jnp.int32, (C, C), 1)).astype(F32)
    row = lax.broadcasted_iota(jnp.int32, (C, 1), 0).astype(F32)
    head_consts = []
    for h in range(RET_HEADS):
        lg = math.log1p(-(2.0 ** (-5.0 - h)))
        head_consts.append((jnp.where(rel >= 0, jnp.exp(jnp.maximum(rel, 0.0) * lg), 0.0),
                            jnp.exp((C - 1 - row) * lg), jnp.exp((row + 1.0) * lg), math.exp(C * lg)))
    hw = RET_HEADS * RET_DV
    dv_shift = RET_DV.bit_length() - 1
    head_avg = jnp.where(jnp.right_shift(lax.broadcasted_iota(jnp.int32, (hw, hw), 0), dv_shift)
                         == jnp.right_shift(lax.broadcasted_iota(jnp.int32, (hw, hw), 1), dv_shift),
                         1.0 / RET_DV, 0.0).astype(BF)
    for b in range(r_ref.shape[0]):
        r = r_ref[b]
        q, k = r[:, :hq], r[:, hq:2 * hq]
        v = r[:, 2 * hq:2 * hq + RET_HEADS * RET_DV]
        g = r[:, 2 * hq + RET_HEADS * RET_DV:]
        outs = []
        for h, (decay, zeta, xi, chunk_decay) in enumerate(head_consts):
            qh = q[:, h * RET_DK:(h + 1) * RET_DK].astype(BF)
            kh = k[:, h * RET_DK:(h + 1) * RET_DK]
            vh = v[:, h * RET_DV:(h + 1) * RET_DV].astype(BF)
            sc = _dot_nt(qh, kh.astype(BF)) * decay
            intra = _dot(sc.astype(BF), vh)
            prev = st_ref[b * RET_HEADS + h]
            cross = _dot(qh, prev.astype(BF)) * xi
            st_ref[b * RET_HEADS + h] = prev * chunk_decay + _dot_tn((kh * zeta).astype(BF), vh)
            outs.append(intra + cross)
        o = jnp.concatenate(outs, axis=-1)
        d = o - sum(_dot(t, head_avg) for t in _split3(o))
        var = sum(_dot(t, head_avg) for t in _split3(d * d))
        o_ref[b] = _silu(g) * (d * lax.rsqrt(var + LN_EPS))


def _retention(ret, B, S):
    n = S // RET_CHUNK
    W = ret.shape[1]
    return pl.pallas_call(
        _ret_kernel,
        grid=(n,),
        in_specs=[pl.BlockSpec((B, RET_CHUNK, W), lambda c: (0, c, 0))],
        out_specs=pl.BlockSpec((B, RET_CHUNK, BRANCH_W), lambda c: (0, c, 0)),
        out_shape=jax.ShapeDtypeStruct((B, S, BRANCH_W), F32),
        scratch_shapes=[pltpu.VMEM((B * RET_HEADS, RET_DK, RET_DV), F32)],
        compiler_params=_params(("arbitrary",)),
        name="retention",
    )(ret.reshape(B, S, W)).reshape(B * S, BRANCH_W)


def _softplus(v):
    return jnp.maximum(v, 0.0) + jnp.log1p(jnp.exp(-jnp.abs(v)))


def _ssd_kernel(s_ref, cw_ref, cb_ref, dtb_c_ref, a_c_ref, dtb_r_ref, a_r_ref, dsk_ref, ng_ref,
                o_ref, st_ref, xpad_ref):
    Q, H, P, N = SSM_CHUNK, SSM_HEADS, SSM_HEADDIM, SSM_STATE

    @pl.when(pl.program_id(0) == 0)
    def _():
        st_ref[...] = jnp.zeros_like(st_ref)
        xpad_ref[:, 0:CONV_PAD, :] = jnp.zeros((xpad_ref.shape[0], CONV_PAD, SSM_CONV_DIM), F32)

    ri = lax.broadcasted_iota(jnp.int32, (Q, Q), 0)
    ci = lax.broadcasted_iota(jnp.int32, (Q, Q), 1)
    tril = ri >= ci
    lo_tri = jnp.where(tril, 1.0, 0.0).astype(BF)
    up_tri = jnp.where(ci >= ri, 1.0, 0.0).astype(BF)
    gn = SSM_GROUPS * N
    gw = SSM_INNER // SSM_GROUPS
    for b in range(s_ref.shape[0]):
        z = s_ref[b, :, :SSM_INNER]
        xpad_ref[b, CONV_PAD:, :] = s_ref[b, :, SSM_INNER:SSM_INNER + SSM_CONV_DIM]
        dt_raw = s_ref[b, :, SSM_INNER + SSM_CONV_DIM:SSM_INNER + SSM_CONV_DIM + H]
        conv = cb_ref[...]
        for kk in range(SSM_CONV):
            off = CONV_PAD - (SSM_CONV - 1) + kk
            conv = conv + cw_ref[kk:kk + 1, :] * xpad_ref[b, off:off + Q, :]
        xpad_ref[b, 0:CONV_PAD, :] = xpad_ref[b, Q:Q + CONV_PAD, :]
        xc = _silu(conv)
        xs = xc[:, :SSM_INNER]
        bm = xc[:, SSM_INNER:SSM_INNER + gn]
        cm = xc[:, SSM_INNER + gn:]

        dt_c = _softplus(dt_raw + dtb_c_ref[...])
        adt_c = dt_c * a_c_ref[...]
        dt_tile = s_ref[b, :, SSM_INNER + SSM_CONV_DIM:SSM_INNER + SSM_CONV_DIM + LANE]
        dt_r = _softplus(dt_tile.T[0:CONV_PAD, :] + dtb_r_ref[...])
        adt_r = dt_r * a_r_ref[...]
        acs_c = sum(_dot(lo_tri, t) for t in _split3(adt_c))
        acs_r = sum(_dot(t, up_tri) for t in _split3(adt_r))

        outs = []
        for grp in range(SSM_GROUPS):
            bg = bm[:, grp * N:(grp + 1) * N]
            cg = cm[:, grp * N:(grp + 1) * N].astype(BF)
            cb = _dot_nt(cg, bg.astype(BF))
            for h in range(grp * (H // SSM_GROUPS), (grp + 1) * (H // SSM_GROUPS)):
                a_col = acs_c[:, h:h + 1]
                a_last = acs_c[Q - 1:Q, h:h + 1]
                lm = jnp.exp(jnp.where(tril, a_col - acs_r[h:h + 1, :], NEG_INF))
                xh = xs[:, h * P:(h + 1) * P]
                xdt = (xh * dt_c[:, h:h + 1]).astype(BF)
                y = _dot((cb * lm).astype(BF), xdt)
                prev = st_ref[b * H + h]
                y = y + _dot(cg, prev.astype(BF)) * jnp.exp(a_col)
                st_ref[b * H + h] = (prev * jnp.exp(a_last)
                                     + _dot_tn((bg * jnp.exp(a_last - a_col)).astype(BF), xdt))
                outs.append(y)
        y = jnp.concatenate(outs, axis=-1) + xs * dsk_ref[...]
        y = y * _silu(z)
        normed = []
        for grp in range(SSM_GROUPS):
            yg = y[:, grp * gw:(grp + 1) * gw]
            normed.append(yg * lax.rsqrt(jnp.mean(yg * yg, -1, keepdims=True) + LN_EPS))
        o_ref[b] = jnp.concatenate(normed, axis=-1) * ng_ref[...]


def _ssd(ssd, conv_w, conv_b, dt_bias, a_log, d_skip, norm_g, B, S):
    Q, H = SSM_CHUNK, SSM_HEADS
    n = S // Q
    W = ssd.shape[1]
    assert W >= SSM_INNER + SSM_CONV_DIM + LANE
    a = -jnp.exp(a_log.astype(F32))
    pad_r = lambda v: jnp.broadcast_to(jnp.pad(v, (0, CONV_PAD - H))[:, None], (CONV_PAD, Q)).astype(F32)
    full = lambda shape: pl.BlockSpec(shape, lambda c: (0,) * len(shape))
    return pl.pallas_call(
        _ssd_kernel,
        grid=(n,),
        in_specs=[
            pl.BlockSpec((B, Q, W), lambda c: (0, c, 0)),
            full((SSM_CONV, SSM_CONV_DIM)), full((1, SSM_CONV_DIM)),
            full((1, H)), full((1, H)), full((CONV_PAD, Q)), full((CONV_PAD, Q)),
            full((1, SSM_INNER)), full((1, SSM_INNER)),
        ],
        out_specs=pl.BlockSpec((B, Q, BRANCH_W), lambda c: (0, c, 0)),
        out_shape=jax.ShapeDtypeStruct((B, S, BRANCH_W), F32),
        scratch_shapes=[pltpu.VMEM((B * H, SSM_STATE, SSM_HEADDIM), F32),
                        pltpu.VMEM((B, Q + CONV_PAD, SSM_CONV_DIM), F32)],
        compiler_params=_params(("arbitrary",)),
        name="ssd",
    )(ssd.reshape(B, S, W), conv_w, conv_b[None, :], dt_bias[None, :], a[None, :], pad_r(dt_bias),
      pad_r(a), jnp.repeat(d_skip, SSM_HEADDIM)[None, :], norm_g[None, :]).reshape(B * S, BRANCH_W)


def _masked_attention(q_bf, kv_bf, mask, heads, dh):
    k_bf = kv_bf[:, :dh]
    ones_v = jnp.where(lax.broadcasted_iota(jnp.int32, kv_bf.shape, 1) < dh, jnp.ones((), kv_bf.dtype), kv_bf)
    outs = []
    for h in range(heads):
        s = jnp.where(mask, _dot_nt(q_bf[:, h * dh:(h + 1) * dh], k_bf), NEG_INF)
        m = jnp.max(s, -1, keepdims=True)
        m = jnp.where(m > NEG_INF, m, 0.0)
        lv = _dot(jnp.exp2((s - m).astype(BF)), ones_v)
        outs.append(lv[:, dh:] / jnp.maximum(lv[:, 0:1], 1e-30))
    return jnp.concatenate(outs, axis=-1)


def _count(m):
    return jnp.sum(jnp.where(m, 1.0, 0.0), -1, keepdims=True)


IDX_KEY_CHUNK = 256
SEL_BIG = float(2 ** 20)
CAUSAL_SEG = 1024
BISECT_PLAIN_STEPS = 26
BISECT_MAX_STEPS = 400


def _kth_threshold(load, nvalid, vmin, vmax, c_ge0, c_gt0, k):
    kf = float(k)
    short = nvalid <= kf
    up0 = c_ge0 >= kf
    tie0 = up0 & (c_gt0 < kf) & (c_ge0 > kf) & (nvalid > kf)
    lo0 = jnp.where(short, jnp.finfo(F32).min, jnp.where(up0, 0.0, vmin))
    hi0 = jnp.where(up0, vmax, 0.0)
    done0 = jnp.where(short | tie0 | (c_ge0 == kf), 1.0, 0.0)

    def split(lo, hi, done):
        mid = jnp.where(done > 0.5, lo, lo + (hi - lo) * 0.5)
        c = _count(load() >= mid)
        up = (c >= kf) & (done < 0.5)
        return jnp.where(up, mid, lo), jnp.where(up | (done > 0.5), hi, mid), jnp.where(up & (c == kf), 1.0, done)

    def plain_cond(st):
        return (st[0] < BISECT_PLAIN_STEPS) & (jnp.min(st[3]) < 0.5)

    def plain_step(st):
        it, lo, hi, done = st
        lo, hi, done = split(lo, hi, done)
        lo, hi, done = split(lo, hi, done)
        return it + 2, lo, hi, done

    _, lo, hi, done = lax.while_loop(plain_cond, plain_step, (jnp.int32(0), lo0, hi0, done0))

    def exact_cond(st):
        return (st[0] < BISECT_MAX_STEPS) & (jnp.min(st[4]) < 0.5)

    def exact_step(st):
        it, lo, hi, tie, done = st
        sc = load()
        cand = jnp.min(jnp.where(sc >= lo, sc, jnp.inf), -1, keepdims=True)
        fin = (_count(sc > cand) < kf) & (done < 0.5)
        lo = jnp.where(fin, cand, lo)
        tie = jnp.where(fin, 1.0, tie)
        done = jnp.where(fin, 1.0, done)
        lo, hi, done = split(lo, hi, done)
        return it + 1, lo, hi, tie, done

    _, thr, _, tie, _ = lax.while_loop(
        exact_cond, exact_step, (jnp.int32(0), lo, hi, jnp.where(tie0, 1.0, 0.0), done))
    return thr, tie


def _dsa_body(q_ref, aux_ref, kv_ref, ik_ref, o_ref, sc_ref, *, n_keep, E):
    Qb, CH = DSA_QB, IDX_KEY_CHUNK
    bi = pl.program_id(1)
    qpos = bi * Qb + lax.broadcasted_iota(jnp.int32, (Qb, 1), 0)
    hq = DSA_HEADS * DSA_DH
    iw = aux_ref[:, IDX_DH:IDX_DH + IDX_HEADS] * (IDX_HEADS ** -0.5) * (IDX_DH ** -0.5)
    iq = [q_ref[:, hq + h * IDX_DH:hq + (h + 1) * IDX_DH] for h in range(IDX_HEADS)]
    fold = lambda f, a: functools.reduce(f, [a[:, j:j + LANE] for j in range(0, CH, LANE)])
    hi_acc = jnp.full((Qb, LANE), NEG_INF, F32)
    lo_acc = jnp.full((Qb, LANE), jnp.inf, F32)
    for c0 in range(0, E, CH):
        ik = ik_ref[c0:c0 + CH, :IDX_DH]
        acc = jnp.maximum(_dot_nt(iq[0], ik), 0.0) * iw[:, 0:1]
        for h in range(1, IDX_HEADS):
            acc = acc + jnp.maximum(_dot_nt(iq[h], ik), 0.0) * iw[:, h:h + 1]
        causal = c0 + lax.broadcasted_iota(jnp.int32, (1, CH), 1) <= qpos
        masked = jnp.where(causal, acc, NEG_INF)
        sc_ref[:, c0:c0 + CH] = masked
        hi_acc = jnp.maximum(hi_acc, fold(jnp.maximum, masked))
        lo_acc = jnp.minimum(lo_acc, fold(jnp.minimum, jnp.where(causal, acc, jnp.inf)))
    load = lambda: sc_ref[:, 0:E]
    thr, tie = _kth_threshold(load, (qpos + 1).astype(F32), jnp.min(lo_acc, -1, keepdims=True),
                              jnp.max(hi_acc, -1, keepdims=True), _count(load() >= 0.0), _count(load() > 0.0),
                              n_keep)

    @pl.when(jnp.max(tie) > 0.5)
    def resolve_ties():
        sc = load()
        gt = sc > thr
        eqf = jnp.where(sc == thr, 1.0, 0.0)
        need = n_keep - _count(gt)
        ch = 256 if E % 256 == 0 else LANE
        before = jnp.where(lax.broadcasted_iota(jnp.int32, (ch, ch), 0)
                           < lax.broadcasted_iota(jnp.int32, (ch, ch), 1), 1.0, 0.0).astype(BF)
        run = jnp.zeros((Qb, 1), F32)
        take = []
        for c0 in range(0, E, ch):
            eqc = eqf[:, c0:c0 + ch]
            prefix = _dot(eqc.astype(BF), before) + run
            take.append(jnp.where(prefix < need, eqc, 0.0))
            run = run + jnp.sum(eqc, -1, keepdims=True)
        sc_ref[:, 0:E] = jnp.where(gt | (jnp.concatenate(take, axis=-1) > 0.5), jnp.inf, NEG_INF)

    o_ref[...] = _masked_attention(q_ref[:, :hq], kv_ref[0:E, :], load() >= thr, DSA_HEADS, DSA_DH)


def _for_causal_extent(S, seg, qb, body):
    bi = pl.program_id(1)
    assert S % seg == 0 and seg % qb == 0
    for e in range(seg, S + 1, seg):
        pl.when((bi >= (e - seg) // qb) & (bi < e // qb))(functools.partial(body, e))


def _dsa_kernel(q_ref, aux_ref, kv_ref, ik_ref, o_ref, sc_ref, *, n_keep, seg):
    _for_causal_extent(kv_ref.shape[0], seg, DSA_QB,
                       lambda e: _dsa_body(q_ref, aux_ref, kv_ref, ik_ref, o_ref, sc_ref, n_keep=n_keep, E=e))


def _dsa(dsa, dsa_aux, B, S):
    nb = S // DSA_QB
    n_keep = min(DSA_TOPK, S // 4)
    return pl.pallas_call(
        functools.partial(_dsa_kernel, n_keep=n_keep, seg=min(CAUSAL_SEG, S)),
        grid=(B, nb),
        in_specs=[
            pl.BlockSpec((DSA_QB, 4 * LANE), lambda b, i: (b * nb + i, 0)),
            pl.BlockSpec((DSA_QB, LANE), lambda b, i: (b * nb + i, 0)),
            pl.BlockSpec((S, LANE), lambda b, i: (b, 4)),
            pl.BlockSpec((S, LANE), lambda b, i: (b, 5)),
        ],
        out_specs=pl.BlockSpec((DSA_QB, BRANCH_W), lambda b, i: (b * nb + i, 0)),
        out_shape=jax.ShapeDtypeStruct((B * S, BRANCH_W), F32),
        scratch_shapes=[pltpu.VMEM((DSA_QB, S), F32)],
        compiler_params=_params(("parallel", "arbitrary")),
        name="dsa",
    )(dsa, dsa_aux, dsa, dsa)


def _cmp_kernel(g_ref, w1a_ref, w1b_ref, pos_ref, w1_ref, w2_ref, o_ref):
    g = g_ref[...].astype(BF)
    n = g.shape[0]
    a = _dot(g, w1a_ref[...])
    b = _dot(g, w1b_ref[...])
    posterm = _dot(pos_ref[...].astype(BF), w1_ref[...])[0:1, :]
    h = jax.nn.gelu(a + pltpu.roll(b, n - 1, 0) + posterm)
    o_ref[...] = _dot(h.astype(BF), w2_ref[...]).astype(o_ref.dtype)


def _compress(g2, cmp_w1, cmp_w2, cmp_pos):
    _, B, n, W = g2.shape
    Dh = NSA_DH
    half = W
    w1 = cmp_w1.astype(BF)
    pos8 = jnp.broadcast_to(cmp_pos.reshape(2, 1, CMP_LEN * Dh), (2, 8, CMP_LEN * Dh))
    return pl.pallas_call(
        _cmp_kernel,
        grid=(2, B),
        in_specs=[
            pl.BlockSpec((None, None, n, W), lambda i, b: (i, b, 0, 0)),
            pl.BlockSpec((None, half, Dh), lambda i, b: (i, 0, 0)),
            pl.BlockSpec((None, half, Dh), lambda i, b: (i, 1, 0)),
            pl.BlockSpec((None, 8, CMP_LEN * Dh), lambda i, b: (i, 0, 0)),
            pl.BlockSpec((None, CMP_LEN * Dh, Dh), lambda i, b: (i, 0, 0)),
            pl.BlockSpec((None, Dh, Dh), lambda i, b: (i, 0, 0)),
        ],
        out_specs=pl.BlockSpec((None, None, n, Dh), lambda i, b: (i, b, 0, 0)),
        out_shape=jax.ShapeDtypeStruct((2, B, n, Dh), BF),
        compiler_params=_params(("parallel", "parallel")),
        name="nsa_compress",
    )(g2, w1, w1, pos8, w1, cmp_w2.astype(BF))


def _nsa_kernel(q_ref, gt_ref, kvc_ref, sel_ref, win_ref, exp_ref, o_ref, osel_ref, *, n_top, seg):
    Qb, H, Dh = NSA_QB, NSA_HEADS, NSA_DH
    S = sel_ref.shape[0]
    n_cmp = kvc_ref.shape[1]
    n_blk = S // SEL_LEN
    bi = pl.program_id(1)
    qpos = bi * Qb + lax.broadcasted_iota(jnp.int32, (Qb, 1), 0)
    q = q_ref[...]

    kc = kvc_ref[0]
    vc = kvc_ref[1]
    cidx = lax.broadcasted_iota(jnp.int32, (1, n_cmp), 1)
    vis = cidx * CMP_STRIDE + (CMP_LEN - 1) <= qpos
    o_cmp = []
    psum = jnp.zeros((Qb, n_cmp), F32)
    for h in range(H):
        s = jnp.where(vis, _dot_nt(q[:, h * Dh:(h + 1) * Dh], kc), NEG_INF)
        m = jnp.max(s, -1, keepdims=True)
        m = jnp.where(m > NEG_INF, m, 0.0)
        e = jnp.exp2(s - m)
        p = e / jnp.maximum(jnp.sum(e, -1, keepdims=True), 1e-30)
        psum = psum + p
        o_cmp.append(_dot(p.astype(BF), vc))

    js = lax.broadcasted_iota(jnp.int32, (n_blk, 1), 0) * SEL_LEN
    cs = lax.broadcasted_iota(jnp.int32, (1, n_cmp), 1) * CMP_STRIDE
    ov = jnp.maximum(jnp.minimum(cs + CMP_LEN, js + SEL_LEN) - jnp.maximum(cs, js), 0).astype(F32) / CMP_LEN
    ov = ov.astype(BF)
    imp = sum(_dot_nt(ov, t) for t in _split3(psum))
    blk = lax.broadcasted_iota(jnp.int32, (n_blk, 1), 0)
    sel_shift = SEL_LEN.bit_length() - 1
    cur = jnp.right_shift(bi * Qb + lax.broadcasted_iota(jnp.int32, (1, Qb), 1), sel_shift)
    forced = (blk == 0) | (blk == cur) | (blk == cur - 1)
    imp = jnp.where(blk <= cur, jnp.where(forced, jnp.inf, imp), NEG_INF)
    rank = jnp.zeros((n_blk, Qb), F32)
    for j in range(n_blk):
        row = imp[j:j + 1, :]
        rank = rank + jnp.where((row > imp) | ((row == imp) & (blk > j)), 1.0, 0.0)
    sub = lax.broadcasted_iota(jnp.int32, (LANE - n_blk, Qb), 0)
    chosen = jnp.concatenate([jnp.where(rank < n_top, SEL_BIG, 0.0), jnp.where(sub < 2, 1.0, 0.0)], axis=0)
    chosen = chosen.T.astype(BF)
    bound = SEL_BIG - 0.5 - qpos.astype(F32)

    def selected(e):
        mask = _dot(chosen, exp_ref[:, 0:e]) > bound
        osel_ref[...] = _masked_attention(q, sel_ref[0:e, :], mask, H, Dh)

    _for_causal_extent(S, seg, Qb, selected)
    o_sel = osel_ref[...]

    wlen = WINDOW + Qb
    start = pl.multiple_of(jnp.maximum(bi * Qb - WINDOW, 0), Qb)
    kwin = win_ref[pl.ds(start, wlen), :]
    dlt = qpos - (start + lax.broadcasted_iota(jnp.int32, (1, wlen), 1))
    o_win = _masked_attention(q, kwin, (dlt >= 0) & (dlt < WINDOW), H, Dh)

    g = jax.nn.sigmoid(gt_ref[:, :3 * H])
    outs = []
    for h in range(H):
        outs.append(g[:, 3 * h:3 * h + 1] * o_cmp[h]
                    + g[:, 3 * h + 1:3 * h + 2] * o_sel[:, h * Dh:(h + 1) * Dh]
                    + g[:, 3 * h + 2:3 * h + 3] * o_win[:, h * Dh:(h + 1) * Dh])
    o_ref[...] = jnp.concatenate(outs, axis=-1)


def _nsa(nsa, nsa_aux, kvc, B, S):
    nb = S // NSA_QB
    n_cmp = kvc.shape[2]
    n_blk = S // SEL_LEN
    n_top = min(SEL_TOPN, n_blk)
    assert S >= WINDOW + NSA_QB and n_blk + 2 <= LANE and S < SEL_BIG and n_blk <= 256
    kpos = np.arange(S)
    expand = np.zeros((LANE, S), np.float32)
    expand[:n_blk] = kpos[None, :] // SEL_LEN == np.arange(n_blk)[:, None]
    expand[n_blk] = -(kpos // SEL_LEN * SEL_LEN)
    expand[n_blk + 1] = -(kpos % SEL_LEN)
    expand = jnp.asarray(expand, BF)
    return pl.pallas_call(
        functools.partial(_nsa_kernel, n_top=n_top, seg=min(CAUSAL_SEG, S)),
        grid=(B, nb),
        in_specs=[
            pl.BlockSpec((NSA_QB, 2 * LANE), lambda b, i: (b * nb + i, 0)),
            pl.BlockSpec((NSA_QB, LANE), lambda b, i: (b * nb + i, 0)),
            pl.BlockSpec((2, None, n_cmp, NSA_DH), lambda b, i: (0, b, 0, 0)),
            pl.BlockSpec((S, LANE), lambda b, i: (b, 3)),
            pl.BlockSpec((S, LANE), lambda b, i: (b, 4)),
            pl.BlockSpec((LANE, S), lambda b, i: (0, 0)),
        ],
        out_specs=pl.BlockSpec((NSA_QB, BRANCH_W), lambda b, i: (b * nb + i, 0)),
        out_shape=jax.ShapeDtypeStruct((B * S, BRANCH_W), F32),
        scratch_shapes=[pltpu.VMEM((NSA_QB, BRANCH_W), F32)],
        compiler_params=_params(("parallel", "arbitrary")),
        name="nsa",
    )(nsa, nsa_aux, kvc, nsa, nsa, expand)


def _merge_kernel(x_ref, y0, y1, y2, y3, wg_ref, wb_ref, wo_ref, g_ref, b_ref, o_ref, *, alpha):
    x = x_ref[...]
    xb = x.astype(BF)
    D = x.shape[1]
    merged = jnp.zeros_like(x)
    for n, y_ref in enumerate((y0, y1, y2, y3)):
        gate = jax.nn.sigmoid(_dot(xb, wg_ref[:, n * D:(n + 1) * D]))
        merged = merged + gate * _dot(y_ref[...].astype(BF), wb_ref[n])
    o_ref[...] = _layer_norm(alpha * x + _dot(merged.astype(BF), wo_ref[...]), g_ref[...], b_ref[...])


def _merge(x, ys, w_gate, w_branch, w_out, layer, g, b, alpha, tm=512):
    T, D = x.shape
    full = lambda shape: pl.BlockSpec(shape, lambda i: (0,) * len(shape), pipeline_mode=pl.Buffered(1))
    return pl.pallas_call(
        functools.partial(_merge_kernel, alpha=alpha),
        grid=(T // tm,),
        in_specs=[pl.BlockSpec((tm, D), lambda i: (i, 0))]
        + [pl.BlockSpec((tm, BRANCH_W), lambda i: (i, 0))] * N_BRANCH
        + [full(w_gate.shape), _layer_spec(w_branch, layer), _layer_spec(w_out, layer), full((1, D)), full((1, D))],
        out_specs=pl.BlockSpec((tm, D), lambda i: (i, 0)),
        out_shape=jax.ShapeDtypeStruct((T, D), F32),
        compiler_params=_params(("parallel",)),
        name="merge",
    )(x, *ys, w_gate, w_branch, w_out, g, b)


def _matmul_kernel(a_ref, w_ref, o_ref):
    o_ref[...] = _dot(a_ref[...].astype(BF), w_ref[...]).astype(o_ref.dtype)


def _matmul(a, w, layer, tm):
    M, K = a.shape
    N = w.shape[2]
    return pl.pallas_call(
        _matmul_kernel,
        grid=(M // tm,),
        in_specs=[pl.BlockSpec((tm, K), lambda i: (i, 0)), _layer_spec(w, layer)],
        out_specs=pl.BlockSpec((tm, N), lambda i: (i, 0)),
        out_shape=jax.ShapeDtypeStruct((M, N), BF),
        compiler_params=_params(("parallel",)),
        name="kv_proj",
    )(a, w)


def _xattn_kernel(x_ref, kv_ref, wq_ref, wo_ref, g_ref, b_ref, o_ref, *, alpha):
    x = x_ref[...]
    D = x.shape[1]
    dh = D // X_HEADS
    q = _dot(x.astype(BF), wq_ref[...])
    outs = []
    for h in range(X_HEADS):
        k = kv_ref[:, h * dh:(h + 1) * dh]
        v = kv_ref[:, D + h * dh:D + (h + 1) * dh]
        s = _dot_nt(q[:, h * dh:(h + 1) * dh].astype(BF), k) * (dh ** -0.5)
        e = jnp.exp(s - jnp.max(s, -1, keepdims=True))
        outs.append(_dot(e.astype(BF), v) / jnp.sum(e, -1, keepdims=True))
    att = jnp.concatenate(outs, axis=-1).astype(BF)
    o_ref[...] = _layer_norm(alpha * x + _dot(att, wo_ref[...]), g_ref[...], b_ref[...])


def _xattn(x, kv, wq, wo, layer, g, b, alpha, S, M, tm=1024):
    T, D = x.shape
    per = S // tm
    full = lambda shape: pl.BlockSpec(shape, lambda i: (0,) * len(shape))
    return pl.pallas_call(
        functools.partial(_xattn_kernel, alpha=alpha),
        grid=(T // tm,),
        in_specs=[pl.BlockSpec((tm, D), lambda i: (i, 0)),
                  pl.BlockSpec((M, 2 * D), lambda i: (i // per, 0)),
                  _layer_spec(wq, layer), _layer_spec(wo, layer), full((1, D)), full((1, D))],
        out_specs=pl.BlockSpec((tm, D), lambda i: (i, 0)),
        out_shape=jax.ShapeDtypeStruct((T, D), F32),
        compiler_params=_params(("parallel",)),
        name="xattn",
    )(x, kv, wq, wo, g, b)


def _rope_tables(S):
    pos = jnp.arange(S).astype(F32)

    def base(rot, theta):
        half = rot // 2
        inv = jnp.power(jnp.float32(theta), -2.0 * jnp.arange(half, dtype=F32) / rot)
        ang = pos[:, None] * inv[None, :]
        return jnp.cos(ang), jnp.sin(ang)

    def tile(hd, rot, theta, width, scale=1.0):
        c, s = base(rot, theta)
        lane = np.arange(LANE)
        jj = lane % hd
        half = rot // 2
        first = (jj < half) & (lane < width)
        second = (jj >= half) & (jj < rot) & (lane < width)
        idx = jj % half
        ct = jnp.where((first | second)[None, :], c[:, idx], 1.0) * scale
        s_hi = jnp.where(first[None, :], -s[:, idx], 0.0) * scale
        s_lo = jnp.where(second[None, :], s[:, idx], 0.0) * scale
        return ct, s_hi, s_lo

    q_scale = DSA_DH ** -0.5 * LOG2E
    tabs = [
        tile(RET_DK, RET_DK, RET_THETA, LANE),
        tile(RET_DK, RET_DK, RET_THETA, LANE, RET_DK ** -0.5),
        tile(DSA_DH, DSA_DH // ROPE_FRAC, ROPE_THETA, LANE, q_scale),
        tile(DSA_DH, DSA_DH // ROPE_FRAC, ROPE_THETA, DSA_DH),
        tile(IDX_DH, IDX_DH // ROPE_FRAC, ROPE_THETA, LANE),
        tile(IDX_DH, IDX_DH // ROPE_FRAC, ROPE_THETA, IDX_DH),
    ]
    return tuple(jnp.concatenate([t[i] for t in tabs], 1) for i in range(3))


def _pack_w_in(wb, layer):
    D = wb.shape[1]
    main = []
    for _, segs, width in _GROUPS:
        used = 0
        for s in segs:
            start, n = _SEG[s]
            main.append(wb[layer, :, start:start + n])
            used += n
        main.append(jnp.zeros((D, width - used), BF))
    return jnp.concatenate(main, 1), wb[layer, :, GATE_START:]


def kernel(x, mem, ln_g, ln_b, ffn1_w_gu, ffn1_w_down, w_in, cmp_w1, cmp_w2, cmp_pos, conv_w, conv_b,
           dt_bias, a_log, d_skip, ssm_norm_g, w_branch, w_out, xattn_wq, xattn_wkv, xattn_wo,
           ffn2_w_gu, ffn2_w_down):
    B, S, D = x.shape
    M = mem.shape[1]
    depth = ln_g.shape[0]
    alpha = (2 * depth) ** 0.25
    tables = _rope_tables(S)
    h = x
    mem2 = mem.reshape(B * M, D)
    bf = lambda w: w.astype(BF)
    ffn1_w_gu, ffn1_w_down, ffn2_w_gu, ffn2_w_down = bf(ffn1_w_gu), bf(ffn1_w_down), bf(ffn2_w_gu), bf(ffn2_w_down)
    w_in, w_branch, w_out = bf(w_in), bf(w_branch), bf(w_out)
    xattn_wq, xattn_wkv, xattn_wo = bf(xattn_wq), bf(xattn_wkv), bf(xattn_wo)
    for l in range(depth):
        lg = lambda i: ln_g[l, i][None, :]
        lb = lambda i: ln_b[l, i][None, :]
        h = _ffn(h, ffn1_w_gu, ffn1_w_down, l, lg(0), lb(0), alpha)

        wm, w_gate = _pack_w_in(w_in, l)
        ret, dsa, nsa, ssd, dsa_aux, nsa_aux = _inproj(h, wm, tables, S)
        y_ret = _retention(ret, B, S)
        y_dsa = _dsa(dsa, dsa_aux, B, S)
        g2 = jnp.stack([nsa[:, 2 * LANE:2 * LANE + NSA_DH], nsa[:, 2 * LANE + NSA_DH:3 * LANE]])
        kvc = _compress(g2.reshape(2, B, S // CMP_STRIDE, CMP_STRIDE * NSA_DH), cmp_w1[l], cmp_w2[l], cmp_pos[l])
        y_nsa = _nsa(nsa, nsa_aux, kvc, B, S)
        y_ssd = _ssd(ssd, conv_w[l], conv_b[l], dt_bias[l], a_log[l], d_skip[l], ssm_norm_g[l], B, S)
        h = _merge(h, (y_ret, y_dsa, y_nsa, y_ssd), w_gate, w_branch, w_out, l, lg(1), lb(1), alpha)

        kv = _matmul(mem2, xattn_wkv, l, tm=min(256, B * M))
        h = _xattn(h, kv, xattn_wq, xattn_wo, l, lg(2), lb(2), alpha, S, M)
        h = _ffn(h, ffn2_w_gu, ffn2_w_down, l, lg(3), lb(3), alpha, out_batch=B if l == depth - 1 else None)
    return h
```
